```python
import jax, jax.numpy as jnp
from jax import lax
import numpy as np

D_MODEL = 2048
BATCH = 2
SEQ = 4096
DEPTH = 1

N_HEADS = 16
QK_NOPE_DIM = 128
QK_ROPE_DIM = 64
QK_HEAD_DIM = QK_NOPE_DIM + QK_ROPE_DIM
V_HEAD_DIM = 128
Q_LORA_RANK = 512
KV_LORA_RANK = 512
ROPE_THETA = 10000.0
Q_BLOCK = 128
POOL_WINDOWS = (2, 4, 8, 16)
POOL_GROUPS = 4
POOL_GROUP_DIM = D_MODEL // 8
POOL_WIDTH = POOL_GROUPS * POOL_GROUP_DIM
N_BRANCHES = 2
D_FF = (8 * D_MODEL + 3 * 256 - 1) // (3 * 256) * 256
EPS = 1e-6

IN_SPLITS = (Q_LORA_RANK, KV_LORA_RANK, QK_ROPE_DIM, POOL_WIDTH, N_BRANCHES * D_MODEL)
D_IN = sum(IN_SPLITS)

kernel_name = "hybrid_pool_mla_gated_block"


def rmsnorm(x, g):
    x32 = x.astype(jnp.float32)
    inv = lax.rsqrt(jnp.mean(x32 * x32, axis=-1, keepdims=True) + EPS)
    return (x32 * inv).astype(x.dtype) * g


def apply_rope(t, positions):
    half = QK_ROPE_DIM // 2
    inv_freq = ROPE_THETA ** (-jnp.arange(half, dtype=jnp.float32) / half)
    ang = positions.astype(jnp.float32)[..., None] * inv_freq
    cos = jnp.cos(ang)[:, :, None, :]
    sin = jnp.sin(ang)[:, :, None, :]
    t32 = t.astype(jnp.float32)
    t1, t2 = t32[..., :half], t32[..., half:]
    out = jnp.concatenate([t1 * cos - t2 * sin, t2 * cos + t1 * sin], axis=-1)
    return out.astype(t.dtype)


def causal_multiscale_pool(u):
    B, S, _ = u.shape
    ug = u.reshape(B, S, POOL_GROUPS, POOL_GROUP_DIM)
    cs = jnp.cumsum(ug.astype(jnp.float32), axis=1)
    cs = jnp.pad(cs, ((0, 0), (1, 0), (0, 0), (0, 0)))
    t = jnp.arange(S)
    outs = []
    for g, w in enumerate(POOL_WINDOWS):
        cs_g = cs[:, :, g]
        lo = jnp.maximum(t + 1 - w, 0)
        window_sum = cs_g[:, t + 1] - cs_g[:, lo]
        count = (t + 1 - lo).astype(jnp.float32)
        outs.append(window_sum / count[None, :, None] - ug[:, :, g].astype(jnp.float32))
    return jnp.stack(outs, axis=2).astype(u.dtype)


def causal_block_attention(q, k, v):
    B, S, H, Dq = q.shape
    nb = S // Q_BLOCK
    qb = q.reshape(B, nb, Q_BLOCK, H, Dq).transpose(1, 0, 2, 3, 4)
    kpos = jnp.arange(S)
    scale = QK_HEAD_DIM ** -0.5

    def one_block(args):
        i, qi = args
        s = jnp.einsum('bqhd,bkhd->bhqk', qi, k, preferred_element_type=jnp.float32) * scale
        qpos = i * Q_BLOCK + jnp.arange(Q_BLOCK)
        mask = kpos[None, :] <= qpos[:, None]
        s = jnp.where(mask[None, None], s, -jnp.inf)
        p = jax.nn.softmax(s, axis=-1).astype(v.dtype)
        return jnp.einsum('bhqk,bkhd->bqhd', p, v)

    out = lax.map(one_block, (jnp.arange(nb), qb))
    return out.transpose(1, 0, 2, 3, 4).reshape(B, S, H, v.shape[-1])


def setup_inputs(seed: int = 0) -> dict:
    key = jax.random.key(seed)
    ks = jax.random.split(key, 24)

    def w(k, shape, fan_in):
        return jax.random.normal(k, shape, jnp.float32) * fan_in ** -0.5

    def gain(k, shape):
        return 1.0 + 0.02 * jax.random.normal(k, shape, jnp.float32)

    L = DEPTH
    return {
        "x": jax.random.normal(ks[0], (BATCH, SEQ, D_MODEL), jnp.float32),
        "positions": jnp.broadcast_to(jnp.arange(SEQ, dtype=jnp.int32), (BATCH, SEQ)),
        "attn_norm_g": gain(ks[1], (L, D_MODEL)),
        "w_in": w(ks[2], (L, D_MODEL, D_IN), D_MODEL),
        "b_gate": 0.02 * jax.random.normal(ks[3], (L, N_BRANCHES * D_MODEL), jnp.float32),
        "q_a_norm_g": gain(ks[4], (L, Q_LORA_RANK)),
        "w_q_b": w(ks[5], (L, Q_LORA_RANK, N_HEADS * QK_HEAD_DIM), Q_LORA_RANK),
        "kv_a_norm_g": gain(ks[6], (L, KV_LORA_RANK)),
        "w_kv_b": w(ks[7], (L, KV_LORA_RANK, N_HEADS * (QK_NOPE_DIM + V_HEAD_DIM)), KV_LORA_RANK),
        "q_norm_g": gain(ks[8], (L, QK_HEAD_DIM)),
        "k_norm_g": gain(ks[9], (L, QK_HEAD_DIM)),
        "w_attn_o": w(ks[10], (L, N_HEADS * V_HEAD_DIM, D_MODEL), N_HEADS * V_HEAD_DIM),
        "w_pool_grp": w(ks[11], (L, POOL_GROUPS, POOL_GROUP_DIM, POOL_GROUP_DIM), POOL_GROUP_DIM),
        "pool_scale": gain(ks[12], (L, POOL_GROUPS, POOL_GROUP_DIM)),
        "w_pool_o": w(ks[13], (L, POOL_WIDTH, D_MODEL), POOL_WIDTH),
        "w_out": w(ks[14], (L, D_MODEL, D_MODEL), D_MODEL),
        "ffn_norm_g": gain(ks[15], (L, D_MODEL)),
        "w_ffn_gate": w(ks[16], (L, D_MODEL, D_FF), D_MODEL),
        "w_ffn_up": w(ks[17], (L, D_MODEL, D_FF), D_MODEL),
        "w_ffn_down": w(ks[18], (L, D_FF, D_MODEL), D_FF),
    }


def reference(x, positions, attn_norm_g, w_in, b_gate, q_a_norm_g, w_q_b, kv_a_norm_g, w_kv_b,
              q_norm_g, k_norm_g, w_attn_o, w_pool_grp, pool_scale, w_pool_o, w_out,
              ffn_norm_g, w_ffn_gate, w_ffn_up, w_ffn_down):
    B, S, _ = x.shape
    offsets = list(np.cumsum(IN_SPLITS)[:-1])
    for l in range(DEPTH):
        h = rmsnorm(x, attn_norm_g[l])
        proj = h @ w_in[l]
        c_q, c_kv, k_rope, u_pool, gate_logits = jnp.split(proj, offsets, axis=-1)
        gates = jax.nn.sigmoid(gate_logits + b_gate[l])
        g_pool, g_attn = gates[..., :D_MODEL], gates[..., D_MODEL:]

        pooled = causal_multiscale_pool(u_pool)
        pooled = jnp.einsum('bsgc,gcd->bsgd', pooled, w_pool_grp[l]) * pool_scale[l]
        y_pool = pooled.reshape(B, S, POOL_WIDTH) @ w_pool_o[l]

        q = (rmsnorm(c_q, q_a_norm_g[l]) @ w_q_b[l]).reshape(B, S, N_HEADS, QK_HEAD_DIM)
        kv = (rmsnorm(c_kv, kv_a_norm_g[l]) @ w_kv_b[l]).reshape(B, S, N_HEADS, QK_NOPE_DIM + V_HEAD_DIM)
        k_nope, v = kv[..., :QK_NOPE_DIM], kv[..., QK_NOPE_DIM:]
        k_rope_h = jnp.broadcast_to(k_rope[:, :, None, :], (B, S, N_HEADS, QK_ROPE_DIM))
        k = jnp.concatenate([k_nope, k_rope_h], axis=-1)
        q = rmsnorm(q, q_norm_g[l])
        k = rmsnorm(k, k_norm_g[l])
        q = jnp.concatenate([q[..., :QK_NOPE_DIM], apply_rope(q[..., QK_NOPE_DIM:], positions)], axis=-1)
        k = jnp.concatenate([k[..., :QK_NOPE_DIM], apply_rope(k[..., QK_NOPE_DIM:], positions)], axis=-1)
        attn = causal_block_attention(q, k, v)
        y_attn = attn.reshape(B, S, N_HEADS * V_HEAD_DIM) @ w_attn_o[l]

        mixed = g_pool * y_pool + g_attn * y_attn
        x = x + mixed @ w_out[l]

        h2 = rmsnorm(x, ffn_norm_g[l])
        x = x + (jax.nn.silu(h2 @ w_ffn_gate[l]) * (h2 @ w_ffn_up[l])) @ w_ffn_down[l]
    return x
```

```python
import functools

import jax
import jax.numpy as jnp
from jax import lax
from jax.experimental import pallas as pl
from jax.experimental.pallas import tpu as pltpu

F32 = jnp.float32
BF16 = jnp.bfloat16

N_HEADS = 16
QK_NOPE_DIM = 128
QK_ROPE_DIM = 64
QK_HEAD_DIM = QK_NOPE_DIM + QK_ROPE_DIM
V_HEAD_DIM = 128
Q_LORA_RANK = 512
KV_LORA_RANK = 512
ROPE_THETA = 10000.0
POOL_WINDOWS = (2, 4, 8, 16)
POOL_GROUPS = 4
EPS = 1e-6

LANES = 128
HEAD_SLOT = 2 * LANES
ROPE_HALF = QK_ROPE_DIM // 2
POOL_HALO = 16
VMEM_LIMIT = 56 * 1024 * 1024


def _params(*sem):
    return pltpu.CompilerParams(dimension_semantics=sem, vmem_limit_bytes=VMEM_LIMIT)


def _dot(a, b):
    return jnp.dot(a, b, preferred_element_type=F32)


def _rms_bf16(x, g):
    ms = jnp.mean(x * x, axis=-1, keepdims=True)
    return (x * lax.rsqrt(ms + EPS) * g).astype(BF16)


def _rope(t, cos, sin):
    return t * cos + pltpu.roll(t, 2 * ROPE_HALF, 1) * sin


def _latent_kernel(x_ref, g_ref, w_ref, gl_ref, wkr_ref, lat_ref, kr_ref, h_ref):
    @pl.when(pl.program_id(1) == 0)
    def _():
        h_ref[...] = _rms_bf16(x_ref[...], g_ref[...])
        kr_ref[...] = _dot(h_ref[...], wkr_ref[...])

    c = _dot(h_ref[...], w_ref[...])
    lat_ref[...] = _rms_bf16(c, gl_ref[...])


def _latent_call(x, g, w_lat, g_lat, w_kr, tm):
    m, d = x.shape
    rank = Q_LORA_RANK
    n_lat = w_lat.shape[1]
    return pl.pallas_call(
        _latent_kernel,
        grid=(m // tm, n_lat // rank),
        in_specs=[
            pl.BlockSpec((tm, d), lambda i, j: (i, 0)),
            pl.BlockSpec((1, d), lambda i, j: (0, 0)),
            pl.BlockSpec((d, rank), lambda i, j: (0, j)),
            pl.BlockSpec((1, rank), lambda i, j: (0, j)),
            pl.BlockSpec((d, LANES), lambda i, j: (0, 0)),
        ],
        out_specs=[
            pl.BlockSpec((tm, rank), lambda i, j: (i, j)),
            pl.BlockSpec((tm, LANES), lambda i, j: (i, 0)),
        ],
        out_shape=[
            jax.ShapeDtypeStruct((m, n_lat), BF16),
            jax.ShapeDtypeStruct((m, LANES), F32),
        ],
        scratch_shapes=[pltpu.VMEM((tm, d), BF16)],
        compiler_params=_params("parallel", "arbitrary"),
        name="latent",
    )(x, g, w_lat, g_lat, w_kr)


def _norm_matmul_kernel(x_ref, g_ref, w_ref, b_ref, o_ref, h_ref, *, gate):
    @pl.when(pl.program_id(1) == 0)
    def _():
        h_ref[...] = _rms_bf16(x_ref[...], g_ref[...])

    acc = _dot(h_ref[...], w_ref[...])
    if gate:
        acc = jax.nn.sigmoid(acc + b_ref[...])
    o_ref[...] = acc.astype(o_ref.dtype)


def _norm_matmul_call(x, g, w, b, tm, tn, gate, name):
    m, d = x.shape
    n = w.shape[1]
    return pl.pallas_call(
        functools.partial(_norm_matmul_kernel, gate=gate),
        grid=(m // tm, n // tn),
        in_specs=[
            pl.BlockSpec((tm, d), lambda i, j: (i, 0)),
            pl.BlockSpec((1, d), lambda i, j: (0, 0)),
            pl.BlockSpec((d, tn), lambda i, j: (0, j)),
            pl.BlockSpec((1, tn), lambda i, j: (0, j)),
        ],
        out_specs=pl.BlockSpec((tm, tn), lambda i, j: (i, j)),
        out_shape=jax.ShapeDtypeStruct((m, n), F32),
        scratch_shapes=[pltpu.VMEM((tm, d), BF16)],
        compiler_params=_params("parallel", "arbitrary"),
        name=name,
    )(x, g, w, b)


def _pool_kernel(u_ref, halo_ref, wg_ref, sc_ref, o_ref, *, tm, seq):
    t0 = (pl.program_id(0) * tm) % seq
    halo = jnp.where(t0 > 0, halo_ref[...], 0.0)
    u = u_ref[...]
    ext = jnp.concatenate([halo, u], axis=0)
    pos = lax.broadcasted_iota(jnp.int32, (tm, 1), 0) + t0
    gd = wg_ref.shape[1]
    for g, w in enumerate(POOL_WINDOWS):
        cols = slice(g * gd, (g + 1) * gd)
        a = ext[:, cols]
        shift = 1
        while shift < w:
            a = a + pltpu.roll(a, shift, 0)
            shift *= 2
        count = jnp.minimum(pos + 1, w).astype(F32)
        pooled = a[POOL_HALO:, :] / count - u[:, cols]
        y = _dot(pooled.astype(BF16), wg_ref[g]) * sc_ref[:, cols]
        o_ref[:, cols] = y.astype(BF16)


def _pool_call(u, w_grp, scale, tm, seq):
    m, width = u.shape
    halo_blocks = tm // POOL_HALO
    return pl.pallas_call(
        functools.partial(_pool_kernel, tm=tm, seq=seq),
        grid=(m // tm,),
        in_specs=[
            pl.BlockSpec((tm, width), lambda i: (i, 0)),
            pl.BlockSpec((POOL_HALO, width), lambda i: (jnp.maximum(i * halo_blocks - 1, 0), 0)),
            pl.BlockSpec(w_grp.shape, lambda i: (0, 0, 0)),
            pl.BlockSpec((1, width), lambda i: (0, 0)),
        ],
        out_specs=pl.BlockSpec((tm, width), lambda i: (i, 0)),
        out_shape=jax.ShapeDtypeStruct((m, width), BF16),
        compiler_params=_params("parallel"),
        name="pool",
    )(u, u, w_grp, scale)


def _qkv_kernel(cq_ref, ckv_ref, kr_ref, cos_ref, sin_ref, wq_ref, wkv_ref,
                gq_ref, gk_ref, q_ref, k_ref, v_ref):
    cos = cos_ref[...]
    sin = sin_ref[...]
    inv_dim = 1.0 / QK_HEAD_DIM
    qf = _dot(cq_ref[...], wq_ref[...])
    kv = _dot(ckv_ref[...], wkv_ref[...])
    gq_n, gq_r = gq_ref[:, :LANES], gq_ref[:, LANES:]
    gk_n, gk_r = gk_ref[:, :LANES], gk_ref[:, LANES:]
    kr = kr_ref[...]
    ss_kr = jnp.sum(kr * kr, axis=-1, keepdims=True)
    kr_rot = _rope(kr * gk_r, cos, sin)
    for h in range(N_HEADS):
        lo = h * HEAD_SLOT
        qn = qf[:, lo:lo + LANES]
        qr = qf[:, lo + LANES:lo + HEAD_SLOT]
        ss = jnp.sum(qn * qn, axis=-1, keepdims=True) + jnp.sum(qr * qr, axis=-1, keepdims=True)
        inv = lax.rsqrt(ss * inv_dim + EPS)
        q_ref[:, lo:lo + LANES] = (qn * inv * gq_n).astype(BF16)
        q_ref[:, lo + LANES:lo + HEAD_SLOT] = _rope(qr * inv * gq_r, cos, sin).astype(BF16)
        kn = kv[:, h * LANES:(h + 1) * LANES]
        ssk = jnp.sum(kn * kn, axis=-1, keepdims=True) + ss_kr
        invk = lax.rsqrt(ssk * inv_dim + EPS)
        k_ref[:, lo:lo + LANES] = (kn * invk * gk_n).astype(BF16)
        k_ref[:, lo + LANES:lo + HEAD_SLOT] = (kr_rot * invk).astype(BF16)
    v_ref[...] = kv[:, N_HEADS * LANES:].astype(BF16)


def _qkv_call(lat, kr, cos, sin, wq, wkv, gq, gk, tm):
    m = lat.shape[0]
    rank = Q_LORA_RANK
    wide = N_HEADS * HEAD_SLOT
    vw = N_HEADS * V_HEAD_DIM
    row = lambda i: (i, 0)
    fixed = lambda i: (0, 0)
    return pl.pallas_call(
        _qkv_kernel,
        grid=(m // tm,),
        in_specs=[
            pl.BlockSpec((tm, rank), lambda i: (i, 0)),
            pl.BlockSpec((tm, rank), lambda i: (i, 1)),
            pl.BlockSpec((tm, LANES), row),
            pl.BlockSpec((tm, LANES), row),
            pl.BlockSpec((tm, LANES), row),
            pl.BlockSpec(wq.shape, fixed),
            pl.BlockSpec(wkv.shape, fixed),
            pl.BlockSpec((1, HEAD_SLOT), fixed),
            pl.BlockSpec((1, HEAD_SLOT), fixed),
        ],
        out_specs=[
            pl.BlockSpec((tm, wide), row),
            pl.BlockSpec((tm, wide), row),
            pl.BlockSpec((tm, vw), row),
        ],
        out_shape=[
            jax.ShapeDtypeStruct((m, wide), BF16),
            jax.ShapeDtypeStruct((m, wide), BF16),
            jax.ShapeDtypeStruct((m, vw), BF16),
        ],
        compiler_params=_params("parallel"),
        name="qkv",
    )(lat, lat, kr, cos, sin, wq, wkv, gq, gk)


def _flash_kernel(q_ref, k_ref, v_ref, o_ref, *, tq):
    qi = pl.program_id(2)
    q = q_ref[...]

    def step(j, carry, diagonal):
        m_prev, l_prev, acc = carry
        start = pl.multiple_of(j * tq, tq)
        k = k_ref[pl.ds(start, tq), :]
        v = v_ref[pl.ds(start, tq), :]
        s = lax.dot_general(q, k, (((1,), (1,)), ((), ())), preferred_element_type=F32)
        if diagonal:
            row = lax.broadcasted_iota(jnp.int32, (tq, tq), 0)
            col = lax.broadcasted_iota(jnp.int32, (tq, tq), 1)
            s = jnp.where(col <= row, s, -jnp.inf)
        m_new = jnp.maximum(m_prev, jnp.max(s, axis=-1, keepdims=True))
        p = jnp.exp(s - m_new)
        alpha = jnp.exp(m_prev - m_new)
        l_new = alpha * l_prev + jnp.sum(p, axis=-1, keepdims=True)
        acc = alpha * acc + _dot(p.astype(BF16), v)
        return m_new, l_new, acc

    init = (jnp.full((tq, 1), -jnp.inf, F32), jnp.zeros((tq, 1), F32),
            jnp.zeros((tq, V_HEAD_DIM), F32))
    carry = lax.fori_loop(0, qi, lambda j, c: step(j, c, False), init)
    _, l, acc = step(qi, carry, True)
    o_ref[...] = (acc / l).astype(BF16)


def _flash_call(q, k, v, tq):
    b, s, _ = q.shape
    return pl.pallas_call(
        functools.partial(_flash_kernel, tq=tq),
        grid=(b, N_HEADS, s // tq),
        in_specs=[
            pl.BlockSpec((None, tq, HEAD_SLOT), lambda bi, h, i: (bi, i, h)),
            pl.BlockSpec((None, s, HEAD_SLOT), lambda bi, h, i: (bi, 0, h)),
            pl.BlockSpec((None, s, V_HEAD_DIM), lambda bi, h, i: (bi, 0, h)),
        ],
        out_specs=pl.BlockSpec((None, tq, V_HEAD_DIM), lambda bi, h, i: (bi, i, h)),
        out_shape=jax.ShapeDtypeStruct((b, s, N_HEADS * V_HEAD_DIM), BF16),
        compiler_params=_params("parallel", "parallel", "arbitrary"),
        name="flash",
    )(q, k, v)


def _mix_kernel(x_ref, pooled_ref, attn_ref, gp_ref, ga_ref, wpo_ref, wao_ref, wout_ref, o_ref):
    y_pool = _dot(pooled_ref[...], wpo_ref[...])
    y_attn = _dot(attn_ref[...], wao_ref[...])
    mixed = gp_ref[...] * y_pool + ga_ref[...] * y_attn
    o_ref[...] = x_ref[...] + _dot(mixed.astype(BF16), wout_ref[...])


def _mix_call(x, pooled, attn, gates, w_po, w_ao, w_out, tm):
    m, d = x.shape
    row = lambda i: (i, 0)
    fixed = lambda i: (0, 0)
    return pl.pallas_call(
        _mix_kernel,
        grid=(m // tm,),
        in_specs=[
            pl.BlockSpec((tm, d), row),
            pl.BlockSpec((tm, pooled.shape[1]), row),
            pl.BlockSpec((tm, attn.shape[1]), row),
            pl.BlockSpec((tm, d), lambda i: (i, 0)),
            pl.BlockSpec((tm, d), lambda i: (i, 1)),
            pl.BlockSpec(w_po.shape, fixed),
            pl.BlockSpec(w_ao.shape, fixed),
            pl.BlockSpec(w_out.shape, fixed),
        ],
        out_specs=pl.BlockSpec((tm, d), row),
        out_shape=jax.ShapeDtypeStruct((m, d), F32),
        compiler_params=_params("parallel"),
        name="mix",
    )(x, pooled, attn, gates, gates, w_po, w_ao, w_out)


def _ffn_kernel(x_ref, g_ref, wg_ref, wu_ref, wd_ref, o_ref, h_ref):
    @pl.when(pl.program_id(1) == 0)
    def _():
        x = x_ref[...]
        h_ref[...] = _rms_bf16(x, g_ref[...])
        o_ref[...] = x

    h = h_ref[...]
    a = _dot(h, wg_ref[...])
    u = _dot(h, wu_ref[...])
    act = a * jax.nn.sigmoid(a) * u
    o_ref[...] += _dot(act.astype(BF16), wd_ref[...])


def _ffn_call(x, g, w_gate, w_up, w_down, tm, tf):
    m, d = x.shape
    f = w_gate.shape[1]
    return pl.pallas_call(
        _ffn_kernel,
        grid=(m // tm, f // tf),
        in_specs=[
            pl.BlockSpec((tm, d), lambda i, j: (i, 0)),
            pl.BlockSpec((1, d), lambda i, j: (0, 0)),
            pl.BlockSpec((d, tf), lambda i, j: (0, j)),
            pl.BlockSpec((d, tf), lambda i, j: (0, j)),
            pl.BlockSpec((tf, d), lambda i, j: (j, 0)),
        ],
        out_specs=pl.BlockSpec((tm, d), lambda i, j: (i, 0)),
        out_shape=jax.ShapeDtypeStruct((m, d), F32),
        scratch_shapes=[pltpu.VMEM((tm, d), BF16)],
        compiler_params=_params("parallel", "arbitrary"),
        name="ffn",
    )(x, g, w_gate, w_up, w_down)


def _rope_slot(t):
    z = jnp.zeros(t.shape[:-1] + (ROPE_HALF,), t.dtype)
    return jnp.concatenate([t[..., :ROPE_HALF], z, t[..., ROPE_HALF:], z], axis=-1)


def _head_slot(t):
    return jnp.concatenate([t[..., :QK_NOPE_DIM], _rope_slot(t[..., QK_NOPE_DIM:])], axis=-1)


def kernel(x, positions, attn_norm_g, w_in, b_gate, q_a_norm_g, w_q_b, kv_a_norm_g, w_kv_b,
           q_norm_g, k_norm_g, w_attn_o, w_pool_grp, pool_scale, w_pool_o, w_out,
           ffn_norm_g, w_ffn_gate, w_ffn_up, w_ffn_down):
    b, s, d = x.shape
    depth = w_in.shape[0]
    m = b * s
    pool_width = w_pool_o.shape[1]
    o_ckv = Q_LORA_RANK
    o_kr = o_ckv + KV_LORA_RANK
    o_up = o_kr + QK_ROPE_DIM
    o_gate = o_up + pool_width

    inv_freq = ROPE_THETA ** (-jnp.arange(ROPE_HALF, dtype=F32) / ROPE_HALF)
    ang = positions.astype(F32).reshape(m, 1) * inv_freq
    cos = jnp.tile(jnp.cos(ang), (1, 4))
    sin1 = jnp.sin(ang)
    sin = jnp.concatenate([-sin1, -sin1, sin1, sin1], axis=-1)

    xf = x.reshape(m, d)
    for l in range(depth):
        wi = w_in[l]
        w_lat = wi[:, :o_kr].astype(BF16)
        w_kr = _rope_slot(wi[:, o_kr:o_up]).astype(BF16)
        w_up = wi[:, o_up:o_gate].astype(BF16)
        w_gate = wi[:, o_gate:].astype(BF16)
        g_lat = jnp.concatenate([q_a_norm_g[l], kv_a_norm_g[l]]).reshape(1, -1)
        wq = _head_slot(w_q_b[l].reshape(Q_LORA_RANK, N_HEADS, QK_HEAD_DIM))
        wq = wq.reshape(Q_LORA_RANK, N_HEADS * HEAD_SLOT).astype(BF16)
        wkv = w_kv_b[l].reshape(KV_LORA_RANK, N_HEADS, QK_NOPE_DIM + V_HEAD_DIM)
        wkv = jnp.concatenate([wkv[..., :QK_NOPE_DIM].reshape(KV_LORA_RANK, -1),
                               wkv[..., QK_NOPE_DIM:].reshape(KV_LORA_RANK, -1)], axis=1).astype(BF16)
        gq = (_head_slot(q_norm_g[l]) * QK_HEAD_DIM ** -0.5).reshape(1, HEAD_SLOT)
        gk = _head_slot(k_norm_g[l]).reshape(1, HEAD_SLOT)
        g_attn_norm = attn_norm_g[l].reshape(1, d)

        lat, kr = _latent_call(xf, g_attn_norm, w_lat, g_lat, w_kr, tm=512)
        u_pool = _norm_matmul_call(xf, g_attn_norm, w_up, jnp.zeros((1, pool_width), F32),
                                   tm=512, tn=512, gate=False, name="upool")
        gates = _norm_matmul_call(xf, g_attn_norm, w_gate, b_gate[l].reshape(1, -1),
                                  tm=512, tn=1024, gate=True, name="gates")
        pooled = _pool_call(u_pool, w_pool_grp[l].astype(BF16), pool_scale[l].reshape(1, -1),
                            tm=512, seq=s)
        q, k, v = _qkv_call(lat, kr, cos, sin, wq, wkv, gq, gk, tm=256)
        attn = _flash_call(q.reshape(b, s, -1), k.reshape(b, s, -1), v.reshape(b, s, -1), tq=512)
        xf = _mix_call(xf, pooled, attn.reshape(m, -1), gates, w_pool_o[l].astype(BF16),
                       w_attn_o[l].astype(BF16), w_out[l].astype(BF16), tm=256)
        xf = _ffn_call(xf, ffn_norm_g[l].reshape(1, d), w_ffn_gate[l].astype(BF16),
                       w_ffn_up[l].astype(BF16), w_ffn_down[l].astype(BF16), tm=512, tf=512)
    return xf.reshape(b, s, d)
```

```python
import functools

import jax
import jax.numpy as jnp
from jax import lax
from jax.experimental import pallas as pl
from jax.experimental.pallas import tpu as pltpu

F32 = jnp.float32
BF16 = jnp.bfloat16

N_HEADS = 16
QK_NOPE_DIM = 128
QK_ROPE_DIM = 64
QK_HEAD_DIM = QK_NOPE_DIM + QK_ROPE_DIM
V_HEAD_DIM = 128
Q_LORA_RANK = 512
KV_LORA_RANK = 512
ROPE_THETA = 10000.0
POOL_WINDOWS = (2, 4, 8, 16)
POOL_GROUPS = 4
EPS = 1e-6

LANES = 128
HEAD_SLOT = 2 * LANES
ROPE_HALF = QK_ROPE_DIM // 2
POOL_HALO = 16
VMEM_LIMIT = 56 * 1024 * 1024
LOG2_E = 1.4426950408889634
_CONTRACT_LAST = (((1,), (1,)), ((), ()))


def _params(*sem):
    return pltpu.CompilerParams(dimension_semantics=sem, vmem_limit_bytes=VMEM_LIMIT)


def _dot(a, b):
    return jnp.dot(a, b, preferred_element_type=F32)


def _rms_bf16(x, g):
    ms = jnp.mean(x * x, axis=-1, keepdims=True)
    return (x * lax.rsqrt(ms + EPS) * g).astype(BF16)


def _rope(t, cos, sin):
    return t * cos + pltpu.roll(t, 2 * ROPE_HALF, 1) * sin


def _latent_kernel(x_ref, g_ref, w_ref, gl_ref, wkr_ref, lat_ref, kr_ref, h_ref):
    @pl.when(pl.program_id(1) == 0)
    def _():
        h_ref[...] = _rms_bf16(x_ref[...], g_ref[...])
        kr_ref[...] = _dot(h_ref[...], wkr_ref[...])

    c = _dot(h_ref[...], w_ref[...])
    lat_ref[...] = _rms_bf16(c, gl_ref[...])


def _latent_call(x, g, w_lat, g_lat, w_kr, tm):
    m, d = x.shape
    rank = Q_LORA_RANK
    n_lat = w_lat.shape[1]
    return pl.pallas_call(
        _latent_kernel,
        grid=(m // tm, n_lat // rank),
        in_specs=[
            pl.BlockSpec((tm, d), lambda i, j: (i, 0)),
            pl.BlockSpec((1, d), lambda i, j: (0, 0)),
            pl.BlockSpec((d, rank), lambda i, j: (0, j)),
            pl.BlockSpec((1, rank), lambda i, j: (0, j)),
            pl.BlockSpec((d, LANES), lambda i, j: (0, 0)),
        ],
        out_specs=[
            pl.BlockSpec((tm, rank), lambda i, j: (i, j)),
            pl.BlockSpec((tm, LANES), lambda i, j: (i, 0)),
        ],
        out_shape=[
            jax.ShapeDtypeStruct((m, n_lat), BF16),
            jax.ShapeDtypeStruct((m, LANES), F32),
        ],
        scratch_shapes=[pltpu.VMEM((tm, d), BF16)],
        compiler_params=_params("parallel", "arbitrary"),
        name="latent",
    )(x, g, w_lat, g_lat, w_kr)


def _norm_matmul_kernel(x_ref, g_ref, w_ref, b_ref, o_ref, h_ref, *, gate):
    @pl.when(pl.program_id(1) == 0)
    def _():
        h_ref[...] = _rms_bf16(x_ref[...], g_ref[...])

    acc = _dot(h_ref[...], w_ref[...])
    if gate:
        acc = jax.nn.sigmoid(acc + b_ref[...])
    o_ref[...] = acc.astype(o_ref.dtype)


def _norm_matmul_call(x, g, w, b, tm, tn, gate, name):
    m, d = x.shape
    n = w.shape[1]
    return pl.pallas_call(
        functools.partial(_norm_matmul_kernel, gate=gate),
        grid=(m // tm, n // tn),
        in_specs=[
            pl.BlockSpec((tm, d), lambda i, j: (i, 0)),
            pl.BlockSpec((1, d), lambda i, j: (0, 0)),
            pl.BlockSpec((d, tn), lambda i, j: (0, j)),
            pl.BlockSpec((1, tn), lambda i, j: (0, j)),
        ],
        out_specs=pl.BlockSpec((tm, tn), lambda i, j: (i, j)),
        out_shape=jax.ShapeDtypeStruct((m, n), F32),
        scratch_shapes=[pltpu.VMEM((tm, d), BF16)],
        compiler_params=_params("parallel", "arbitrary"),
        name=name,
    )(x, g, w, b)


def _pool_kernel(u_ref, halo_ref, wg_ref, sc_ref, o_ref, *, tm, seq):
    t0 = (pl.program_id(0) * tm) % seq
    halo = jnp.where(t0 > 0, halo_ref[...], 0.0)
    u = u_ref[...]
    ext = jnp.concatenate([halo, u], axis=0)
    pos = lax.broadcasted_iota(jnp.int32, (tm, 1), 0) + t0
    gd = wg_ref.shape[1]
    for g, w in enumerate(POOL_WINDOWS):
        cols = slice(g * gd, (g + 1) * gd)
        a = ext[:, cols]
        shift = 1
        while shift < w:
            a = a + pltpu.roll(a, shift, 0)
            shift *= 2
        count = jnp.minimum(pos + 1, w).astype(F32)
        pooled = a[POOL_HALO:, :] / count - u[:, cols]
        y = _dot(pooled.astype(BF16), wg_ref[g]) * sc_ref[:, cols]
        o_ref[:, cols] = y.astype(BF16)


def _pool_call(u, w_grp, scale, tm, seq):
    m, width = u.shape
    halo_blocks = tm // POOL_HALO
    return pl.pallas_call(
        functools.partial(_pool_kernel, tm=tm, seq=seq),
        grid=(m // tm,),
        in_specs=[
            pl.BlockSpec((tm, width), lambda i: (i, 0)),
            pl.BlockSpec((POOL_HALO, width), lambda i: (jnp.maximum(i * halo_blocks - 1, 0), 0)),
            pl.BlockSpec(w_grp.shape, lambda i: (0, 0, 0)),
            pl.BlockSpec((1, width), lambda i: (0, 0)),
        ],
        out_specs=pl.BlockSpec((tm, width), lambda i: (i, 0)),
        out_shape=jax.ShapeDtypeStruct((m, width), BF16),
        compiler_params=_params("parallel"),
        name="pool",
    )(u, u, w_grp, scale)


def _qkv_kernel(cq_ref, ckv_ref, kr_ref, cos_ref, sin_ref, wq_ref, wk_ref, wvt_ref,
                gq_ref, gk_ref, q_ref, k_ref, vt_ref):
    cos = cos_ref[...]
    sin = sin_ref[...]
    inv_dim = 1.0 / QK_HEAD_DIM
    ckv = ckv_ref[...]
    qf = _dot(cq_ref[...], wq_ref[...])
    kv = _dot(ckv, wk_ref[...])
    vt_ref[...] = lax.dot_general(wvt_ref[...], ckv, _CONTRACT_LAST,
                                  preferred_element_type=F32).astype(BF16)
    gq_n, gq_r = gq_ref[:, :LANES], gq_ref[:, LANES:]
    gk_n, gk_r = gk_ref[:, :LANES], gk_ref[:, LANES:]
    kr = kr_ref[...]
    ss_kr = jnp.sum(kr * kr, axis=-1, keepdims=True)
    kr_rot = _rope(kr * gk_r, cos, sin)
    for h in range(N_HEADS):
        lo = h * HEAD_SLOT
        qn = qf[:, lo:lo + LANES]
        qr = qf[:, lo + LANES:lo + HEAD_SLOT]
        ss = jnp.sum(qn * qn, axis=-1, keepdims=True) + jnp.sum(qr * qr, axis=-1, keepdims=True)
        inv = lax.rsqrt(ss * inv_dim + EPS)
        q_ref[:, lo:lo + LANES] = (qn * inv * gq_n).astype(BF16)
        q_ref[:, lo + LANES:lo + HEAD_SLOT] = _rope(qr * inv * gq_r, cos, sin).astype(BF16)
        kn = kv[:, h * LANES:(h + 1) * LANES]
        ssk = jnp.sum(kn * kn, axis=-1, keepdims=True) + ss_kr
        invk = lax.rsqrt(ssk * inv_dim + EPS)
        k_ref[:, lo:lo + LANES] = (kn * invk * gk_n).astype(BF16)
        k_ref[:, lo + LANES:lo + HEAD_SLOT] = (kr_rot * invk).astype(BF16)


def _qkv_call(lat, kr, cos, sin, wq, wk, wvt, gq, gk, tm):
    m = lat.shape[0]
    rank = Q_LORA_RANK
    wide = N_HEADS * HEAD_SLOT
    vw = N_HEADS * V_HEAD_DIM
    row = lambda i: (i, 0)
    fixed = lambda i: (0, 0)
    return pl.pallas_call(
        _qkv_kernel,
        grid=(m // tm,),
        in_specs=[
            pl.BlockSpec((tm, rank), lambda i: (i, 0)),
            pl.BlockSpec((tm, rank), lambda i: (i, 1)),
            pl.BlockSpec((tm, LANES), row),
            pl.BlockSpec((tm, LANES), row),
            pl.BlockSpec((tm, LANES), row),
            pl.BlockSpec(wq.shape, fixed),
            pl.BlockSpec(wk.shape, fixed),
            pl.BlockSpec(wvt.shape, fixed),
            pl.BlockSpec((1, HEAD_SLOT), fixed),
            pl.BlockSpec((1, HEAD_SLOT), fixed),
        ],
        out_specs=[
            pl.BlockSpec((tm, wide), row),
            pl.BlockSpec((tm, wide), row),
            pl.BlockSpec((None, vw, tm), lambda i: (i, 0, 0)),
        ],
        out_shape=[
            jax.ShapeDtypeStruct((m, wide), BF16),
            jax.ShapeDtypeStruct((m, wide), BF16),
            jax.ShapeDtypeStruct((m // tm, vw, tm), BF16),
        ],
        compiler_params=_params("parallel"),
        name="qkv",
    )(lat, lat, kr, cos, sin, wq, wk, wvt, gq, gk)


def _flash_kernel(q_ref, k_ref, vt_ref, o_ref, m_ref, l_ref, acc_ref, *, tq, heads):
    qi = pl.program_id(2)
    qs = [q_ref[:, h * HEAD_SLOT:(h + 1) * HEAD_SLOT] for h in range(heads)]
    m_ref[...] = jnp.full(m_ref.shape, -jnp.inf, F32)
    l_ref[...] = jnp.zeros(l_ref.shape, F32)
    acc_ref[...] = jnp.zeros(acc_ref.shape, F32)

    def scores(h, j):
        start = pl.multiple_of(j * tq, tq)
        k = k_ref[pl.ds(start, tq), h * HEAD_SLOT:(h + 1) * HEAD_SLOT]
        return lax.dot_general(k, qs[h], _CONTRACT_LAST, preferred_element_type=F32)

    def head_step(h, j, s, diagonal):
        if diagonal:
            key = lax.broadcasted_iota(jnp.int32, (tq, tq), 0)
            qry = lax.broadcasted_iota(jnp.int32, (tq, tq), 1)
            s = jnp.where(key <= qry, s, -jnp.inf)
        m_prev = m_ref[h]
        m_new = jnp.maximum(m_prev, jnp.max(s, axis=0, keepdims=True))
        p = jnp.exp2(s - m_new)
        alpha = jnp.exp2(m_prev - m_new)
        m_ref[h] = m_new
        l_ref[h] = alpha * l_ref[h] + jnp.sum(p, axis=0, keepdims=True)
        vt = vt_ref[j, h * V_HEAD_DIM:(h + 1) * V_HEAD_DIM, :]
        acc_ref[h] = alpha * acc_ref[h] + _dot(vt, p.astype(BF16))

    def step(j, diagonal):
        ss = [scores(h, j) for h in range(heads)]
        for h in range(heads):
            head_step(h, j, ss[h], diagonal)

    lax.fori_loop(0, qi, lambda j, c: step(j, False), None)
    step(qi, True)
    for h in range(heads):
        o_ref[:, h * V_HEAD_DIM:(h + 1) * V_HEAD_DIM] = (acc_ref[h] / l_ref[h]).T.astype(BF16)


def _flash_call(q, k, vt, tq, heads):
    b, s, _ = q.shape
    return pl.pallas_call(
        functools.partial(_flash_kernel, tq=tq, heads=heads),
        grid=(b, N_HEADS // heads, s // tq),
        in_specs=[
            pl.BlockSpec((None, tq, heads * HEAD_SLOT), lambda bi, h, i: (bi, i, h)),
            pl.BlockSpec((None, s, heads * HEAD_SLOT), lambda bi, h, i: (bi, 0, h)),
            pl.BlockSpec((None, s // tq, heads * V_HEAD_DIM, tq), lambda bi, h, i: (bi, 0, h, 0)),
        ],
        out_specs=pl.BlockSpec((None, tq, heads * V_HEAD_DIM), lambda bi, h, i: (bi, i, h)),
        out_shape=jax.ShapeDtypeStruct((b, s, N_HEADS * V_HEAD_DIM), BF16),
        scratch_shapes=[
            pltpu.VMEM((heads, 1, tq), F32),
            pltpu.VMEM((heads, 1, tq), F32),
            pltpu.VMEM((heads, V_HEAD_DIM, tq), F32),
        ],
        compiler_params=_params("parallel", "parallel", "arbitrary"),
        name="flash",
    )(q, k, vt)


def _mix_kernel(x_ref, pooled_ref, attn_ref, gp_ref, ga_ref, wpo_ref, wao_ref, wout_ref, o_ref):
    y_pool = _dot(pooled_ref[...], wpo_ref[...])
    y_attn = _dot(attn_ref[...], wao_ref[...])
    mixed = gp_ref[...] * y_pool + ga_ref[...] * y_attn
    o_ref[...] = x_ref[...] + _dot(mixed.astype(BF16), wout_ref[...])


def _mix_call(x, pooled, attn, gates, w_po, w_ao, w_out, tm):
    m, d = x.shape
    row = lambda i: (i, 0)
    fixed = lambda i: (0, 0)
    return pl.pallas_call(
        _mix_kernel,
        grid=(m // tm,),
        in_specs=[
            pl.BlockSpec((tm, d), row),
            pl.BlockSpec((tm, pooled.shape[1]), row),
            pl.BlockSpec((tm, attn.shape[1]), row),
            pl.BlockSpec((tm, d), lambda i: (i, 0)),
            pl.BlockSpec((tm, d), lambda i: (i, 1)),
            pl.BlockSpec(w_po.shape, fixed),
            pl.BlockSpec(w_ao.shape, fixed),
            pl.BlockSpec(w_out.shape, fixed),
        ],
        out_specs=pl.BlockSpec((tm, d), row),
        out_shape=jax.ShapeDtypeStruct((m, d), F32),
        compiler_params=_params("parallel"),
        name="mix",
    )(x, pooled, attn, gates, gates, w_po, w_ao, w_out)


def _ffn_kernel(x_ref, g_ref, wg_ref, wu_ref, wd_ref, o_ref, h_ref):
    @pl.when(pl.program_id(1) == 0)
    def _():
        x = x_ref[...]
        h_ref[...] = _rms_bf16(x, g_ref[...])
        o_ref[...] = x

    h = h_ref[...]
    a = _dot(h, wg_ref[...])
    u = _dot(h, wu_ref[...])
    act = a * jax.nn.sigmoid(a) * u
    o_ref[...] += _dot(act.astype(BF16), wd_ref[...])


def _ffn_call(x, g, w_gate, w_up, w_down, tm, tf):
    m, d = x.shape
    f = w_gate.shape[1]
    return pl.pallas_call(
        _ffn_kernel,
        grid=(m // tm, f // tf),
        in_specs=[
            pl.BlockSpec((tm, d), lambda i, j: (i, 0)),
            pl.BlockSpec((1, d), lambda i, j: (0, 0)),
            pl.BlockSpec((d, tf), lambda i, j: (0, j)),
            pl.BlockSpec((d, tf), lambda i, j: (0, j)),
            pl.BlockSpec((tf, d), lambda i, j: (j, 0)),
        ],
        out_specs=pl.BlockSpec((tm, d), lambda i, j: (i, 0)),
        out_shape=jax.ShapeDtypeStruct((m, d), F32),
        scratch_shapes=[pltpu.VMEM((tm, d), BF16)],
        compiler_params=_params("parallel", "arbitrary"),
        name="ffn",
    )(x, g, w_gate, w_up, w_down)


def _rope_slot(t):
    z = jnp.zeros(t.shape[:-1] + (ROPE_HALF,), t.dtype)
    return jnp.concatenate([t[..., :ROPE_HALF], z, t[..., ROPE_HALF:], z], axis=-1)


def _head_slot(t):
    return jnp.concatenate([t[..., :QK_NOPE_DIM], _rope_slot(t[..., QK_NOPE_DIM:])], axis=-1)


def kernel(x, positions, attn_norm_g, w_in, b_gate, q_a_norm_g, w_q_b, kv_a_norm_g, w_kv_b,
           q_norm_g, k_norm_g, w_attn_o, w_pool_grp, pool_scale, w_pool_o, w_out,
           ffn_norm_g, w_ffn_gate, w_ffn_up, w_ffn_down):
    b, s, d = x.shape
    depth = w_in.shape[0]
    m = b * s
    pool_width = w_pool_o.shape[1]
    o_ckv = Q_LORA_RANK
    o_kr = o_ckv + KV_LORA_RANK
    o_up = o_kr + QK_ROPE_DIM
    o_gate = o_up + pool_width

    inv_freq = ROPE_THETA ** (-jnp.arange(ROPE_HALF, dtype=F32) / ROPE_HALF)
    ang = positions.astype(F32).reshape(m, 1) * inv_freq
    cos = jnp.tile(jnp.cos(ang), (1, 4))
    sin1 = jnp.sin(ang)
    sin = jnp.concatenate([-sin1, -sin1, sin1, sin1], axis=-1)

    xf = x.reshape(m, d)
    for l in range(depth):
        wi = w_in[l]
        w_lat = wi[:, :o_kr].astype(BF16)
        w_kr = _rope_slot(wi[:, o_kr:o_up]).astype(BF16)
        w_up = wi[:, o_up:o_gate].astype(BF16)
        w_gate = wi[:, o_gate:].astype(BF16)
        g_lat = jnp.concatenate([q_a_norm_g[l], kv_a_norm_g[l]]).reshape(1, -1)
        wq = _head_slot(w_q_b[l].reshape(Q_LORA_RANK, N_HEADS, QK_HEAD_DIM))
        wq = wq.reshape(Q_LORA_RANK, N_HEADS * HEAD_SLOT).astype(BF16)
        wkv = w_kv_b[l].reshape(KV_LORA_RANK, N_HEADS, QK_NOPE_DIM + V_HEAD_DIM)
        wk = wkv[..., :QK_NOPE_DIM].reshape(KV_LORA_RANK, -1).astype(BF16)
        wvt = wkv[..., QK_NOPE_DIM:].reshape(KV_LORA_RANK, -1).T.astype(BF16)
        gq = (_head_slot(q_norm_g[l]) * (QK_HEAD_DIM ** -0.5 * LOG2_E)).reshape(1, HEAD_SLOT)
        gk = _head_slot(k_norm_g[l]).reshape(1, HEAD_SLOT)
        g_attn_norm = attn_norm_g[l].reshape(1, d)

        lat, kr = _latent_call(xf, g_attn_norm, w_lat, g_lat, w_kr, tm=512)
        u_pool = _norm_matmul_call(xf, g_attn_norm, w_up, jnp.zeros((1, pool_width), F32),
                                   tm=512, tn=512, gate=False, name="upool")
        gates = _norm_matmul_call(xf, g_attn_norm, w_gate, b_gate[l].reshape(1, -1),
                                  tm=512, tn=1024, gate=True, name="gates")
        pooled = _pool_call(u_pool, w_pool_grp[l].astype(BF16), pool_scale[l].reshape(1, -1),
                            tm=512, seq=s)
        tq = 512
        q, k, vt = _qkv_call(lat, kr, cos, sin, wq, wk, wvt, gq, gk, tm=tq)
        attn = _flash_call(q.reshape(b, s, -1), k.reshape(b, s, -1),
                           vt.reshape(b, s // tq, -1, tq), tq=tq, heads=4)
        xf = _mix_call(xf, pooled, attn.reshape(m, -1), gates, w_pool_o[l].astype(BF16),
                       w_attn_o[l].astype(BF16), w_out[l].astype(BF16), tm=256)
        xf = _ffn_call(xf, ffn_norm_g[l].reshape(1, d), w_ffn_gate[l].astype(BF16),
                       w_ffn_up[l].astype(BF16), w_ffn_down[l].astype(BF16), tm=512, tf=512)
    return xf.reshape(b, s, d)
```

```python
import functools

import jax
import jax.numpy as jnp
from jax import lax
from jax.experimental import pallas as pl
from jax.experimental.pallas import tpu as pltpu

F32 = jnp.float32
BF16 = jnp.bfloat16

N_HEADS = 16
QK_NOPE_DIM = 128
QK_ROPE_DIM = 64
QK_HEAD_DIM = QK_NOPE_DIM + QK_ROPE_DIM
V_HEAD_DIM = 128
Q_LORA_RANK = 512
KV_LORA_RANK = 512
ROPE_THETA = 10000.0
POOL_WINDOWS = (2, 4, 8, 16)
POOL_GROUPS = 4
EPS = 1e-6

LANES = 128
HEAD_SLOT = 2 * LANES
ROPE_HALF = QK_ROPE_DIM // 2
POOL_HALO = 16
VMEM_LIMIT = 56 * 1024 * 1024
LOG2_E = 1.4426950408889634
_CONTRACT_LAST = (((1,), (1,)), ((), ()))


def _params(*sem):
    return pltpu.CompilerParams(dimension_semantics=sem, vmem_limit_bytes=VMEM_LIMIT)


def _dot(a, b):
    return jnp.dot(a, b, preferred_element_type=F32)


def _rms_bf16(x, g):
    ms = jnp.mean(x * x, axis=-1, keepdims=True)
    return (x * lax.rsqrt(ms + EPS) * g).astype(BF16)


def _rope(t, cos, sin):
    return t * cos + pltpu.roll(t, 2 * ROPE_HALF, 1) * sin


def _inproj_kernel(x_ref, g_ref, w_ref, gl_ref, b_ref, wkr_ref,
                   lat_ref, up_ref, gate_ref, kr_ref, h_ref, *, n_lat, n_up):
    j = pl.program_id(1)

    @pl.when(j == 0)
    def _():
        h_ref[...] = _rms_bf16(x_ref[...], g_ref[...])
        kr_ref[...] = _dot(h_ref[...], wkr_ref[...])

    @pl.when(j < n_lat)
    def _():
        lat_ref[...] = _rms_bf16(_dot(h_ref[...], w_ref[...]), gl_ref[...])

    @pl.when(jnp.logical_and(j >= n_lat, j < n_lat + n_up))
    def _():
        up_ref[...] = _dot(h_ref[...], w_ref[...])

    @pl.when(j >= n_lat + n_up)
    def _():
        gate_ref[...] = jax.nn.sigmoid(_dot(h_ref[...], w_ref[...]) + b_ref[...])


def _inproj_call(x, g, w_all, g_lat, b_gate, w_kr, pool_width, tm):
    m, d = x.shape
    tn = Q_LORA_RANK
    n_lat = g_lat.shape[1] // tn
    n_up = pool_width // tn
    n_gate = b_gate.shape[1] // tn
    clamp = lambda j, lo, n: jnp.clip(j - lo, 0, n - 1)
    return pl.pallas_call(
        functools.partial(_inproj_kernel, n_lat=n_lat, n_up=n_up),
        grid=(m // tm, n_lat + n_up + n_gate),
        in_specs=[
            pl.BlockSpec((tm, d), lambda i, j: (i, 0)),
            pl.BlockSpec((1, d), lambda i, j: (0, 0)),
            pl.BlockSpec((d, tn), lambda i, j: (0, j)),
            pl.BlockSpec((1, tn), lambda i, j: (0, clamp(j, 0, n_lat))),
            pl.BlockSpec((1, tn), lambda i, j: (0, clamp(j, n_lat + n_up, n_gate))),
            pl.BlockSpec((d, LANES), lambda i, j: (0, 0)),
        ],
        out_specs=[
            pl.BlockSpec((tm, tn), lambda i, j: (i, clamp(j, 0, n_lat))),
            pl.BlockSpec((tm, tn), lambda i, j: (i, clamp(j, n_lat, n_up))),
            pl.BlockSpec((tm, tn), lambda i, j: (i, clamp(j, n_lat + n_up, n_gate))),
            pl.BlockSpec((tm, LANES), lambda i, j: (i, 0)),
        ],
        out_shape=[
            jax.ShapeDtypeStruct((m, n_lat * tn), BF16),
            jax.ShapeDtypeStruct((m, n_up * tn), F32),
            jax.ShapeDtypeStruct((m, n_gate * tn), F32),
            jax.ShapeDtypeStruct((m, LANES), F32),
        ],
        scratch_shapes=[pltpu.VMEM((tm, d), BF16)],
        compiler_params=_params("parallel", "arbitrary"),
        name="inproj",
    )(x, g, w_all, g_lat, b_gate, w_kr)


def _pool_kernel(u_ref, halo_ref, wg_ref, sc_ref, o_ref, *, tm, seq):
    t0 = (pl.program_id(0) * tm) % seq
    halo = jnp.where(t0 > 0, halo_ref[...], 0.0)
    u = u_ref[...]
    ext = jnp.concatenate([halo, u], axis=0)
    pos = lax.broadcasted_iota(jnp.int32, (tm, 1), 0) + t0
    gd = wg_ref.shape[1]
    for g, w in enumerate(POOL_WINDOWS):
        cols = slice(g * gd, (g + 1) * gd)
        a = ext[:, cols]
        shift = 1
        while shift < w:
            a = a + pltpu.roll(a, shift, 0)
            shift *= 2
        count = jnp.minimum(pos + 1, w).astype(F32)
        pooled = a[POOL_HALO:, :] / count - u[:, cols]
        y = _dot(pooled.astype(BF16), wg_ref[g]) * sc_ref[:, cols]
        o_ref[:, cols] = y.astype(BF16)


def _pool_call(u, w_grp, scale, tm, seq):
    m, width = u.shape
    halo_blocks = tm // POOL_HALO
    return pl.pallas_call(
        functools.partial(_pool_kernel, tm=tm, seq=seq),
        grid=(m // tm,),
        in_specs=[
            pl.BlockSpec((tm, width), lambda i: (i, 0)),
            pl.BlockSpec((POOL_HALO, width), lambda i: (jnp.maximum(i * halo_blocks - 1, 0), 0)),
            pl.BlockSpec(w_grp.shape, lambda i: (0, 0, 0)),
            pl.BlockSpec((1, width), lambda i: (0, 0)),
        ],
        out_specs=pl.BlockSpec((tm, width), lambda i: (i, 0)),
        out_shape=jax.ShapeDtypeStruct((m, width), BF16),
        compiler_params=_params("parallel"),
        name="pool",
    )(u, u, w_grp, scale)


def _qkv_kernel(cq_ref, ckv_ref, kr_ref, cos_ref, sin_ref, wq_ref, wk_ref, wvt_ref,
                gq_ref, gk_ref, q_ref, k_ref, vt_ref):
    cos = cos_ref[...]
    sin = sin_ref[...]
    inv_dim = 1.0 / QK_HEAD_DIM
    ckv = ckv_ref[...]
    qf = _dot(cq_ref[...], wq_ref[...])
    kv = _dot(ckv, wk_ref[...])
    vt_ref[...] = lax.dot_general(wvt_ref[...], ckv, _CONTRACT_LAST,
                                  preferred_element_type=F32).astype(BF16)
    gq_n, gq_r = gq_ref[:, :LANES], gq_ref[:, LANES:]
    gk_n, gk_r = gk_ref[:, :LANES], gk_ref[:, LANES:]
    kr = kr_ref[...]
    ss_kr = jnp.sum(kr * kr, axis=-1, keepdims=True)
    kr_rot = _rope(kr * gk_r, cos, sin)
    for h in range(N_HEADS):
        lo = h * HEAD_SLOT
        qn = qf[:, lo:lo + LANES]
        qr = qf[:, lo + LANES:lo + HEAD_SLOT]
        ss = jnp.sum(qn * qn, axis=-1, keepdims=True) + jnp.sum(qr * qr, axis=-1, keepdims=True)
        inv = lax.rsqrt(ss * inv_dim + EPS)
        q_ref[:, lo:lo + LANES] = (qn * inv * gq_n).astype(BF16)
        q_ref[:, lo + LANES:lo + HEAD_SLOT] = _rope(qr * inv * gq_r, cos, sin).astype(BF16)
        kn = kv[:, h * LANES:(h + 1) * LANES]
        ssk = jnp.sum(kn * kn, axis=-1, keepdims=True) + ss_kr
        invk = lax.rsqrt(ssk * inv_dim + EPS)
        k_ref[:, lo:lo + LANES] = (kn * invk * gk_n).astype(BF16)
        k_ref[:, lo + LANES:lo + HEAD_SLOT] = (kr_rot * invk).astype(BF16)


def _qkv_call(lat, kr, cos, sin, wq, wk, wvt, gq, gk, tm):
    m = lat.shape[0]
    rank = Q_LORA_RANK
    wide = N_HEADS * HEAD_SLOT
    vw = N_HEADS * V_HEAD_DIM
    row = lambda i: (i, 0)
    fixed = lambda i: (0, 0)
    return pl.pallas_call(
        _qkv_kernel,
        grid=(m // tm,),
        in_specs=[
            pl.BlockSpec((tm, rank), lambda i: (i, 0)),
            pl.BlockSpec((tm, rank), lambda i: (i, 1)),
            pl.BlockSpec((tm, LANES), row),
            pl.BlockSpec((tm, LANES), row),
            pl.BlockSpec((tm, LANES), row),
            pl.BlockSpec(wq.shape, fixed),
            pl.BlockSpec(wk.shape, fixed),
            pl.BlockSpec(wvt.shape, fixed),
            pl.BlockSpec((1, HEAD_SLOT), fixed),
            pl.BlockSpec((1, HEAD_SLOT), fixed),
        ],
        out_specs=[
            pl.BlockSpec((tm, wide), row),
            pl.BlockSpec((tm, wide), row),
            pl.BlockSpec((None, vw, tm), lambda i: (i, 0, 0)),
        ],
        out_shape=[
            jax.ShapeDtypeStruct((m, wide), BF16),
            jax.ShapeDtypeStruct((m, wide), BF16),
            jax.ShapeDtypeStruct((m // tm, vw, tm), BF16),
        ],
        compiler_params=_params("parallel"),
        name="qkv",
    )(lat, lat, kr, cos, sin, wq, wk, wvt, gq, gk)


def _flash_kernel(q_ref, k_ref, vt_ref, o_ref, m_ref, l_ref, acc_ref, *, tq, heads):
    qi = pl.program_id(2)
    qs = [q_ref[:, h * HEAD_SLOT:(h + 1) * HEAD_SLOT] for h in range(heads)]
    m_ref[...] = jnp.full(m_ref.shape, -jnp.inf, F32)
    l_ref[...] = jnp.zeros(l_ref.shape, F32)
    acc_ref[...] = jnp.zeros(acc_ref.shape, F32)

    def scores(h, j):
        start = pl.multiple_of(j * tq, tq)
        k = k_ref[pl.ds(start, tq), h * HEAD_SLOT:(h + 1) * HEAD_SLOT]
        return lax.dot_general(k, qs[h], _CONTRACT_LAST, preferred_element_type=F32)

    def head_step(h, j, s, diagonal):
        if diagonal:
            key = lax.broadcasted_iota(jnp.int32, (tq, tq), 0)
            qry = lax.broadcasted_iota(jnp.int32, (tq, tq), 1)
            s = jnp.where(key <= qry, s, -jnp.inf)
        m_prev = m_ref[h]
        m_new = jnp.maximum(m_prev, jnp.max(s, axis=0, keepdims=True))
        p = jnp.exp2(s - m_new)
        alpha = jnp.exp2(m_prev - m_new)
        m_ref[h] = m_new
        l_ref[h] = alpha * l_ref[h] + jnp.sum(p, axis=0, keepdims=True)
        vt = vt_ref[j, h * V_HEAD_DIM:(h + 1) * V_HEAD_DIM, :]
        acc_ref[h] = alpha * acc_ref[h] + _dot(vt, p.astype(BF16))

    def step(j, diagonal):
        ss = [scores(h, j) for h in range(heads)]
        for h in range(heads):
            head_step(h, j, ss[h], diagonal)

    lax.fori_loop(0, qi, lambda j, c: step(j, False), None)
    step(qi, True)
    for h in range(heads):
        o_ref[:, h * V_HEAD_DIM:(h + 1) * V_HEAD_DIM] = (acc_ref[h] / l_ref[h]).T.astype(BF16)


def _flash_call(q, k, vt, tq, heads):
    b, s, _ = q.shape
    return pl.pallas_call(
        functools.partial(_flash_kernel, tq=tq, heads=heads),
        grid=(b, N_HEADS // heads, s // tq),
        in_specs=[
            pl.BlockSpec((None, tq, heads * HEAD_SLOT), lambda bi, h, i: (bi, i, h)),
            pl.BlockSpec((None, s, heads * HEAD_SLOT), lambda bi, h, i: (bi, 0, h)),
            pl.BlockSpec((None, s // tq, heads * V_HEAD_DIM, tq), lambda bi, h, i: (bi, 0, h, 0)),
        ],
        out_specs=pl.BlockSpec((None, tq, heads * V_HEAD_DIM), lambda bi, h, i: (bi, i, h)),
        out_shape=jax.ShapeDtypeStruct((b, s, N_HEADS * V_HEAD_DIM), BF16),
        scratch_shapes=[
            pltpu.VMEM((heads, 1, tq), F32),
            pltpu.VMEM((heads, 1, tq), F32),
            pltpu.VMEM((heads, V_HEAD_DIM, tq), F32),
        ],
        compiler_params=_params("parallel", "parallel", "arbitrary"),
        name="flash",
    )(q, k, vt)


def _mix_kernel(x_ref, pooled_ref, attn_ref, gp_ref, ga_ref, wpo_ref, wao_ref, wout_ref, o_ref):
    y_pool = _dot(pooled_ref[...], wpo_ref[...])
    y_attn = _dot(attn_ref[...], wao_ref[...])
    mixed = gp_ref[...] * y_pool + ga_ref[...] * y_attn
    o_ref[...] = x_ref[...] + _dot(mixed.astype(BF16), wout_ref[...])


def _mix_call(x, pooled, attn, gates, w_po, w_ao, w_out, tm):
    m, d = x.shape
    row = lambda i: (i, 0)
    fixed = lambda i: (0, 0)
    return pl.pallas_call(
        _mix_kernel,
        grid=(m // tm,),
        in_specs=[
            pl.BlockSpec((tm, d), row),
            pl.BlockSpec((tm, pooled.shape[1]), row),
            pl.BlockSpec((tm, attn.shape[1]), row),
            pl.BlockSpec((tm, d), lambda i: (i, 0)),
            pl.BlockSpec((tm, d), lambda i: (i, 1)),
            pl.BlockSpec(w_po.shape, fixed),
            pl.BlockSpec(w_ao.shape, fixed),
            pl.BlockSpec(w_out.shape, fixed),
        ],
        out_specs=pl.BlockSpec((tm, d), row),
        out_shape=jax.ShapeDtypeStruct((m, d), F32),
        compiler_params=_params("parallel"),
        name="mix",
    )(x, pooled, attn, gates, gates, w_po, w_ao, w_out)


def _ffn_kernel(x_ref, g_ref, wg_ref, wu_ref, wd_ref, o_ref, h_ref):
    @pl.when(pl.program_id(1) == 0)
    def _():
        x = x_ref[...]
        h_ref[...] = _rms_bf16(x, g_ref[...])
        o_ref[...] = x

    h = h_ref[...]
    a = _dot(h, wg_ref[...])
    u = _dot(h, wu_ref[...])
    act = a * jax.nn.sigmoid(a) * u
    o_ref[...] += _dot(act.astype(BF16), wd_ref[...])


def _ffn_call(x, g, w_gate, w_up, w_down, tm, tf):
    m, d = x.shape
    f = w_gate.shape[1]
    return pl.pallas_call(
        _ffn_kernel,
        grid=(m // tm, f // tf),
        in_specs=[
            pl.BlockSpec((tm, d), lambda i, j: (i, 0)),
            pl.BlockSpec((1, d), lambda i, j: (0, 0)),
            pl.BlockSpec((d, tf), lambda i, j: (0, j)),
            pl.BlockSpec((d, tf), lambda i, j: (0, j)),
            pl.BlockSpec((tf, d), lambda i, j: (j, 0)),
        ],
        out_specs=pl.BlockSpec((tm, d), lambda i, j: (i, 0)),
        out_shape=jax.ShapeDtypeStruct((m, d), F32),
        scratch_shapes=[pltpu.VMEM((tm, d), BF16)],
        compiler_params=_params("parallel", "arbitrary"),
        name="ffn",
    )(x, g, w_gate, w_up, w_down)


def _rope_slot(t):
    z = jnp.zeros(t.shape[:-1] + (ROPE_HALF,), t.dtype)
    return jnp.concatenate([t[..., :ROPE_HALF], z, t[..., ROPE_HALF:], z], axis=-1)


def _head_slot(t):
    return jnp.concatenate([t[..., :QK_NOPE_DIM], _rope_slot(t[..., QK_NOPE_DIM:])], axis=-1)


def kernel(x, positions, attn_norm_g, w_in, b_gate, q_a_norm_g, w_q_b, kv_a_norm_g, w_kv_b,
           q_norm_g, k_norm_g, w_attn_o, w_pool_grp, pool_scale, w_pool_o, w_out,
           ffn_norm_g, w_ffn_gate, w_ffn_up, w_ffn_down):
    b, s, d = x.shape
    depth = w_in.shape[0]
    m = b * s
    pool_width = w_pool_o.shape[1]
    o_ckv = Q_LORA_RANK
    o_kr = o_ckv + KV_LORA_RANK
    o_up = o_kr + QK_ROPE_DIM
    o_gate = o_up + pool_width

    inv_freq = ROPE_THETA ** (-jnp.arange(ROPE_HALF, dtype=F32) / ROPE_HALF)
    ang = positions.astype(F32).reshape(m, 1) * inv_freq
    cos = jnp.tile(jnp.cos(ang), (1, 4))
    sin1 = jnp.sin(ang)
    sin = jnp.concatenate([-sin1, -sin1, sin1, sin1], axis=-1)

    xf = x.reshape(m, d)
    for l in range(depth):
        wi = w_in[l]
        w_all = jnp.concatenate([wi[:, :o_kr], wi[:, o_up:]], axis=1).astype(BF16)
        w_kr = _rope_slot(wi[:, o_kr:o_up]).astype(BF16)
        g_lat = jnp.concatenate([q_a_norm_g[l], kv_a_norm_g[l]]).reshape(1, -1)
        wq = _head_slot(w_q_b[l].reshape(Q_LORA_RANK, N_HEADS, QK_HEAD_DIM))
        wq = wq.reshape(Q_LORA_RANK, N_HEADS * HEAD_SLOT).astype(BF16)
        wkv = w_kv_b[l].reshape(KV_LORA_RANK, N_HEADS, QK_NOPE_DIM + V_HEAD_DIM)
        wk = wkv[..., :QK_NOPE_DIM].reshape(KV_LORA_RANK, -1).astype(BF16)
        wvt = wkv[..., QK_NOPE_DIM:].reshape(KV_LORA_RANK, -1).T.astype(BF16)
        gq = (_head_slot(q_norm_g[l]) * (QK_HEAD_DIM ** -0.5 * LOG2_E)).reshape(1, HEAD_SLOT)
        gk = _head_slot(k_norm_g[l]).reshape(1, HEAD_SLOT)
        g_attn_norm = attn_norm_g[l].reshape(1, d)

        lat, u_pool, gates, kr = _inproj_call(xf, g_attn_norm, w_all, g_lat, b_gate[l].reshape(1, -1),
                                              w_kr, pool_width, tm=1024)
        pooled = _pool_call(u_pool, w_pool_grp[l].astype(BF16), pool_scale[l].reshape(1, -1),
                            tm=512, seq=s)
        tq = 512
        q, k, vt = _qkv_call(lat, kr, cos, sin, wq, wk, wvt, gq, gk, tm=tq)
        attn = _flash_call(q.reshape(b, s, -1), k.reshape(b, s, -1),
                           vt.reshape(b, s // tq, -1, tq), tq=tq, heads=4)
        xf = _mix_call(xf, pooled, attn.reshape(m, -1), gates, w_pool_o[l].astype(BF16),
                       w_attn_o[l].astype(BF16), w_out[l].astype(BF16), tm=256)
        xf = _ffn_call(xf, ffn_norm_g[l].reshape(1, d), w_ffn_gate[l].astype(BF16),
                       w_ffn_up[l].astype(BF16), w_ffn_down[l].astype(BF16), tm=512, tf=512)
    return xf.reshape(b, s, d)
```

```python
import functools

import jax
import jax.numpy as jnp
from jax import lax
from jax.experimental import pallas as pl
from jax.experimental.pallas import tpu as pltpu

F32 = jnp.float32
BF16 = jnp.bfloat16

N_HEADS = 16
QK_NOPE_DIM = 128
QK_ROPE_DIM = 64
QK_HEAD_DIM = QK_NOPE_DIM + QK_ROPE_DIM
V_HEAD_DIM = 128
Q_LORA_RANK = 512
KV_LORA_RANK = 512
ROPE_THETA = 10000.0
POOL_WINDOWS = (2, 4, 8, 16)
POOL_GROUPS = 4
EPS = 1e-6

LANES = 128
HEAD_SLOT = 2 * LANES
ROPE_HALF = QK_ROPE_DIM // 2
POOL_HALO = 16
VMEM_LIMIT = 56 * 1024 * 1024
SUM_ROWS = 16
LOG2_E = 1.4426950408889634
_CONTRACT_LAST = (((1,), (1,)), ((), ()))


def _params(*sem):
    return pltpu.CompilerParams(dimension_semantics=sem, vmem_limit_bytes=VMEM_LIMIT)


def _dot(a, b):
    return jnp.dot(a, b, preferred_element_type=F32)


def _rms_bf16(x, g):
    ms = jnp.mean(x * x, axis=-1, keepdims=True)
    return (x * lax.rsqrt(ms + EPS) * g).astype(BF16)


def _rope(t, cos, sin):
    return t * cos + pltpu.roll(t, 2 * ROPE_HALF, 1) * sin


def _inproj_kernel(x_ref, g_ref, w_ref, gl_ref, b_ref, wkr_ref,
                   lat_ref, up_ref, gate_ref, kr_ref, h_ref, *, n_lat, n_up):
    j = pl.program_id(1)

    @pl.when(j == 0)
    def _():
        h_ref[...] = _rms_bf16(x_ref[...], g_ref[...])
        kr_ref[...] = _dot(h_ref[...], wkr_ref[...])

    @pl.when(j < n_lat)
    def _():
        lat_ref[...] = _rms_bf16(_dot(h_ref[...], w_ref[...]), gl_ref[...])

    @pl.when(jnp.logical_and(j >= n_lat, j < n_lat + n_up))
    def _():
        up_ref[...] = _dot(h_ref[...], w_ref[...])

    @pl.when(j >= n_lat + n_up)
    def _():
        gate_ref[...] = jax.nn.sigmoid(_dot(h_ref[...], w_ref[...]) + b_ref[...])


def _inproj_call(x, g, w_all, g_lat, b_gate, w_kr, pool_width, tm):
    m, d = x.shape
    tn = Q_LORA_RANK
    n_lat = g_lat.shape[1] // tn
    n_up = pool_width // tn
    n_gate = b_gate.shape[1] // tn
    clamp = lambda j, lo, n: jnp.clip(j - lo, 0, n - 1)
    return pl.pallas_call(
        functools.partial(_inproj_kernel, n_lat=n_lat, n_up=n_up),
        grid=(m // tm, n_lat + n_up + n_gate),
        in_specs=[
            pl.BlockSpec((tm, d), lambda i, j: (i, 0)),
            pl.BlockSpec((1, d), lambda i, j: (0, 0)),
            pl.BlockSpec((d, tn), lambda i, j: (0, j)),
            pl.BlockSpec((1, tn), lambda i, j: (0, clamp(j, 0, n_lat))),
            pl.BlockSpec((1, tn), lambda i, j: (0, clamp(j, n_lat + n_up, n_gate))),
            pl.BlockSpec((d, LANES), lambda i, j: (0, 0)),
        ],
        out_specs=[
            pl.BlockSpec((tm, tn), lambda i, j: (i, clamp(j, 0, n_lat))),
            pl.BlockSpec((tm, tn), lambda i, j: (i, clamp(j, n_lat, n_up))),
            pl.BlockSpec((tm, tn), lambda i, j: (i, clamp(j, n_lat + n_up, n_gate))),
            pl.BlockSpec((tm, LANES), lambda i, j: (i, 0)),
        ],
        out_shape=[
            jax.ShapeDtypeStruct((m, n_lat * tn), BF16),
            jax.ShapeDtypeStruct((m, n_up * tn), F32),
            jax.ShapeDtypeStruct((m, n_gate * tn), F32),
            jax.ShapeDtypeStruct((m, LANES), F32),
        ],
        scratch_shapes=[pltpu.VMEM((tm, d), BF16)],
        compiler_params=_params("parallel", "arbitrary"),
        name="inproj",
    )(x, g, w_all, g_lat, b_gate, w_kr)


def _pool_kernel(u_ref, halo_ref, wg_ref, sc_ref, o_ref, *, tm, seq):
    t0 = (pl.program_id(0) * tm) % seq
    halo = jnp.where(t0 > 0, halo_ref[...], 0.0)
    u = u_ref[...]
    ext = jnp.concatenate([halo, u], axis=0)
    pos = lax.broadcasted_iota(jnp.int32, (tm, 1), 0) + t0
    gd = wg_ref.shape[1]
    for g, w in enumerate(POOL_WINDOWS):
        cols = slice(g * gd, (g + 1) * gd)
        a = ext[:, cols]
        shift = 1
        while shift < w:
            a = a + pltpu.roll(a, shift, 0)
            shift *= 2
        count = jnp.minimum(pos + 1, w).astype(F32)
        pooled = a[POOL_HALO:, :] / count - u[:, cols]
        y = _dot(pooled.astype(BF16), wg_ref[g]) * sc_ref[:, cols]
        o_ref[:, cols] = y.astype(BF16)


def _pool_call(u, w_grp, scale, tm, seq):
    m, width = u.shape
    halo_blocks = tm // POOL_HALO
    return pl.pallas_call(
        functools.partial(_pool_kernel, tm=tm, seq=seq),
        grid=(m // tm,),
        in_specs=[
            pl.BlockSpec((tm, width), lambda i: (i, 0)),
            pl.BlockSpec((POOL_HALO, width), lambda i: (jnp.maximum(i * halo_blocks - 1, 0), 0)),
            pl.BlockSpec(w_grp.shape, lambda i: (0, 0, 0)),
            pl.BlockSpec((1, width), lambda i: (0, 0)),
        ],
        out_specs=pl.BlockSpec((tm, width), lambda i: (i, 0)),
        out_shape=jax.ShapeDtypeStruct((m, width), BF16),
        compiler_params=_params("parallel"),
        name="pool",
    )(u, u, w_grp, scale)


def _qkv_kernel(cq_ref, ckv_ref, kr_ref, cos_ref, sin_ref, wq_ref, wk_ref, wvt_ref,
                gq_ref, gk_ref, q_ref, k_ref, vt_ref):
    cos = cos_ref[...]
    sin = sin_ref[...]
    inv_dim = 1.0 / QK_HEAD_DIM
    ckv = ckv_ref[...]
    qf = _dot(cq_ref[...], wq_ref[...])
    kv = _dot(ckv, wk_ref[...])
    vt_ref[...] = lax.dot_general(wvt_ref[...], ckv, _CONTRACT_LAST,
                                  preferred_element_type=F32).astype(BF16)
    gq_n, gq_r = gq_ref[:, :LANES], gq_ref[:, LANES:]
    gk_n, gk_r = gk_ref[:, :LANES], gk_ref[:, LANES:]
    kr = kr_ref[...]
    ss_kr = jnp.sum(kr * kr, axis=-1, keepdims=True)
    kr_rot = _rope(kr * gk_r, cos, sin)
    for h in range(N_HEADS):
        lo = h * HEAD_SLOT
        qn = qf[:, lo:lo + LANES]
        qr = qf[:, lo + LANES:lo + HEAD_SLOT]
        ss = jnp.sum(qn * qn, axis=-1, keepdims=True) + jnp.sum(qr * qr, axis=-1, keepdims=True)
        inv = lax.rsqrt(ss * inv_dim + EPS)
        q_ref[:, lo:lo + LANES] = (qn * inv * gq_n).astype(BF16)
        q_ref[:, lo + LANES:lo + HEAD_SLOT] = _rope(qr * inv * gq_r, cos, sin).astype(BF16)
        kn = kv[:, h * LANES:(h + 1) * LANES]
        ssk = jnp.sum(kn * kn, axis=-1, keepdims=True) + ss_kr
        invk = lax.rsqrt(ssk * inv_dim + EPS)
        k_ref[:, lo:lo + LANES] = (kn * invk * gk_n).astype(BF16)
        k_ref[:, lo + LANES:lo + HEAD_SLOT] = (kr_rot * invk).astype(BF16)


def _qkv_call(lat, kr, cos, sin, wq, wk, wvt, gq, gk, tm):
    m = lat.shape[0]
    rank = Q_LORA_RANK
    wide = N_HEADS * HEAD_SLOT
    vw = N_HEADS * V_HEAD_DIM
    row = lambda i: (i, 0)
    fixed = lambda i: (0, 0)
    return pl.pallas_call(
        _qkv_kernel,
        grid=(m // tm,),
        in_specs=[
            pl.BlockSpec((tm, rank), lambda i: (i, 0)),
            pl.BlockSpec((tm, rank), lambda i: (i, 1)),
            pl.BlockSpec((tm, LANES), row),
            pl.BlockSpec((tm, LANES), row),
            pl.BlockSpec((tm, LANES), row),
            pl.BlockSpec(wq.shape, fixed),
            pl.BlockSpec(wk.shape, fixed),
            pl.BlockSpec(wvt.shape, fixed),
            pl.BlockSpec((1, HEAD_SLOT), fixed),
            pl.BlockSpec((1, HEAD_SLOT), fixed),
        ],
        out_specs=[
            pl.BlockSpec((tm, wide), row),
            pl.BlockSpec((tm, wide), row),
            pl.BlockSpec((None, vw, tm), lambda i: (i, 0, 0)),
        ],
        out_shape=[
            jax.ShapeDtypeStruct((m, wide), BF16),
            jax.ShapeDtypeStruct((m, wide), BF16),
            jax.ShapeDtypeStruct((m // tm, vw, tm), BF16),
        ],
        compiler_params=_params("parallel"),
        name="qkv",
    )(lat, lat, kr, cos, sin, wq, wk, wvt, gq, gk)


def _flash_kernel(q_ref, k_ref, vt_ref, o_ref, m_ref, acc_ref, s_ref, *, tq, heads):
    qi = pl.program_id(2)
    qs = [q_ref[:, h * HEAD_SLOT:(h + 1) * HEAD_SLOT] for h in range(heads)]
    ones = jnp.ones((SUM_ROWS, tq), BF16)
    m_ref[...] = jnp.full(m_ref.shape, -jnp.inf, F32)
    acc_ref[...] = jnp.zeros(acc_ref.shape, F32)

    def scores(h, j):
        start = pl.multiple_of(j * tq, tq)
        k = k_ref[pl.ds(start, tq), h * HEAD_SLOT:(h + 1) * HEAD_SLOT]
        return lax.dot_general(k, qs[h], _CONTRACT_LAST, preferred_element_type=F32)

    def head_step(h, j, s, diagonal):
        if diagonal:
            key = lax.broadcasted_iota(jnp.int32, (tq, tq), 0)
            qry = lax.broadcasted_iota(jnp.int32, (tq, tq), 1)
            s = jnp.where(key <= qry, s, -jnp.inf)
        m_prev = m_ref[h]
        s_ref[h] = s
        m_new = jnp.maximum(m_prev, jnp.max(s, axis=0, keepdims=True))
        p = jnp.exp2(s_ref[h] - m_new)
        alpha = jnp.exp2(m_prev - m_new)
        m_ref[h] = m_new
        vt = vt_ref[j, h * V_HEAD_DIM:(h + 1) * V_HEAD_DIM, :]
        vt = jnp.concatenate([vt, ones], axis=0)
        acc_ref[h] = alpha * acc_ref[h] + _dot(vt, p.astype(BF16))

    def step(j, diagonal):
        ss = [scores(h, j) for h in range(heads)]
        for h in range(heads):
            head_step(h, j, ss[h], diagonal)

    lax.fori_loop(0, qi, lambda j, c: step(j, False), None)
    step(qi, True)
    for h in range(heads):
        out = acc_ref[h, :V_HEAD_DIM, :] / acc_ref[h, V_HEAD_DIM:V_HEAD_DIM + 1, :]
        o_ref[:, h * V_HEAD_DIM:(h + 1) * V_HEAD_DIM] = out.T.astype(BF16)


def _flash_call(q, k, vt, tq, heads):
    b, s, _ = q.shape
    return pl.pallas_call(
        functools.partial(_flash_kernel, tq=tq, heads=heads),
        grid=(b, N_HEADS // heads, s // tq),
        in_specs=[
            pl.BlockSpec((None, tq, heads * HEAD_SLOT), lambda bi, h, i: (bi, i, h)),
            pl.BlockSpec((None, s, heads * HEAD_SLOT), lambda bi, h, i: (bi, 0, h)),
            pl.BlockSpec((None, s // tq, heads * V_HEAD_DIM, tq), lambda bi, h, i: (bi, 0, h, 0)),
        ],
        out_specs=pl.BlockSpec((None, tq, heads * V_HEAD_DIM), lambda bi, h, i: (bi, i, h)),
        out_shape=jax.ShapeDtypeStruct((b, s, N_HEADS * V_HEAD_DIM), BF16),
        scratch_shapes=[
            pltpu.VMEM((heads, 1, tq), F32),
            pltpu.VMEM((heads, V_HEAD_DIM + SUM_ROWS, tq), F32),
            pltpu.VMEM((heads, tq, tq), F32),
        ],
        compiler_params=_params("parallel", "parallel", "arbitrary"),
        name="flash",
    )(q, k, vt)


def _mix_kernel(x_ref, pooled_ref, attn_ref, gp_ref, ga_ref, wpo_ref, wao_ref, wout_ref, o_ref):
    y_pool = _dot(pooled_ref[...], wpo_ref[...])
    y_attn = _dot(attn_ref[...], wao_ref[...])
    mixed = gp_ref[...] * y_pool + ga_ref[...] * y_attn
    o_ref[...] = x_ref[...] + _dot(mixed.astype(BF16), wout_ref[...])


def _mix_call(x, pooled, attn, gates, w_po, w_ao, w_out, tm):
    m, d = x.shape
    row = lambda i: (i, 0)
    fixed = lambda i: (0, 0)
    return pl.pallas_call(
        _mix_kernel,
        grid=(m // tm,),
        in_specs=[
            pl.BlockSpec((tm, d), row),
            pl.BlockSpec((tm, pooled.shape[1]), row),
            pl.BlockSpec((tm, attn.shape[1]), row),
            pl.BlockSpec((tm, d), lambda i: (i, 0)),
            pl.BlockSpec((tm, d), lambda i: (i, 1)),
            pl.BlockSpec(w_po.shape, fixed),
            pl.BlockSpec(w_ao.shape, fixed),
            pl.BlockSpec(w_out.shape, fixed),
        ],
        out_specs=pl.BlockSpec((tm, d), row),
        out_shape=jax.ShapeDtypeStruct((m, d), F32),
        compiler_params=_params("parallel"),
        name="mix",
    )(x, pooled, attn, gates, gates, w_po, w_ao, w_out)


def _ffn_kernel(x_ref, g_ref, wg_ref, wu_ref, wd_ref, o_ref, h_ref):
    @pl.when(pl.program_id(1) == 0)
    def _():
        x = x_ref[...]
        h_ref[...] = _rms_bf16(x, g_ref[...])
        o_ref[...] = x

    h = h_ref[...]
    a = _dot(h, wg_ref[...].astype(BF16))
    u = _dot(h, wu_ref[...].astype(BF16))
    act = a * jax.nn.sigmoid(a) * u
    o_ref[...] += _dot(act.astype(BF16), wd_ref[...].astype(BF16))


def _ffn_call(x, g, w_gate, w_up, w_down, tm, tf):
    m, d = x.shape
    f = w_gate.shape[1]
    return pl.pallas_call(
        _ffn_kernel,
        grid=(m // tm, f // tf),
        in_specs=[
            pl.BlockSpec((tm, d), lambda i, j: (i, 0)),
            pl.BlockSpec((1, d), lambda i, j: (0, 0)),
            pl.BlockSpec((d, tf), lambda i, j: (0, j)),
            pl.BlockSpec((d, tf), lambda i, j: (0, j)),
            pl.BlockSpec((tf, d), lambda i, j: (j, 0)),
        ],
        out_specs=pl.BlockSpec((tm, d), lambda i, j: (i, 0)),
        out_shape=jax.ShapeDtypeStruct((m, d), F32),
        scratch_shapes=[pltpu.VMEM((tm, d), BF16)],
        compiler_params=_params("parallel", "arbitrary"),
        name="ffn",
    )(x, g, w_gate, w_up, w_down)


def _rope_slot(t):
    z = jnp.zeros(t.shape[:-1] + (ROPE_HALF,), t.dtype)
    return jnp.concatenate([t[..., :ROPE_HALF], z, t[..., ROPE_HALF:], z], axis=-1)


def _head_slot(t):
    return jnp.concatenate([t[..., :QK_NOPE_DIM], _rope_slot(t[..., QK_NOPE_DIM:])], axis=-1)


def kernel(x, positions, attn_norm_g, w_in, b_gate, q_a_norm_g, w_q_b, kv_a_norm_g, w_kv_b,
           q_norm_g, k_norm_g, w_attn_o, w_pool_grp, pool_scale, w_pool_o, w_out,
           ffn_norm_g, w_ffn_gate, w_ffn_up, w_ffn_down):
    b, s, d = x.shape
    depth = w_in.shape[0]
    m = b * s
    pool_width = w_pool_o.shape[1]
    o_ckv = Q_LORA_RANK
    o_kr = o_ckv + KV_LORA_RANK
    o_up = o_kr + QK_ROPE_DIM
    o_gate = o_up + pool_width

    inv_freq = ROPE_THETA ** (-jnp.arange(ROPE_HALF, dtype=F32) / ROPE_HALF)
    ang = positions.astype(F32).reshape(m, 1) * inv_freq
    cos = jnp.tile(jnp.cos(ang), (1, 4))
    sin1 = jnp.sin(ang)
    sin = jnp.concatenate([-sin1, -sin1, sin1, sin1], axis=-1)

    xf = x.reshape(m, d)
    for l in range(depth):
        wi = w_in[l]
        w_all = jnp.concatenate([wi[:, :o_kr], wi[:, o_up:]], axis=1).astype(BF16)
        w_kr = _rope_slot(wi[:, o_kr:o_up]).astype(BF16)
        g_lat = jnp.concatenate([q_a_norm_g[l], kv_a_norm_g[l]]).reshape(1, -1)
        wq = _head_slot(w_q_b[l].reshape(Q_LORA_RANK, N_HEADS, QK_HEAD_DIM))
        wq = wq.reshape(Q_LORA_RANK, N_HEADS * HEAD_SLOT).astype(BF16)
        wkv = w_kv_b[l].reshape(KV_LORA_RANK, N_HEADS, QK_NOPE_DIM + V_HEAD_DIM)
        wk = wkv[..., :QK_NOPE_DIM].reshape(KV_LORA_RANK, -1).astype(BF16)
        wvt = wkv[..., QK_NOPE_DIM:].reshape(KV_LORA_RANK, -1).T.astype(BF16)
        gq = (_head_slot(q_norm_g[l]) * (QK_HEAD_DIM ** -0.5 * LOG2_E)).reshape(1, HEAD_SLOT)
        gk = _head_slot(k_norm_g[l]).reshape(1, HEAD_SLOT)
        g_attn_norm = attn_norm_g[l].reshape(1, d)

        lat, u_pool, gates, kr = _inproj_call(xf, g_attn_norm, w_all, g_lat, b_gate[l].reshape(1, -1),
                                              w_kr, pool_width, tm=1024)
        pooled = _pool_call(u_pool, w_pool_grp[l].astype(BF16), pool_scale[l].reshape(1, -1),
                            tm=512, seq=s)
        tq = 512
        q, k, vt = _qkv_call(lat, kr, cos, sin, wq, wk, wvt, gq, gk, tm=tq)
        attn = _flash_call(q.reshape(b, s, -1), k.reshape(b, s, -1),
                           vt.reshape(b, s // tq, -1, tq), tq=tq, heads=4)
        xf = _mix_call(xf, pooled, attn.reshape(m, -1), gates, w_pool_o[l].astype(BF16),
                       w_attn_o[l].astype(BF16), w_out[l].astype(BF16), tm=256)
        xf = _ffn_call(xf, ffn_norm_g[l].reshape(1, d), w_ffn_gate[l], w_ffn_up[l], w_ffn_down[l],
                       tm=1024, tf=256)
    return xf.reshape(b, s, d)
```

```python
import functools

import jax
import jax.numpy as jnp
from jax import lax
from jax.experimental import pallas as pl
from jax.experimental.pallas import tpu as pltpu

F32 = jnp.float32
BF16 = jnp.bfloat16

N_HEADS = 16
QK_NOPE_DIM = 128
QK_ROPE_DIM = 64
QK_HEAD_DIM = QK_NOPE_DIM + QK_ROPE_DIM
V_HEAD_DIM = 128
Q_LORA_RANK = 512
KV_LORA_RANK = 512
ROPE_THETA = 10000.0
POOL_WINDOWS = (2, 4, 8, 16)
POOL_GROUPS = 4
EPS = 1e-6

LANES = 128
SUBLANES = 8
HEAD_SLOT = 2 * LANES
ROPE_HALF = QK_ROPE_DIM // 2
POOL_HALO = 16
VMEM_LIMIT = 56 * 1024 * 1024
SUM_ROWS = 16
LOG2_E = 1.4426950408889634
_CONTRACT_LAST = (((1,), (1,)), ((), ()))


def _params(*sem):
    return pltpu.CompilerParams(dimension_semantics=sem, vmem_limit_bytes=VMEM_LIMIT)


def _dot(a, b):
    return jnp.dot(a, b, preferred_element_type=F32)


def _rms_bf16(x, g):
    ms = jnp.mean(x * x, axis=-1, keepdims=True)
    return (x * lax.rsqrt(ms + EPS) * g).astype(BF16)


def _rope(t, cos, sin):
    return t * cos + pltpu.roll(t, 2 * ROPE_HALF, 1) * sin


def _dot_t(a, w):
    return lax.dot_general(a, w.astype(BF16), _CONTRACT_LAST, preferred_element_type=F32)


def _inproj_kernel(x_ref, g_ref, w_ref, gl_ref, b_ref, wkr_ref,
                   lat_ref, up_ref, gate_ref, kr_ref, h_ref, *, n_lat, n_up):
    j = pl.program_id(1)

    @pl.when(j == 0)
    def _():
        h_ref[...] = _rms_bf16(x_ref[...], g_ref[...])
        wkr = wkr_ref[...]
        z = jnp.zeros((ROPE_HALF, wkr.shape[1]), F32)
        slot = jnp.concatenate([wkr[:ROPE_HALF], z, wkr[ROPE_HALF:], z], axis=0)
        kr_ref[...] = _dot_t(h_ref[...], slot)

    @pl.when(j < n_lat)
    def _():
        lat_ref[...] = _rms_bf16(_dot_t(h_ref[...], w_ref[...]), gl_ref[...])

    @pl.when(jnp.logical_and(j >= n_lat, j < n_lat + n_up))
    def _():
        up_ref[...] = _dot_t(h_ref[...], w_ref[...])

    @pl.when(j >= n_lat + n_up)
    def _():
        gate_ref[...] = jax.nn.sigmoid(_dot_t(h_ref[...], w_ref[...]) + b_ref[...])


def _inproj_call(x, g, w_t, g_lat, b_gate, pool_width, tm):
    m, d = x.shape
    tn = Q_LORA_RANK
    n_lat = g_lat.shape[1] // tn
    n_up = pool_width // tn
    n_gate = b_gate.shape[1] // tn
    o_kr = n_lat * tn
    o_up = o_kr + QK_ROPE_DIM
    clamp = lambda j, lo, n: jnp.clip(j - lo, 0, n - 1)
    sub = SUBLANES
    w_row = lambda j: sub * jnp.where(j < n_lat, j * (tn // sub),
                                      o_up // sub + (j - n_lat) * (tn // sub))
    return pl.pallas_call(
        functools.partial(_inproj_kernel, n_lat=n_lat, n_up=n_up),
        grid=(m // tm, n_lat + n_up + n_gate),
        in_specs=[
            pl.BlockSpec((tm, d), lambda i, j: (i, 0)),
            pl.BlockSpec((1, d), lambda i, j: (0, 0)),
            pl.BlockSpec((pl.Element(tn), pl.Element(d)), lambda i, j: (w_row(j), 0)),
            pl.BlockSpec((1, tn), lambda i, j: (0, clamp(j, 0, n_lat))),
            pl.BlockSpec((1, tn), lambda i, j: (0, clamp(j, n_lat + n_up, n_gate))),
            pl.BlockSpec((pl.Element(QK_ROPE_DIM), pl.Element(d)), lambda i, j: (o_kr, 0)),
        ],
        out_specs=[
            pl.BlockSpec((tm, tn), lambda i, j: (i, clamp(j, 0, n_lat))),
            pl.BlockSpec((tm, tn), lambda i, j: (i, clamp(j, n_lat, n_up))),
            pl.BlockSpec((tm, tn), lambda i, j: (i, clamp(j, n_lat + n_up, n_gate))),
            pl.BlockSpec((tm, LANES), lambda i, j: (i, 0)),
        ],
        out_shape=[
            jax.ShapeDtypeStruct((m, n_lat * tn), BF16),
            jax.ShapeDtypeStruct((m, n_up * tn), F32),
            jax.ShapeDtypeStruct((m, n_gate * tn), F32),
            jax.ShapeDtypeStruct((m, LANES), F32),
        ],
        scratch_shapes=[pltpu.VMEM((tm, d), BF16)],
        compiler_params=_params("parallel", "arbitrary"),
        name="inproj",
    )(x, g, w_t, g_lat, b_gate, w_t)


def _pool_kernel(u_ref, halo_ref, wg_ref, sc_ref, o_ref, *, tm, seq):
    t0 = (pl.program_id(0) * tm) % seq
    halo = jnp.where(t0 > 0, halo_ref[...], 0.0)
    u = u_ref[...]
    ext = jnp.concatenate([halo, u], axis=0)
    pos = lax.broadcasted_iota(jnp.int32, (tm, 1), 0) + t0
    gd = wg_ref.shape[1]
    for g, w in enumerate(POOL_WINDOWS):
        cols = slice(g * gd, (g + 1) * gd)
        a = ext[:, cols]
        shift = 1
        while shift < w:
            a = a + pltpu.roll(a, shift, 0)
            shift *= 2
        count = jnp.minimum(pos + 1, w).astype(F32)
        pooled = a[POOL_HALO:, :] / count - u[:, cols]
        y = _dot(pooled.astype(BF16), wg_ref[g]) * sc_ref[:, cols]
        o_ref[:, cols] = y.astype(BF16)


def _pool_call(u, w_grp, scale, tm, seq):
    m, width = u.shape
    halo_blocks = tm // POOL_HALO
    return pl.pallas_call(
        functools.partial(_pool_kernel, tm=tm, seq=seq),
        grid=(m // tm,),
        in_specs=[
            pl.BlockSpec((tm, width), lambda i: (i, 0)),
            pl.BlockSpec((POOL_HALO, width), lambda i: (jnp.maximum(i * halo_blocks - 1, 0), 0)),
            pl.BlockSpec(w_grp.shape, lambda i: (0, 0, 0)),
            pl.BlockSpec((1, width), lambda i: (0, 0)),
        ],
        out_specs=pl.BlockSpec((tm, width), lambda i: (i, 0)),
        out_shape=jax.ShapeDtypeStruct((m, width), BF16),
        compiler_params=_params("parallel"),
        name="pool",
    )(u, u, w_grp, scale)


def _qkv_kernel(cq_ref, ckv_ref, kr_ref, cos_ref, sin_ref, wq_ref, wk_ref, wvt_ref,
                gq_ref, gk_ref, q_ref, k_ref, vt_ref):
    cos = cos_ref[...]
    sin = sin_ref[...]
    inv_dim = 1.0 / QK_HEAD_DIM
    ckv = ckv_ref[...]
    qf = _dot(cq_ref[...], wq_ref[...])
    kv = _dot(ckv, wk_ref[...])
    vt_ref[...] = lax.dot_general(wvt_ref[...], ckv, _CONTRACT_LAST,
                                  preferred_element_type=F32).astype(BF16)
    gq_n, gq_r = gq_ref[:, :LANES], gq_ref[:, LANES:]
    gk_n, gk_r = gk_ref[:, :LANES], gk_ref[:, LANES:]
    kr = kr_ref[...]
    ss_kr = jnp.sum(kr * kr, axis=-1, keepdims=True)
    kr_rot = _rope(kr * gk_r, cos, sin)
    for h in range(N_HEADS):
        lo = h * HEAD_SLOT
        qn = qf[:, lo:lo + LANES]
        qr = qf[:, lo + LANES:lo + HEAD_SLOT]
        ss = jnp.sum(qn * qn + qr * qr, axis=-1, keepdims=True)
        inv = lax.rsqrt(ss * inv_dim + EPS)
        q_ref[:, lo:lo + LANES] = (qn * inv * gq_n).astype(BF16)
        q_ref[:, lo + LANES:lo + HEAD_SLOT] = _rope(qr * inv * gq_r, cos, sin).astype(BF16)
        kn = kv[:, h * LANES:(h + 1) * LANES]
        ssk = jnp.sum(kn * kn, axis=-1, keepdims=True) + ss_kr
        invk = lax.rsqrt(ssk * inv_dim + EPS)
        k_ref[:, lo:lo + LANES] = (kn * invk * gk_n).astype(BF16)
        k_ref[:, lo + LANES:lo + HEAD_SLOT] = (kr_rot * invk).astype(BF16)


def _qkv_call(lat, kr, cos, sin, wq, wk, wvt, gq, gk, tm):
    m = lat.shape[0]
    rank = Q_LORA_RANK
    wide = N_HEADS * HEAD_SLOT
    vw = N_HEADS * V_HEAD_DIM
    row = lambda i: (i, 0)
    fixed = lambda i: (0, 0)
    return pl.pallas_call(
        _qkv_kernel,
        grid=(m // tm,),
        in_specs=[
            pl.BlockSpec((tm, rank), lambda i: (i, 0)),
            pl.BlockSpec((tm, rank), lambda i: (i, 1)),
            pl.BlockSpec((tm, LANES), row),
            pl.BlockSpec((tm, LANES), row),
            pl.BlockSpec((tm, LANES), row),
            pl.BlockSpec(wq.shape, fixed),
            pl.BlockSpec(wk.shape, fixed),
            pl.BlockSpec(wvt.shape, fixed),
            pl.BlockSpec((1, HEAD_SLOT), fixed),
            pl.BlockSpec((1, HEAD_SLOT), fixed),
        ],
        out_specs=[
            pl.BlockSpec((tm, wide), row),
            pl.BlockSpec((tm, wide), row),
            pl.BlockSpec((None, vw, tm), lambda i: (i, 0, 0)),
        ],
        out_shape=[
            jax.ShapeDtypeStruct((m, wide), BF16),
            jax.ShapeDtypeStruct((m, wide), BF16),
            jax.ShapeDtypeStruct((m // tm, vw, tm), BF16),
        ],
        compiler_params=_params("parallel"),
        name="qkv",
    )(lat, lat, kr, cos, sin, wq, wk, wvt, gq, gk)


def _flash_kernel(q_ref, k_ref, vt_ref, o_ref, m_ref, acc_ref, s_ref, *, tq, heads):
    qi = pl.program_id(2)
    qs = [q_ref[:, h * HEAD_SLOT:(h + 1) * HEAD_SLOT] for h in range(heads)]
    ones = jnp.ones((SUM_ROWS, tq), BF16)
    m_ref[...] = jnp.full(m_ref.shape, -jnp.inf, F32)
    acc_ref[...] = jnp.zeros(acc_ref.shape, F32)

    def scores(h, j):
        start = pl.multiple_of(j * tq, tq)
        k = k_ref[pl.ds(start, tq), h * HEAD_SLOT:(h + 1) * HEAD_SLOT]
        return lax.dot_general(k, qs[h], _CONTRACT_LAST, preferred_element_type=F32)

    def head_step(h, j, s, diagonal):
        if diagonal:
            key = lax.broadcasted_iota(jnp.int32, (tq, tq), 0)
            qry = lax.broadcasted_iota(jnp.int32, (tq, tq), 1)
            s = jnp.where(key <= qry, s, -jnp.inf)
        m_prev = m_ref[h]
        s_ref[h] = s
        m_new = jnp.maximum(m_prev, jnp.max(s, axis=0, keepdims=True))
        p = jnp.exp2(s_ref[h] - m_new)
        alpha = jnp.exp2(m_prev - m_new)
        m_ref[h] = m_new
        vt = vt_ref[j, h * V_HEAD_DIM:(h + 1) * V_HEAD_DIM, :]
        vt = jnp.concatenate([vt, ones], axis=0)
        acc_ref[h] = alpha * acc_ref[h] + _dot(vt, p.astype(BF16))

    def step(j, diagonal):
        ss = [scores(h, j) for h in range(heads)]
        for h in range(heads):
            head_step(h, j, ss[h], diagonal)

    lax.fori_loop(0, qi, lambda j, c: step(j, False), None)
    step(qi, True)
    for h in range(heads):
        out = acc_ref[h, :V_HEAD_DIM, :] / acc_ref[h, V_HEAD_DIM:V_HEAD_DIM + 1, :]
        o_ref[:, h * V_HEAD_DIM:(h + 1) * V_HEAD_DIM] = out.T.astype(BF16)


def _flash_call(q, k, vt, tq, heads):
    b, s, _ = q.shape
    return pl.pallas_call(
        functools.partial(_flash_kernel, tq=tq, heads=heads),
        grid=(b, N_HEADS // heads, s // tq),
        in_specs=[
            pl.BlockSpec((None, tq, heads * HEAD_SLOT), lambda bi, h, i: (bi, i, h)),
            pl.BlockSpec((None, s, heads * HEAD_SLOT), lambda bi, h, i: (bi, 0, h)),
            pl.BlockSpec((None, s // tq, heads * V_HEAD_DIM, tq), lambda bi, h, i: (bi, 0, h, 0)),
        ],
        out_specs=pl.BlockSpec((None, tq, heads * V_HEAD_DIM), lambda bi, h, i: (bi, i, h)),
        out_shape=jax.ShapeDtypeStruct((b, s, N_HEADS * V_HEAD_DIM), BF16),
        scratch_shapes=[
            pltpu.VMEM((heads, 1, tq), F32),
            pltpu.VMEM((heads, V_HEAD_DIM + SUM_ROWS, tq), F32),
            pltpu.VMEM((heads, tq, tq), F32),
        ],
        compiler_params=_params("parallel", "parallel", "arbitrary"),
        name="flash",
    )(q, k, vt)


def _mix_kernel(x_ref, pooled_ref, attn_ref, gp_ref, ga_ref, wpo_ref, wao_ref, wout_ref, o_ref):
    y_pool = _dot(pooled_ref[...], wpo_ref[...])
    y_attn = _dot(attn_ref[...], wao_ref[...])
    mixed = gp_ref[...] * y_pool + ga_ref[...] * y_attn
    o_ref[...] = x_ref[...] + _dot(mixed.astype(BF16), wout_ref[...])


def _mix_call(x, pooled, attn, gates, w_po, w_ao, w_out, tm):
    m, d = x.shape
    row = lambda i: (i, 0)
    fixed = lambda i: (0, 0)
    return pl.pallas_call(
        _mix_kernel,
        grid=(m // tm,),
        in_specs=[
            pl.BlockSpec((tm, d), row),
            pl.BlockSpec((tm, pooled.shape[1]), row),
            pl.BlockSpec((tm, attn.shape[1]), row),
            pl.BlockSpec((tm, d), lambda i: (i, 0)),
            pl.BlockSpec((tm, d), lambda i: (i, 1)),
            pl.BlockSpec(w_po.shape, fixed),
            pl.BlockSpec(w_ao.shape, fixed),
            pl.BlockSpec(w_out.shape, fixed),
        ],
        out_specs=pl.BlockSpec((tm, d), row),
        out_shape=jax.ShapeDtypeStruct((m, d), F32),
        compiler_params=_params("parallel"),
        name="mix",
    )(x, pooled, attn, gates, gates, w_po, w_ao, w_out)


def _ffn_kernel(x_ref, g_ref, wg_ref, wu_ref, wd_ref, o_ref, h_ref):
    @pl.when(pl.program_id(1) == 0)
    def _():
        x = x_ref[...]
        h_ref[...] = _rms_bf16(x, g_ref[...])
        o_ref[...] = x

    h = h_ref[...]
    a = _dot(h, wg_ref[...].astype(BF16))
    u = _dot(h, wu_ref[...].astype(BF16))
    act = a * jax.nn.sigmoid(a) * u
    o_ref[...] += _dot(act.astype(BF16), wd_ref[...].astype(BF16))


def _ffn_call(x, g, w_gate, w_up, w_down, tm, tf):
    m, d = x.shape
    f = w_gate.shape[1]
    return pl.pallas_call(
        _ffn_kernel,
        grid=(m // tm, f // tf),
        in_specs=[
            pl.BlockSpec((tm, d), lambda i, j: (i, 0)),
            pl.BlockSpec((1, d), lambda i, j: (0, 0)),
            pl.BlockSpec((d, tf), lambda i, j: (0, j)),
            pl.BlockSpec((d, tf), lambda i, j: (0, j)),
            pl.BlockSpec((tf, d), lambda i, j: (j, 0)),
        ],
        out_specs=pl.BlockSpec((tm, d), lambda i, j: (i, 0)),
        out_shape=jax.ShapeDtypeStruct((m, d), F32),
        scratch_shapes=[pltpu.VMEM((tm, d), BF16)],
        compiler_params=_params("parallel", "arbitrary"),
        name="ffn",
    )(x, g, w_gate, w_up, w_down)


def _rope_slot(t):
    z = jnp.zeros(t.shape[:-1] + (ROPE_HALF,), t.dtype)
    return jnp.concatenate([t[..., :ROPE_HALF], z, t[..., ROPE_HALF:], z], axis=-1)


def _head_slot(t):
    return jnp.concatenate([t[..., :QK_NOPE_DIM], _rope_slot(t[..., QK_NOPE_DIM:])], axis=-1)


def kernel(x, positions, attn_norm_g, w_in, b_gate, q_a_norm_g, w_q_b, kv_a_norm_g, w_kv_b,
           q_norm_g, k_norm_g, w_attn_o, w_pool_grp, pool_scale, w_pool_o, w_out,
           ffn_norm_g, w_ffn_gate, w_ffn_up, w_ffn_down):
    b, s, d = x.shape
    depth = w_in.shape[0]
    m = b * s
    pool_width = w_pool_o.shape[1]

    inv_freq = ROPE_THETA ** (-jnp.arange(ROPE_HALF, dtype=F32) / ROPE_HALF)
    ang = positions.astype(F32).reshape(m, 1) * jnp.tile(inv_freq, 4)
    cos = jnp.cos(ang)
    sin = jnp.sin(ang) * jnp.repeat(jnp.array([-1.0, 1.0], F32), 2 * ROPE_HALF)

    xf = x.reshape(m, d)
    for l in range(depth):
        g_lat = jnp.concatenate([q_a_norm_g[l], kv_a_norm_g[l]]).reshape(1, -1)
        wq = _head_slot(w_q_b[l].reshape(Q_LORA_RANK, N_HEADS, QK_HEAD_DIM))
        wq = wq.reshape(Q_LORA_RANK, N_HEADS * HEAD_SLOT).astype(BF16)
        wkv = w_kv_b[l].reshape(KV_LORA_RANK, N_HEADS, QK_NOPE_DIM + V_HEAD_DIM)
        wk = wkv[..., :QK_NOPE_DIM].reshape(KV_LORA_RANK, -1).astype(BF16)
        wvt = wkv[..., QK_NOPE_DIM:].reshape(KV_LORA_RANK, -1).T.astype(BF16)
        gq = (_head_slot(q_norm_g[l]) * (QK_HEAD_DIM ** -0.5 * LOG2_E)).reshape(1, HEAD_SLOT)
        gk = _head_slot(k_norm_g[l]).reshape(1, HEAD_SLOT)
        g_attn_norm = attn_norm_g[l].reshape(1, d)

        lat, u_pool, gates, kr = _inproj_call(xf, g_attn_norm, w_in[l].T, g_lat,
                                              b_gate[l].reshape(1, -1), pool_width, tm=1024)
        pooled = _pool_call(u_pool, w_pool_grp[l].astype(BF16), pool_scale[l].reshape(1, -1),
                            tm=512, seq=s)
        tq = 512
        q, k, vt = _qkv_call(lat, kr, cos, sin, wq, wk, wvt, gq, gk, tm=tq)
        attn = _flash_call(q.reshape(b, s, -1), k.reshape(b, s, -1),
                           vt.reshape(b, s // tq, -1, tq), tq=tq, heads=4)
        xf = _mix_call(xf, pooled, attn.reshape(m, -1), gates, w_pool_o[l].astype(BF16),
                       w_attn_o[l].astype(BF16), w_out[l].astype(BF16), tm=256)
        xf = _ffn_call(xf, ffn_norm_g[l].reshape(1, d), w_ffn_gate[l], w_ffn_up[l], w_ffn_down[l],
                       tm=1024, tf=256)
    return xf.reshape(b, s, d)
```

```python
import functools

import jax
import jax.numpy as jnp
from jax import lax
from jax.experimental import pallas as pl
from jax.experimental.pallas import tpu as pltpu

F32 = jnp.float32
BF16 = jnp.bfloat16

N_HEADS = 16
QK_NOPE_DIM = 128
QK_ROPE_DIM = 64
QK_HEAD_DIM = QK_NOPE_DIM + QK_ROPE_DIM
V_HEAD_DIM = 128
Q_LORA_RANK = 512
KV_LORA_RANK = 512
ROPE_THETA = 10000.0
POOL_WINDOWS = (2, 4, 8, 16)
POOL_GROUPS = 4
EPS = 1e-6

LANES = 128
SUBLANES = 8
HEAD_SLOT = 2 * LANES
ROPE_HALF = QK_ROPE_DIM // 2
POOL_HALO = 16
VMEM_LIMIT = 56 * 1024 * 1024
SUM_ROWS = 16
LOG2_E = 1.4426950408889634
_CONTRACT_LAST = (((1,), (1,)), ((), ()))


def _params(*sem):
    return pltpu.CompilerParams(dimension_semantics=sem, vmem_limit_bytes=VMEM_LIMIT)


def _dot(a, b):
    return jnp.dot(a, b, preferred_element_type=F32)


def _rms_bf16(x, g):
    ms = jnp.mean(x * x, axis=-1, keepdims=True)
    return (x * lax.rsqrt(ms + EPS) * g).astype(BF16)


def _rope(t, cos, sin):
    return t * cos + pltpu.roll(t, 2 * ROPE_HALF, 1) * sin


def _dot_t(a, w):
    return lax.dot_general(a, w.astype(BF16), _CONTRACT_LAST, preferred_element_type=F32)


def _inproj_kernel(x_ref, g_ref, w_ref, gl_ref, b_ref, wkr_ref,
                   lat_ref, up_ref, gate_ref, kr_ref, h_ref, *, n_lat, n_up):
    j = pl.program_id(1)

    @pl.when(j == 0)
    def _():
        h_ref[...] = _rms_bf16(x_ref[...], g_ref[...])
        wkr = wkr_ref[...]
        z = jnp.zeros((ROPE_HALF, wkr.shape[1]), F32)
        slot = jnp.concatenate([wkr[:ROPE_HALF], z, wkr[ROPE_HALF:], z], axis=0)
        kr_ref[...] = _dot_t(h_ref[...], slot)

    @pl.when(j < n_lat)
    def _():
        lat_ref[...] = _rms_bf16(_dot_t(h_ref[...], w_ref[...]), gl_ref[...])

    @pl.when(jnp.logical_and(j >= n_lat, j < n_lat + n_up))
    def _():
        up_ref[...] = _dot_t(h_ref[...], w_ref[...])

    @pl.when(j >= n_lat + n_up)
    def _():
        gate_ref[...] = jax.nn.sigmoid(_dot_t(h_ref[...], w_ref[...]) + b_ref[...])


def _inproj_call(x, g, w_t, g_lat, b_gate, pool_width, tm):
    m, d = x.shape
    tn = Q_LORA_RANK
    n_lat = g_lat.shape[1] // tn
    n_up = pool_width // tn
    n_gate = b_gate.shape[1] // tn
    o_kr = n_lat * tn
    o_up = o_kr + QK_ROPE_DIM
    clamp = lambda j, lo, n: jnp.clip(j - lo, 0, n - 1)
    sub = SUBLANES
    w_row = lambda j: sub * jnp.where(j < n_lat, j * (tn // sub),
                                      o_up // sub + (j - n_lat) * (tn // sub))
    return pl.pallas_call(
        functools.partial(_inproj_kernel, n_lat=n_lat, n_up=n_up),
        grid=(m // tm, n_lat + n_up + n_gate),
        in_specs=[
            pl.BlockSpec((tm, d), lambda i, j: (i, 0)),
            pl.BlockSpec((1, d), lambda i, j: (0, 0)),
            pl.BlockSpec((pl.Element(tn), pl.Element(d)), lambda i, j: (w_row(j), 0)),
            pl.BlockSpec((1, tn), lambda i, j: (0, clamp(j, 0, n_lat))),
            pl.BlockSpec((1, tn), lambda i, j: (0, clamp(j, n_lat + n_up, n_gate))),
            pl.BlockSpec((pl.Element(QK_ROPE_DIM), pl.Element(d)), lambda i, j: (o_kr, 0)),
        ],
        out_specs=[
            pl.BlockSpec((tm, tn), lambda i, j: (i, clamp(j, 0, n_lat))),
            pl.BlockSpec((tm, tn), lambda i, j: (i, clamp(j, n_lat, n_up))),
            pl.BlockSpec((tm, tn), lambda i, j: (i, clamp(j, n_lat + n_up, n_gate))),
            pl.BlockSpec((tm, LANES), lambda i, j: (i, 0)),
        ],
        out_shape=[
            jax.ShapeDtypeStruct((m, n_lat * tn), BF16),
            jax.ShapeDtypeStruct((m, n_up * tn), F32),
            jax.ShapeDtypeStruct((m, n_gate * tn), F32),
            jax.ShapeDtypeStruct((m, LANES), F32),
        ],
        scratch_shapes=[pltpu.VMEM((tm, d), BF16)],
        compiler_params=_params("parallel", "arbitrary"),
        name="inproj",
    )(x, g, w_t, g_lat, b_gate, w_t)


def _pool_kernel(u_ref, halo_ref, wg_ref, sc_ref, o_ref, *, tm, seq):
    t0 = (pl.program_id(0) * tm) % seq
    halo = jnp.where(t0 > 0, halo_ref[...], 0.0)
    u = u_ref[...]
    ext = jnp.concatenate([halo, u], axis=0)
    pos = lax.broadcasted_iota(jnp.int32, (tm, 1), 0) + t0
    gd = wg_ref.shape[1]
    for g, w in enumerate(POOL_WINDOWS):
        cols = slice(g * gd, (g + 1) * gd)
        a = ext[:, cols]
        shift = 1
        while shift < w:
            a = a + pltpu.roll(a, shift, 0)
            shift *= 2
        count = jnp.minimum(pos + 1, w).astype(F32)
        pooled = a[POOL_HALO:, :] / count - u[:, cols]
        y = _dot(pooled.astype(BF16), wg_ref[g]) * sc_ref[:, cols]
        o_ref[:, cols] = y.astype(BF16)


def _pool_call(u, w_grp, scale, tm, seq):
    m, width = u.shape
    halo_blocks = tm // POOL_HALO
    return pl.pallas_call(
        functools.partial(_pool_kernel, tm=tm, seq=seq),
        grid=(m // tm,),
        in_specs=[
            pl.BlockSpec((tm, width), lambda i: (i, 0)),
            pl.BlockSpec((POOL_HALO, width), lambda i: (jnp.maximum(i * halo_blocks - 1, 0), 0)),
            pl.BlockSpec(w_grp.shape, lambda i: (0, 0, 0)),
            pl.BlockSpec((1, width), lambda i: (0, 0)),
        ],
        out_specs=pl.BlockSpec((tm, width), lambda i: (i, 0)),
        out_shape=jax.ShapeDtypeStruct((m, width), BF16),
        compiler_params=_params("parallel"),
        name="pool",
    )(u, u, w_grp, scale)


def _qkv_kernel(cq_ref, ckv_ref, kr_ref, cos_ref, sin_ref, wq_ref, wk_ref, wvt_ref,
                gq_ref, gk_ref, q_ref, k_ref, vt_ref):
    cos = cos_ref[...]
    sin = sin_ref[...]
    inv_dim = 1.0 / QK_HEAD_DIM
    ckv = ckv_ref[...]
    qf = _dot(cq_ref[...], wq_ref[...])
    kv = _dot(ckv, wk_ref[...])
    vt_ref[...] = lax.dot_general(wvt_ref[...], ckv, _CONTRACT_LAST,
                                  preferred_element_type=F32).astype(BF16)
    gq_n, gq_r = gq_ref[:, :LANES], gq_ref[:, LANES:]
    gk_n, gk_r = gk_ref[:, :LANES], gk_ref[:, LANES:]
    kr = kr_ref[...]
    ss_kr = jnp.sum(kr * kr, axis=-1, keepdims=True)
    kr_rot = _rope(kr * gk_r, cos, sin)
    for h in range(N_HEADS):
        lo = h * HEAD_SLOT
        qn = qf[:, lo:lo + LANES]
        qr = qf[:, lo + LANES:lo + HEAD_SLOT]
        ss = jnp.sum(qn * qn + qr * qr, axis=-1, keepdims=True)
        inv = lax.rsqrt(ss * inv_dim + EPS)
        q_ref[:, lo:lo + LANES] = (qn * inv * gq_n).astype(BF16)
        q_ref[:, lo + LANES:lo + HEAD_SLOT] = _rope(qr * inv * gq_r, cos, sin).astype(BF16)
        kn = kv[:, h * LANES:(h + 1) * LANES]
        ssk = jnp.sum(kn * kn, axis=-1, keepdims=True) + ss_kr
        invk = lax.rsqrt(ssk * inv_dim + EPS)
        k_ref[:, lo:lo + LANES] = (kn * invk * gk_n).astype(BF16)
        k_ref[:, lo + LANES:lo + HEAD_SLOT] = (kr_rot * invk).astype(BF16)


def _qkv_call(lat, kr, cos, sin, wq, wk, wvt, gq, gk, tm):
    m = lat.shape[0]
    rank = Q_LORA_RANK
    wide = N_HEADS * HEAD_SLOT
    vw = N_HEADS * V_HEAD_DIM
    row = lambda i: (i, 0)
    fixed = lambda i: (0, 0)
    return pl.pallas_call(
        _qkv_kernel,
        grid=(m // tm,),
        in_specs=[
            pl.BlockSpec((tm, rank), lambda i: (i, 0)),
            pl.BlockSpec((tm, rank), lambda i: (i, 1)),
            pl.BlockSpec((tm, LANES), row),
            pl.BlockSpec((tm, LANES), row),
            pl.BlockSpec((tm, LANES), row),
            pl.BlockSpec(wq.shape, fixed),
            pl.BlockSpec(wk.shape, fixed),
            pl.BlockSpec(wvt.shape, fixed),
            pl.BlockSpec((1, HEAD_SLOT), fixed),
            pl.BlockSpec((1, HEAD_SLOT), fixed),
        ],
        out_specs=[
            pl.BlockSpec((tm, wide), row),
            pl.BlockSpec((tm, wide), row),
            pl.BlockSpec((None, vw, tm), lambda i: (i, 0, 0)),
        ],
        out_shape=[
            jax.ShapeDtypeStruct((m, wide), BF16),
            jax.ShapeDtypeStruct((m, wide), BF16),
            jax.ShapeDtypeStruct((m // tm, vw, tm), BF16),
        ],
        compiler_params=_params("parallel"),
        name="qkv",
    )(lat, lat, kr, cos, sin, wq, wk, wvt, gq, gk)


def _flash_kernel(q_ref, k_ref, vt_ref, o_ref, m_ref, acc_ref, s_ref, mx_ref, *, tq, heads):
    n = pl.program_id(2)
    qs = [q_ref[:, h * HEAD_SLOT:(h + 1) * HEAD_SLOT] for h in range(heads)]
    ones = jnp.ones((SUM_ROWS, tq), BF16)
    m_ref[...] = jnp.full(m_ref.shape, -jnp.inf, F32)
    acc_ref[...] = jnp.zeros(acc_ref.shape, F32)

    def scores(h, c, buf, diagonal):
        start = pl.multiple_of(c * tq, tq)
        k = k_ref[pl.ds(start, tq), h * HEAD_SLOT:(h + 1) * HEAD_SLOT]
        s = lax.dot_general(k, qs[h], _CONTRACT_LAST, preferred_element_type=F32)
        if diagonal:
            key = lax.broadcasted_iota(jnp.int32, (tq, tq), 0)
            qry = lax.broadcasted_iota(jnp.int32, (tq, tq), 1)
            s = jnp.where(key <= qry, s, -jnp.inf)
        s_ref[buf, h] = s
        mx_ref[buf, h] = jnp.max(s, axis=0, keepdims=True)

    def accumulate(h, c, buf):
        m_prev = m_ref[h]
        m_new = jnp.maximum(m_prev, mx_ref[buf, h])
        p = jnp.exp2(s_ref[buf, h] - m_new)
        alpha = jnp.exp2(m_prev - m_new)
        m_ref[h] = m_new
        vt = vt_ref[c, h * V_HEAD_DIM:(h + 1) * V_HEAD_DIM, :]
        vt = jnp.concatenate([vt, ones], axis=0)
        acc_ref[h] = alpha * acc_ref[h] + _dot(vt, p.astype(BF16))

    def stage(c, buf, next_diagonal):
        for h in range(heads):
            scores(h, c + 1, 1 - buf, next_diagonal)
            accumulate(h, c, buf)

    def drain(c, buf):
        for h in range(heads):
            accumulate(h, c, buf)

    @pl.when(n == 0)
    def _():
        for h in range(heads):
            scores(h, 0, 0, True)
        drain(0, 0)

    @pl.when(n > 0)
    def _():
        for h in range(heads):
            scores(h, 0, 0, False)

    def two_stages(i, carry):
        stage(2 * i, 0, False)
        stage(2 * i + 1, 1, False)
        return carry

    lax.fori_loop(0, lax.div(n - 1, 2), two_stages, None)

    @pl.when(n % 2 == 1)
    def _():
        stage(n - 1, 0, True)
        drain(n, 1)

    @pl.when(jnp.logical_and(n >= 2, n % 2 == 0))
    def _():
        stage(n - 2, 0, False)
        stage(n - 1, 1, True)
        drain(n, 0)

    for h in range(heads):
        out = acc_ref[h, :V_HEAD_DIM, :] / acc_ref[h, V_HEAD_DIM:V_HEAD_DIM + 1, :]
        o_ref[:, h * V_HEAD_DIM:(h + 1) * V_HEAD_DIM] = out.T.astype(BF16)


def _flash_call(q, k, vt, tq, heads):
    b, s, _ = q.shape
    return pl.pallas_call(
        functools.partial(_flash_kernel, tq=tq, heads=heads),
        grid=(b, N_HEADS // heads, s // tq),
        in_specs=[
            pl.BlockSpec((None, tq, heads * HEAD_SLOT), lambda bi, h, i: (bi, i, h)),
            pl.BlockSpec((None, s, heads * HEAD_SLOT), lambda bi, h, i: (bi, 0, h)),
            pl.BlockSpec((None, s // tq, heads * V_HEAD_DIM, tq), lambda bi, h, i: (bi, 0, h, 0)),
        ],
        out_specs=pl.BlockSpec((None, tq, heads * V_HEAD_DIM), lambda bi, h, i: (bi, i, h)),
        out_shape=jax.ShapeDtypeStruct((b, s, N_HEADS * V_HEAD_DIM), BF16),
        scratch_shapes=[
            pltpu.VMEM((heads, 1, tq), F32),
            pltpu.VMEM((heads, V_HEAD_DIM + SUM_ROWS, tq), F32),
            pltpu.VMEM((2, heads, tq, tq), F32),
            pltpu.VMEM((2, heads, 1, tq), F32),
        ],
        compiler_params=_params("parallel", "parallel", "arbitrary"),
        name="flash",
    )(q, k, vt)


def _mix_kernel(x_ref, pooled_ref, attn_ref, gp_ref, ga_ref, wpo_ref, wao_ref, wout_ref, o_ref):
    y_pool = _dot(pooled_ref[...], wpo_ref[...])
    y_attn = _dot(attn_ref[...], wao_ref[...])
    mixed = gp_ref[...] * y_pool + ga_ref[...] * y_attn
    o_ref[...] = x_ref[...] + _dot(mixed.astype(BF16), wout_ref[...])


def _mix_call(x, pooled, attn, gates, w_po, w_ao, w_out, tm):
    m, d = x.shape
    row = lambda i: (i, 0)
    fixed = lambda i: (0, 0)
    return pl.pallas_call(
        _mix_kernel,
        grid=(m // tm,),
        in_specs=[
            pl.BlockSpec((tm, d), row),
            pl.BlockSpec((tm, pooled.shape[1]), row),
            pl.BlockSpec((tm, attn.shape[1]), row),
            pl.BlockSpec((tm, d), lambda i: (i, 0)),
            pl.BlockSpec((tm, d), lambda i: (i, 1)),
            pl.BlockSpec(w_po.shape, fixed),
            pl.BlockSpec(w_ao.shape, fixed),
            pl.BlockSpec(w_out.shape, fixed),
        ],
        out_specs=pl.BlockSpec((tm, d), row),
        out_shape=jax.ShapeDtypeStruct((m, d), F32),
        compiler_params=_params("parallel"),
        name="mix",
    )(x, pooled, attn, gates, gates, w_po, w_ao, w_out)


def _ffn_kernel(x_ref, g_ref, wg_ref, wu_ref, wd_ref, o_ref, h_ref):
    @pl.when(pl.program_id(1) == 0)
    def _():
        x = x_ref[...]
        h_ref[...] = _rms_bf16(x, g_ref[...])
        o_ref[...] = x

    h = h_ref[...]
    a = _dot(h, wg_ref[...].astype(BF16))
    u = _dot(h, wu_ref[...].astype(BF16))
    act = a * jax.nn.sigmoid(a) * u
    o_ref[...] += _dot(act.astype(BF16), wd_ref[...].astype(BF16))


def _ffn_call(x, g, w_gate, w_up, w_down, tm, tf):
    m, d = x.shape
    f = w_gate.shape[1]
    return pl.pallas_call(
        _ffn_kernel,
        grid=(m // tm, f // tf),
        in_specs=[
            pl.BlockSpec((tm, d), lambda i, j: (i, 0)),
            pl.BlockSpec((1, d), lambda i, j: (0, 0)),
            pl.BlockSpec((d, tf), lambda i, j: (0, j)),
            pl.BlockSpec((d, tf), lambda i, j: (0, j)),
            pl.BlockSpec((tf, d), lambda i, j: (j, 0)),
        ],
        out_specs=pl.BlockSpec((tm, d), lambda i, j: (i, 0)),
        out_shape=jax.ShapeDtypeStruct((m, d), F32),
        scratch_shapes=[pltpu.VMEM((tm, d), BF16)],
        compiler_params=_params("parallel", "arbitrary"),
        name="ffn",
    )(x, g, w_gate, w_up, w_down)


def _rope_slot(t):
    z = jnp.zeros(t.shape[:-1] + (ROPE_HALF,), t.dtype)
    return jnp.concatenate([t[..., :ROPE_HALF], z, t[..., ROPE_HALF:], z], axis=-1)


def _head_slot(t):
    return jnp.concatenate([t[..., :QK_NOPE_DIM], _rope_slot(t[..., QK_NOPE_DIM:])], axis=-1)


def kernel(x, positions, attn_norm_g, w_in, b_gate, q_a_norm_g, w_q_b, kv_a_norm_g, w_kv_b,
           q_norm_g, k_norm_g, w_attn_o, w_pool_grp, pool_scale, w_pool_o, w_out,
           ffn_norm_g, w_ffn_gate, w_ffn_up, w_ffn_down):
    b, s, d = x.shape
    depth = w_in.shape[0]
    m = b * s
    pool_width = w_pool_o.shape[1]

    inv_freq = ROPE_THETA ** (-jnp.arange(ROPE_HALF, dtype=F32) / ROPE_HALF)
    ang = positions.astype(F32).reshape(m, 1) * jnp.tile(inv_freq, 4)
    cos = jnp.cos(ang)
    sin = jnp.sin(ang) * jnp.repeat(jnp.array([-1.0, 1.0], F32), 2 * ROPE_HALF)

    xf = x.reshape(m, d)
    for l in range(depth):
        g_lat = jnp.concatenate([q_a_norm_g[l], kv_a_norm_g[l]]).reshape(1, -1)
        wq = _head_slot(w_q_b[l].reshape(Q_LORA_RANK, N_HEADS, QK_HEAD_DIM))
        wq = wq.reshape(Q_LORA_RANK, N_HEADS * HEAD_SLOT).astype(BF16)
        wkv = w_kv_b[l].reshape(KV_LORA_RANK, N_HEADS, QK_NOPE_DIM + V_HEAD_DIM)
        wk = wkv[..., :QK_NOPE_DIM].reshape(KV_LORA_RANK, -1).astype(BF16)
        wvt = wkv[..., QK_NOPE_DIM:].reshape(KV_LORA_RANK, -1).T.astype(BF16)
        gq = (_head_slot(q_norm_g[l]) * (QK_HEAD_DIM ** -0.5 * LOG2_E)).reshape(1, HEAD_SLOT)
        gk = _head_slot(k_norm_g[l]).reshape(1, HEAD_SLOT)
        g_attn_norm = attn_norm_g[l].reshape(1, d)

        lat, u_pool, gates, kr = _inproj_call(xf, g_attn_norm, w_in[l].T, g_lat,
                                              b_gate[l].reshape(1, -1), pool_width, tm=1024)
        pooled = _pool_call(u_pool, w_pool_grp[l].astype(BF16), pool_scale[l].reshape(1, -1),
                            tm=512, seq=s)
        tq = 512
        q, k, vt = _qkv_call(lat, kr, cos, sin, wq, wk, wvt, gq, gk, tm=tq)
        attn = _flash_call(q.reshape(b, s, -1), k.reshape(b, s, -1),
                           vt.reshape(b, s // tq, -1, tq), tq=tq, heads=4)
        xf = _mix_call(xf, pooled, attn.reshape(m, -1), gates, w_pool_o[l].astype(BF16),
                       w_attn_o[l].astype(BF16), w_out[l].astype(BF16), tm=256)
        xf = _ffn_call(xf, ffn_norm_g[l].reshape(1, d), w_ffn_gate[l], w_ffn_up[l], w_ffn_down[l],
                       tm=1024, tf=256)
    return xf.reshape(b, s, d)
```

```python
import functools

import jax
import jax.numpy as jnp
from jax import lax
from jax.experimental import pallas as pl
from jax.experimental.pallas import tpu as pltpu

F32 = jnp.float32
BF16 = jnp.bfloat16

N_HEADS = 16
QK_NOPE_DIM = 128
QK_ROPE_DIM = 64
QK_HEAD_DIM = QK_NOPE_DIM + QK_ROPE_DIM
V_HEAD_DIM = 128
Q_LORA_RANK = 512
KV_LORA_RANK = 512
ROPE_THETA = 10000.0
POOL_WINDOWS = (2, 4, 8, 16)
POOL_GROUPS = 4
EPS = 1e-6

LANES = 128
SUBLANES = 8
HEAD_SLOT = 2 * LANES
ROPE_HALF = QK_ROPE_DIM // 2
POOL_HALO = 16
VMEM_LIMIT = 56 * 1024 * 1024
SUM_ROWS = 16
LOG2_E = 1.4426950408889634
_CONTRACT_LAST = (((1,), (1,)), ((), ()))


def _params(*sem):
    return pltpu.CompilerParams(dimension_semantics=sem, vmem_limit_bytes=VMEM_LIMIT)


def _dot(a, b):
    return jnp.dot(a, b, preferred_element_type=F32)


def _rms_bf16(x, g):
    ms = jnp.mean(x * x, axis=-1, keepdims=True)
    return (x * lax.rsqrt(ms + EPS) * g).astype(BF16)


def _rope(t, cos, sin):
    return t * cos + pltpu.roll(t, 2 * ROPE_HALF, 1) * sin


def _dot_t(a, w):
    return lax.dot_general(a, w.astype(BF16), _CONTRACT_LAST, preferred_element_type=F32)


def _inproj_kernel(x_ref, g_ref, w_ref, gl_ref, b_ref, wkr_ref,
                   lat_ref, up_ref, gate_ref, kr_ref, h_ref, *, n_lat, n_up):
    j = pl.program_id(1)

    @pl.when(j == 0)
    def _():
        h_ref[...] = _rms_bf16(x_ref[...], g_ref[...])
        wkr = wkr_ref[...]
        z = jnp.zeros((ROPE_HALF, wkr.shape[1]), F32)
        slot = jnp.concatenate([wkr[:ROPE_HALF], z, wkr[ROPE_HALF:], z], axis=0)
        kr_ref[...] = _dot_t(h_ref[...], slot)

    @pl.when(j < n_lat)
    def _():
        lat_ref[...] = _rms_bf16(_dot_t(h_ref[...], w_ref[...]), gl_ref[...])

    @pl.when(jnp.logical_and(j >= n_lat, j < n_lat + n_up))
    def _():
        up_ref[...] = _dot_t(h_ref[...], w_ref[...])

    @pl.when(j >= n_lat + n_up)
    def _():
        gate_ref[...] = jax.nn.sigmoid(_dot_t(h_ref[...], w_ref[...]) + b_ref[...])


def _inproj_call(x, g, w_t, g_lat, b_gate, pool_width, tm):
    m, d = x.shape
    tn = Q_LORA_RANK
    n_lat = g_lat.shape[1] // tn
    n_up = pool_width // tn
    n_gate = b_gate.shape[1] // tn
    o_kr = n_lat * tn
    o_up = o_kr + QK_ROPE_DIM
    clamp = lambda j, lo, n: jnp.clip(j - lo, 0, n - 1)
    sub = SUBLANES
    w_row = lambda j: sub * jnp.where(j < n_lat, j * (tn // sub),
                                      o_up // sub + (j - n_lat) * (tn // sub))
    return pl.pallas_call(
        functools.partial(_inproj_kernel, n_lat=n_lat, n_up=n_up),
        grid=(m // tm, n_lat + n_up + n_gate),
        in_specs=[
            pl.BlockSpec((tm, d), lambda i, j: (i, 0)),
            pl.BlockSpec((1, d), lambda i, j: (0, 0)),
            pl.BlockSpec((pl.Element(tn), pl.Element(d)), lambda i, j: (w_row(j), 0)),
            pl.BlockSpec((1, tn), lambda i, j: (0, clamp(j, 0, n_lat))),
            pl.BlockSpec((1, tn), lambda i, j: (0, clamp(j, n_lat + n_up, n_gate))),
            pl.BlockSpec((pl.Element(QK_ROPE_DIM), pl.Element(d)), lambda i, j: (o_kr, 0)),
        ],
        out_specs=[
            pl.BlockSpec((tm, tn), lambda i, j: (i, clamp(j, 0, n_lat))),
            pl.BlockSpec((tm, tn), lambda i, j: (i, clamp(j, n_lat, n_up))),
            pl.BlockSpec((tm, tn), lambda i, j: (i, clamp(j, n_lat + n_up, n_gate))),
            pl.BlockSpec((tm, LANES), lambda i, j: (i, 0)),
        ],
        out_shape=[
            jax.ShapeDtypeStruct((m, n_lat * tn), BF16),
            jax.ShapeDtypeStruct((m, n_up * tn), F32),
            jax.ShapeDtypeStruct((m, n_gate * tn), F32),
            jax.ShapeDtypeStruct((m, LANES), F32),
        ],
        scratch_shapes=[pltpu.VMEM((tm, d), BF16)],
        compiler_params=_params("parallel", "arbitrary"),
        name="inproj",
    )(x, g, w_t, g_lat, b_gate, w_t)


def _pool_kernel(u_ref, halo_ref, wg_ref, sc_ref, o_ref, *, tm, seq):
    t0 = (pl.program_id(0) * tm) % seq
    halo = jnp.where(t0 > 0, halo_ref[...], 0.0)
    u = u_ref[...]
    ext = jnp.concatenate([halo, u], axis=0)
    pos = lax.broadcasted_iota(jnp.int32, (tm, 1), 0) + t0
    gd = wg_ref.shape[1]
    for g, w in enumerate(POOL_WINDOWS):
        cols = slice(g * gd, (g + 1) * gd)
        a = ext[:, cols]
        shift = 1
        while shift < w:
            a = a + pltpu.roll(a, shift, 0)
            shift *= 2
        count = jnp.minimum(pos + 1, w).astype(F32)
        pooled = a[POOL_HALO:, :] / count - u[:, cols]
        y = _dot(pooled.astype(BF16), wg_ref[g]) * sc_ref[:, cols]
        o_ref[:, cols] = y.astype(BF16)


def _pool_call(u, w_grp, scale, tm, seq):
    m, width = u.shape
    halo_blocks = tm // POOL_HALO
    return pl.pallas_call(
        functools.partial(_pool_kernel, tm=tm, seq=seq),
        grid=(m // tm,),
        in_specs=[
            pl.BlockSpec((tm, width), lambda i: (i, 0)),
            pl.BlockSpec((POOL_HALO, width), lambda i: (jnp.maximum(i * halo_blocks - 1, 0), 0)),
            pl.BlockSpec(w_grp.shape, lambda i: (0, 0, 0)),
            pl.BlockSpec((1, width), lambda i: (0, 0)),
        ],
        out_specs=pl.BlockSpec((tm, width), lambda i: (i, 0)),
        out_shape=jax.ShapeDtypeStruct((m, width), BF16),
        compiler_params=_params("parallel"),
        name="pool",
    )(u, u, w_grp, scale)


def _qkv_kernel(cq_ref, ckv_ref, kr_ref, cos_ref, sin_ref, wqt_ref, wk_ref, wvt_ref,
                gq_ref, gk_ref, qt_ref, k_ref, vt_ref):
    inv_dim = 1.0 / QK_HEAD_DIM
    half = ROPE_HALF
    cq = cq_ref[...]
    ckv = ckv_ref[...]
    cos_t = cos_ref[...]
    sin_t = sin_ref[...]
    qt = lax.dot_general(wqt_ref[...], cq, _CONTRACT_LAST, preferred_element_type=F32)
    vt_ref[...] = lax.dot_general(wvt_ref[...], ckv, _CONTRACT_LAST,
                                  preferred_element_type=F32).astype(BF16)
    gq = gq_ref[...]
    zeros = jnp.zeros((half, qt.shape[1]), BF16)
    for h in range(N_HEADS):
        src = h * QK_HEAD_DIM
        dst = h * HEAD_SLOT
        qh = qt[src:src + QK_HEAD_DIM]
        inv = lax.rsqrt(jnp.sum(qh * qh, axis=0, keepdims=True) * inv_dim + EPS)
        qh = qh * inv * gq
        t1 = qh[QK_NOPE_DIM:QK_NOPE_DIM + half]
        t2 = qh[QK_NOPE_DIM + half:]
        qt_ref[dst:dst + QK_NOPE_DIM] = qh[:QK_NOPE_DIM].astype(BF16)
        qt_ref[dst + LANES:dst + LANES + half] = (t1 * cos_t - t2 * sin_t).astype(BF16)
        qt_ref[dst + LANES + half:dst + LANES + 2 * half] = zeros
        qt_ref[dst + LANES + 2 * half:dst + LANES + 3 * half] = (t2 * cos_t + t1 * sin_t).astype(BF16)
        qt_ref[dst + LANES + 3 * half:dst + HEAD_SLOT] = zeros

    cos_r = cos_t.T
    sin_r = sin_t.T
    cos = jnp.concatenate([cos_r] * 4, axis=1)
    sin = jnp.concatenate([-sin_r, -sin_r, sin_r, sin_r], axis=1)
    kv = _dot(ckv, wk_ref[...])
    gk_n, gk_r = gk_ref[:, :LANES], gk_ref[:, LANES:]
    kr = kr_ref[...]
    ss_kr = jnp.sum(kr * kr, axis=-1, keepdims=True)
    kr_rot = _rope(kr * gk_r, cos, sin)
    for h in range(N_HEADS):
        lo = h * HEAD_SLOT
        kn = kv[:, h * LANES:(h + 1) * LANES]
        ssk = jnp.sum(kn * kn, axis=-1, keepdims=True) + ss_kr
        invk = lax.rsqrt(ssk * inv_dim + EPS)
        k_ref[:, lo:lo + LANES] = (kn * invk * gk_n).astype(BF16)
        k_ref[:, lo + LANES:lo + HEAD_SLOT] = (kr_rot * invk).astype(BF16)


def _qkv_call(lat, kr, cos_t, sin_t, wqt, wk, wvt, gq, gk, tm):
    m = lat.shape[0]
    rank = Q_LORA_RANK
    wide = N_HEADS * HEAD_SLOT
    vw = N_HEADS * V_HEAD_DIM
    row = lambda i: (i, 0)
    col = lambda i: (0, i)
    fixed = lambda i: (0, 0)
    return pl.pallas_call(
        _qkv_kernel,
        grid=(m // tm,),
        in_specs=[
            pl.BlockSpec((tm, rank), lambda i: (i, 0)),
            pl.BlockSpec((tm, rank), lambda i: (i, 1)),
            pl.BlockSpec((tm, LANES), row),
            pl.BlockSpec((ROPE_HALF, tm), col),
            pl.BlockSpec((ROPE_HALF, tm), col),
            pl.BlockSpec(wqt.shape, fixed),
            pl.BlockSpec(wk.shape, fixed),
            pl.BlockSpec(wvt.shape, fixed),
            pl.BlockSpec(gq.shape, fixed),
            pl.BlockSpec((1, HEAD_SLOT), fixed),
        ],
        out_specs=[
            pl.BlockSpec((None, wide, tm), lambda i: (i, 0, 0)),
            pl.BlockSpec((tm, wide), row),
            pl.BlockSpec((None, vw, tm), lambda i: (i, 0, 0)),
        ],
        out_shape=[
            jax.ShapeDtypeStruct((m // tm, wide, tm), BF16),
            jax.ShapeDtypeStruct((m, wide), BF16),
            jax.ShapeDtypeStruct((m // tm, vw, tm), BF16),
        ],
        compiler_params=_params("parallel"),
        name="qkv",
    )(lat, lat, kr, cos_t, sin_t, wqt, wk, wvt, gq, gk)


def _flash_kernel(q_ref, k_ref, vt_ref, o_ref, m_ref, acc_ref, s_ref, mx_ref, *, tq, heads):
    n = pl.program_id(2)
    qs = [q_ref[h * HEAD_SLOT:(h + 1) * HEAD_SLOT, :] for h in range(heads)]
    ones = jnp.ones((SUM_ROWS, tq), BF16)
    m_ref[...] = jnp.full(m_ref.shape, -jnp.inf, F32)
    acc_ref[...] = jnp.zeros(acc_ref.shape, F32)

    def scores(h, c, buf, diagonal):
        start = pl.multiple_of(c * tq, tq)
        k = k_ref[pl.ds(start, tq), h * HEAD_SLOT:(h + 1) * HEAD_SLOT]
        s = _dot(k, qs[h])
        if diagonal:
            key = lax.broadcasted_iota(jnp.int32, (tq, tq), 0)
            qry = lax.broadcasted_iota(jnp.int32, (tq, tq), 1)
            s = jnp.where(key <= qry, s, -jnp.inf)
        s_ref[buf, h] = s
        mx_ref[buf, h] = jnp.max(s, axis=0, keepdims=True)

    def accumulate(h, c, buf):
        m_prev = m_ref[h]
        m_new = jnp.maximum(m_prev, mx_ref[buf, h])
        p = jnp.exp2(s_ref[buf, h] - m_new)
        alpha = jnp.exp2(m_prev - m_new)
        m_ref[h] = m_new
        vt = vt_ref[c, h * V_HEAD_DIM:(h + 1) * V_HEAD_DIM, :]
        vt = jnp.concatenate([vt, ones], axis=0)
        acc_ref[h] = alpha * acc_ref[h] + _dot(vt, p.astype(BF16))

    def stage(c, buf, next_diagonal):
        for h in range(heads):
            scores(h, c + 1, 1 - buf, next_diagonal)
            accumulate(h, c, buf)

    def drain(c, buf):
        for h in range(heads):
            accumulate(h, c, buf)

    @pl.when(n == 0)
    def _():
        for h in range(heads):
            scores(h, 0, 0, True)
        drain(0, 0)

    @pl.when(n > 0)
    def _():
        for h in range(heads):
            scores(h, 0, 0, False)

    def two_stages(i, carry):
        stage(2 * i, 0, False)
        stage(2 * i + 1, 1, False)
        return carry

    lax.fori_loop(0, lax.div(n - 1, 2), two_stages, None)

    @pl.when(n % 2 == 1)
    def _():
        stage(n - 1, 0, True)
        drain(n, 1)

    @pl.when(jnp.logical_and(n >= 2, n % 2 == 0))
    def _():
        stage(n - 2, 0, False)
        stage(n - 1, 1, True)
        drain(n, 0)

    for h in range(heads):
        out = acc_ref[h, :V_HEAD_DIM, :] / acc_ref[h, V_HEAD_DIM:V_HEAD_DIM + 1, :]
        o_ref[:, h * V_HEAD_DIM:(h + 1) * V_HEAD_DIM] = out.T.astype(BF16)


def _flash_call(q, k, vt, tq, heads):
    b, s, _ = k.shape
    return pl.pallas_call(
        functools.partial(_flash_kernel, tq=tq, heads=heads),
        grid=(b, N_HEADS // heads, s // tq),
        in_specs=[
            pl.BlockSpec((None, None, heads * HEAD_SLOT, tq), lambda bi, h, i: (bi, i, h, 0)),
            pl.BlockSpec((None, s, heads * HEAD_SLOT), lambda bi, h, i: (bi, 0, h)),
            pl.BlockSpec((None, s // tq, heads * V_HEAD_DIM, tq), lambda bi, h, i: (bi, 0, h, 0)),
        ],
        out_specs=pl.BlockSpec((None, tq, heads * V_HEAD_DIM), lambda bi, h, i: (bi, i, h)),
        out_shape=jax.ShapeDtypeStruct((b, s, N_HEADS * V_HEAD_DIM), BF16),
        scratch_shapes=[
            pltpu.VMEM((heads, 1, tq), F32),
            pltpu.VMEM((heads, V_HEAD_DIM + SUM_ROWS, tq), F32),
            pltpu.VMEM((2, heads, tq, tq), F32),
            pltpu.VMEM((2, heads, 1, tq), F32),
        ],
        compiler_params=_params("parallel", "parallel", "arbitrary"),
        name="flash",
    )(q, k, vt)


def _mix_kernel(x_ref, pooled_ref, attn_ref, gp_ref, ga_ref, wpo_ref, wao_ref, wout_ref, o_ref):
    y_pool = _dot(pooled_ref[...], wpo_ref[...])
    y_attn = _dot(attn_ref[...], wao_ref[...])
    mixed = gp_ref[...] * y_pool + ga_ref[...] * y_attn
    o_ref[...] = x_ref[...] + _dot(mixed.astype(BF16), wout_ref[...])


def _mix_call(x, pooled, attn, gates, w_po, w_ao, w_out, tm):
    m, d = x.shape
    row = lambda i: (i, 0)
    fixed = lambda i: (0, 0)
    return pl.pallas_call(
        _mix_kernel,
        grid=(m // tm,),
        in_specs=[
            pl.BlockSpec((tm, d), row),
            pl.BlockSpec((tm, pooled.shape[1]), row),
            pl.BlockSpec((tm, attn.shape[1]), row),
            pl.BlockSpec((tm, d), lambda i: (i, 0)),
            pl.BlockSpec((tm, d), lambda i: (i, 1)),
            pl.BlockSpec(w_po.shape, fixed),
            pl.BlockSpec(w_ao.shape, fixed),
            pl.BlockSpec(w_out.shape, fixed),
        ],
        out_specs=pl.BlockSpec((tm, d), row),
        out_shape=jax.ShapeDtypeStruct((m, d), F32),
        compiler_params=_params("parallel"),
        name="mix",
    )(x, pooled, attn, gates, gates, w_po, w_ao, w_out)


def _ffn_kernel(x_ref, g_ref, wg_ref, wu_ref, wd_ref, o_ref, h_ref):
    @pl.when(pl.program_id(1) == 0)
    def _():
        x = x_ref[...]
        h_ref[...] = _rms_bf16(x, g_ref[...])
        o_ref[...] = x

    h = h_ref[...]
    a = _dot(h, wg_ref[...].astype(BF16))
    u = _dot(h, wu_ref[...].astype(BF16))
    act = a * jax.nn.sigmoid(a) * u
    o_ref[...] += _dot(act.astype(BF16), wd_ref[...].astype(BF16))


def _ffn_call(x, g, w_gate, w_up, w_down, tm, tf):
    m, d = x.shape
    f = w_gate.shape[1]
    return pl.pallas_call(
        _ffn_kernel,
        grid=(m // tm, f // tf),
        in_specs=[
            pl.BlockSpec((tm, d), lambda i, j: (i, 0)),
            pl.BlockSpec((1, d), lambda i, j: (0, 0)),
            pl.BlockSpec((d, tf), lambda i, j: (0, j)),
            pl.BlockSpec((d, tf), lambda i, j: (0, j)),
            pl.BlockSpec((tf, d), lambda i, j: (j, 0)),
        ],
        out_specs=pl.BlockSpec((tm, d), lambda i, j: (i, 0)),
        out_shape=jax.ShapeDtypeStruct((m, d), F32),
        scratch_shapes=[pltpu.VMEM((tm, d), BF16)],
        compiler_params=_params("parallel", "arbitrary"),
        name="ffn",
    )(x, g, w_gate, w_up, w_down)


def _rope_slot(t):
    z = jnp.zeros(t.shape[:-1] + (ROPE_HALF,), t.dtype)
    return jnp.concatenate([t[..., :ROPE_HALF], z, t[..., ROPE_HALF:], z], axis=-1)


def _head_slot(t):
    return jnp.concatenate([t[..., :QK_NOPE_DIM], _rope_slot(t[..., QK_NOPE_DIM:])], axis=-1)


def kernel(x, positions, attn_norm_g, w_in, b_gate, q_a_norm_g, w_q_b, kv_a_norm_g, w_kv_b,
           q_norm_g, k_norm_g, w_attn_o, w_pool_grp, pool_scale, w_pool_o, w_out,
           ffn_norm_g, w_ffn_gate, w_ffn_up, w_ffn_down):
    b, s, d = x.shape
    depth = w_in.shape[0]
    m = b * s
    pool_width = w_pool_o.shape[1]

    inv_freq = ROPE_THETA ** (-jnp.arange(ROPE_HALF, dtype=F32) / ROPE_HALF)
    ang_t = positions.astype(F32).reshape(1, m) * inv_freq.reshape(ROPE_HALF, 1)
    cos_t = jnp.cos(ang_t)
    sin_t = jnp.sin(ang_t)

    xf = x.reshape(m, d)
    for l in range(depth):
        g_lat = jnp.concatenate([q_a_norm_g[l], kv_a_norm_g[l]]).reshape(1, -1)
        wqt = w_q_b[l].T.astype(BF16)
        wkv = w_kv_b[l].reshape(KV_LORA_RANK, N_HEADS, QK_NOPE_DIM + V_HEAD_DIM)
        wk = wkv[..., :QK_NOPE_DIM].reshape(KV_LORA_RANK, -1).astype(BF16)
        wvt = wkv[..., QK_NOPE_DIM:].reshape(KV_LORA_RANK, -1).T.astype(BF16)
        tq = 512
        gq = jnp.broadcast_to((q_norm_g[l] * (QK_HEAD_DIM ** -0.5 * LOG2_E)).reshape(-1, 1),
                              (QK_HEAD_DIM, tq))
        gk = _head_slot(k_norm_g[l]).reshape(1, HEAD_SLOT)
        g_attn_norm = attn_norm_g[l].reshape(1, d)

        lat, u_pool, gates, kr = _inproj_call(xf, g_attn_norm, w_in[l].T, g_lat,
                                              b_gate[l].reshape(1, -1), pool_width, tm=1024)
        pooled = _pool_call(u_pool, w_pool_grp[l].astype(BF16), pool_scale[l].reshape(1, -1),
                            tm=512, seq=s)
        qt, k, vt = _qkv_call(lat, kr, cos_t, sin_t, wqt, wk, wvt, gq, gk, tm=tq)
        attn = _flash_call(qt.reshape(b, s // tq, -1, tq), k.reshape(b, s, -1),
                           vt.reshape(b, s // tq, -1, tq), tq=tq, heads=4)
        xf = _mix_call(xf, pooled, attn.reshape(m, -1), gates, w_pool_o[l].astype(BF16),
                       w_attn_o[l].astype(BF16), w_out[l].astype(BF16), tm=256)
        xf = _ffn_call(xf, ffn_norm_g[l].reshape(1, d), w_ffn_gate[l], w_ffn_up[l], w_ffn_down[l],
                       tm=1024, tf=256)
    return xf.reshape(b, s, d)
```

```python
import functools

import jax
import jax.numpy as jnp
from jax import lax
from jax.experimental import pallas as pl
from jax.experimental.pallas import tpu as pltpu

F32 = jnp.float32
BF16 = jnp.bfloat16

N_HEADS = 16
QK_NOPE_DIM = 128
QK_ROPE_DIM = 64
QK_HEAD_DIM = QK_NOPE_DIM + QK_ROPE_DIM
V_HEAD_DIM = 128
Q_LORA_RANK = 512
KV_LORA_RANK = 512
ROPE_THETA = 10000.0
POOL_WINDOWS = (2, 4, 8, 16)
POOL_GROUPS = 4
EPS = 1e-6

LANES = 128
SUBLANES = 8
HEAD_SLOT = 2 * LANES
ROPE_HALF = QK_ROPE_DIM // 2
POOL_HALO = 16
VMEM_LIMIT = 56 * 1024 * 1024
FFN_VMEM_LIMIT = 60 * 1024 * 1024
SUM_ROWS = 16
LOG2_E = 1.4426950408889634
_CONTRACT_LAST = (((1,), (1,)), ((), ()))


def _params(*sem, vmem_limit=VMEM_LIMIT):
    return pltpu.CompilerParams(dimension_semantics=sem, vmem_limit_bytes=vmem_limit)


def _dot(a, b):
    return jnp.dot(a, b, preferred_element_type=F32)


def _rms_bf16(x, g):
    ms = jnp.mean(x * x, axis=-1, keepdims=True)
    return (x * lax.rsqrt(ms + EPS) * g).astype(BF16)


def _rope(t, cos, sin):
    return t * cos + pltpu.roll(t, 2 * ROPE_HALF, 1) * sin


def _dot_t(a, w):
    return lax.dot_general(a, w.astype(BF16), _CONTRACT_LAST, preferred_element_type=F32)


def _inproj_kernel(x_ref, g_ref, w_ref, gl_ref, b_ref, wkr_ref,
                   lat_ref, up_ref, gate_ref, kr_ref, h_ref, *, n_lat, n_up):
    j = pl.program_id(1)

    @pl.when(j == 0)
    def _():
        h_ref[...] = _rms_bf16(x_ref[...], g_ref[...])
        wkr = wkr_ref[...]
        z = jnp.zeros((ROPE_HALF, wkr.shape[1]), F32)
        slot = jnp.concatenate([wkr[:ROPE_HALF], z, wkr[ROPE_HALF:], z], axis=0)
        kr_ref[...] = _dot_t(h_ref[...], slot)

    @pl.when(j < n_lat)
    def _():
        lat_ref[...] = _rms_bf16(_dot_t(h_ref[...], w_ref[...]), gl_ref[...])

    @pl.when(jnp.logical_and(j >= n_lat, j < n_lat + n_up))
    def _():
        up_ref[...] = _dot_t(h_ref[...], w_ref[...])

    @pl.when(j >= n_lat + n_up)
    def _():
        gate_ref[...] = jax.nn.sigmoid(_dot_t(h_ref[...], w_ref[...]) + b_ref[...])


def _inproj_call(x, g, w_t, g_lat, b_gate, pool_width, tm):
    m, d = x.shape
    tn = Q_LORA_RANK
    n_lat = g_lat.shape[1] // tn
    n_up = pool_width // tn
    n_gate = b_gate.shape[1] // tn
    o_kr = n_lat * tn
    o_up = o_kr + QK_ROPE_DIM
    clamp = lambda j, lo, n: jnp.clip(j - lo, 0, n - 1)
    sub = SUBLANES
    w_row = lambda j: sub * jnp.where(j < n_lat, j * (tn // sub),
                                      o_up // sub + (j - n_lat) * (tn // sub))
    return pl.pallas_call(
        functools.partial(_inproj_kernel, n_lat=n_lat, n_up=n_up),
        grid=(m // tm, n_lat + n_up + n_gate),
        in_specs=[
            pl.BlockSpec((tm, d), lambda i, j: (i, 0)),
            pl.BlockSpec((1, d), lambda i, j: (0, 0)),
            pl.BlockSpec((pl.Element(tn), pl.Element(d)), lambda i, j: (w_row(j), 0)),
            pl.BlockSpec((1, tn), lambda i, j: (0, clamp(j, 0, n_lat))),
            pl.BlockSpec((1, tn), lambda i, j: (0, clamp(j, n_lat + n_up, n_gate))),
            pl.BlockSpec((pl.Element(QK_ROPE_DIM), pl.Element(d)), lambda i, j: (o_kr, 0)),
        ],
        out_specs=[
            pl.BlockSpec((tm, tn), lambda i, j: (i, clamp(j, 0, n_lat))),
            pl.BlockSpec((tm, tn), lambda i, j: (i, clamp(j, n_lat, n_up))),
            pl.BlockSpec((tm, tn), lambda i, j: (i, clamp(j, n_lat + n_up, n_gate))),
            pl.BlockSpec((tm, LANES), lambda i, j: (i, 0)),
        ],
        out_shape=[
            jax.ShapeDtypeStruct((m, n_lat * tn), BF16),
            jax.ShapeDtypeStruct((m, n_up * tn), F32),
            jax.ShapeDtypeStruct((m, n_gate * tn), F32),
            jax.ShapeDtypeStruct((m, LANES), F32),
        ],
        scratch_shapes=[pltpu.VMEM((tm, d), BF16)],
        compiler_params=_params("parallel", "arbitrary"),
        name="inproj",
    )(x, g, w_t, g_lat, b_gate, w_t)


def _pool_mixer(u, halo, t0, wg_ref, sc_ref):
    tm = u.shape[0]
    halo = jnp.where(t0 > 0, halo, 0.0)
    ext = jnp.concatenate([halo, u], axis=0)
    pos = lax.broadcasted_iota(jnp.int32, (tm, 1), 0) + t0
    gd = wg_ref.shape[1]
    out = []
    for g, w in enumerate(POOL_WINDOWS):
        cols = slice(g * gd, (g + 1) * gd)
        a = ext[:, cols]
        shift = 1
        while shift < w:
            a = a + pltpu.roll(a, shift, 0)
            shift *= 2
        count = jnp.minimum(pos + 1, w).astype(F32)
        pooled = a[POOL_HALO:, :] / count - u[:, cols]
        y = _dot(pooled.astype(BF16), wg_ref[g]) * sc_ref[:, cols]
        out.append(y.astype(BF16))
    return jnp.concatenate(out, axis=1)


def _qkv_kernel(cq_ref, ckv_ref, kr_ref, cos_ref, sin_ref, wqt_ref, wk_ref, wvt_ref,
                gq_ref, gk_ref, qt_ref, k_ref, vt_ref):
    inv_dim = 1.0 / QK_HEAD_DIM
    half = ROPE_HALF
    cq = cq_ref[...]
    ckv = ckv_ref[...]
    cos_t = cos_ref[...]
    sin_t = sin_ref[...]
    qt = lax.dot_general(wqt_ref[...], cq, _CONTRACT_LAST, preferred_element_type=F32)
    vt_ref[...] = lax.dot_general(wvt_ref[...], ckv, _CONTRACT_LAST,
                                  preferred_element_type=F32).astype(BF16)
    gq = gq_ref[...]
    zeros = jnp.zeros((half, qt.shape[1]), BF16)
    for h in range(N_HEADS):
        src = h * QK_HEAD_DIM
        dst = h * HEAD_SLOT
        qh = qt[src:src + QK_HEAD_DIM]
        inv = lax.rsqrt(jnp.sum(qh * qh, axis=0, keepdims=True) * inv_dim + EPS)
        qh = qh * inv * gq
        t1 = qh[QK_NOPE_DIM:QK_NOPE_DIM + half]
        t2 = qh[QK_NOPE_DIM + half:]
        qt_ref[dst:dst + QK_NOPE_DIM] = qh[:QK_NOPE_DIM].astype(BF16)
        qt_ref[dst + LANES:dst + LANES + half] = (t1 * cos_t - t2 * sin_t).astype(BF16)
        qt_ref[dst + LANES + half:dst + LANES + 2 * half] = zeros
        qt_ref[dst + LANES + 2 * half:dst + LANES + 3 * half] = (t2 * cos_t + t1 * sin_t).astype(BF16)
        qt_ref[dst + LANES + 3 * half:dst + HEAD_SLOT] = zeros

    cos_r = cos_t.T
    sin_r = sin_t.T
    cos = jnp.concatenate([cos_r] * 4, axis=1)
    sin = jnp.concatenate([-sin_r, -sin_r, sin_r, sin_r], axis=1)
    kv = _dot(ckv, wk_ref[...])
    gk_n, gk_r = gk_ref[:, :LANES], gk_ref[:, LANES:]
    kr = kr_ref[...]
    ss_kr = jnp.sum(kr * kr, axis=-1, keepdims=True)
    kr_rot = _rope(kr * gk_r, cos, sin)
    for h in range(N_HEADS):
        lo = h * HEAD_SLOT
        kn = kv[:, h * LANES:(h + 1) * LANES]
        ssk = jnp.sum(kn * kn, axis=-1, keepdims=True) + ss_kr
        invk = lax.rsqrt(ssk * inv_dim + EPS)
        k_ref[:, lo:lo + LANES] = (kn * invk * gk_n).astype(BF16)
        k_ref[:, lo + LANES:lo + HEAD_SLOT] = (kr_rot * invk).astype(BF16)


def _qkv_call(lat, kr, cos_t, sin_t, wqt, wk, wvt, gq, gk, tm):
    m = lat.shape[0]
    rank = Q_LORA_RANK
    wide = N_HEADS * HEAD_SLOT
    vw = N_HEADS * V_HEAD_DIM
    row = lambda i: (i, 0)
    col = lambda i: (0, i)
    fixed = lambda i: (0, 0)
    return pl.pallas_call(
        _qkv_kernel,
        grid=(m // tm,),
        in_specs=[
            pl.BlockSpec((tm, rank), lambda i: (i, 0)),
            pl.BlockSpec((tm, rank), lambda i: (i, 1)),
            pl.BlockSpec((tm, LANES), row),
            pl.BlockSpec((ROPE_HALF, tm), col),
            pl.BlockSpec((ROPE_HALF, tm), col),
            pl.BlockSpec(wqt.shape, fixed),
            pl.BlockSpec(wk.shape, fixed),
            pl.BlockSpec(wvt.shape, fixed),
            pl.BlockSpec(gq.shape, fixed),
            pl.BlockSpec((1, HEAD_SLOT), fixed),
        ],
        out_specs=[
            pl.BlockSpec((None, wide, tm), lambda i: (i, 0, 0)),
            pl.BlockSpec((tm, wide), row),
            pl.BlockSpec((None, vw, tm), lambda i: (i, 0, 0)),
        ],
        out_shape=[
            jax.ShapeDtypeStruct((m // tm, wide, tm), BF16),
            jax.ShapeDtypeStruct((m, wide), BF16),
            jax.ShapeDtypeStruct((m // tm, vw, tm), BF16),
        ],
        compiler_params=_params("parallel"),
        name="qkv",
    )(lat, lat, kr, cos_t, sin_t, wqt, wk, wvt, gq, gk)


def _flash_kernel(q_ref, k_ref, vt_ref, o_ref, m_ref, acc_ref, s_ref, mx_ref, *, tq, heads):
    n = pl.program_id(2)
    qs = [q_ref[h * HEAD_SLOT:(h + 1) * HEAD_SLOT, :] for h in range(heads)]
    ones = jnp.ones((SUM_ROWS, tq), BF16)
    m_ref[...] = jnp.full(m_ref.shape, -jnp.inf, F32)
    acc_ref[...] = jnp.zeros(acc_ref.shape, F32)

    def scores(h, c, buf, diagonal):
        start = pl.multiple_of(c * tq, tq)
        k = k_ref[pl.ds(start, tq), h * HEAD_SLOT:(h + 1) * HEAD_SLOT]
        s = _dot(k, qs[h])
        if diagonal:
            key = lax.broadcasted_iota(jnp.int32, (tq, tq), 0)
            qry = lax.broadcasted_iota(jnp.int32, (tq, tq), 1)
            s = jnp.where(key <= qry, s, -jnp.inf)
        s_ref[buf, h] = s
        mx_ref[buf, h] = jnp.max(s, axis=0, keepdims=True)

    def accumulate(h, c, buf):
        m_prev = m_ref[h]
        m_new = jnp.maximum(m_prev, mx_ref[buf, h])
        p = jnp.exp2(s_ref[buf, h] - m_new)
        alpha = jnp.exp2(m_prev - m_new)
        m_ref[h] = m_new
        vt = vt_ref[c, h * V_HEAD_DIM:(h + 1) * V_HEAD_DIM, :]
        vt = jnp.concatenate([vt, ones], axis=0)
        acc_ref[h] = alpha * acc_ref[h] + _dot(vt, p.astype(BF16))

    def stage(c, buf, next_diagonal):
        for h in range(heads):
            scores(h, c + 1, 1 - buf, next_diagonal)
            accumulate(h, c, buf)

    def drain(c, buf):
        for h in range(heads):
            accumulate(h, c, buf)

    @pl.when(n == 0)
    def _():
        for h in range(heads):
            scores(h, 0, 0, True)
        drain(0, 0)

    @pl.when(n > 0)
    def _():
        for h in range(heads):
            scores(h, 0, 0, False)

    def two_stages(i, carry):
        stage(2 * i, 0, False)
        stage(2 * i + 1, 1, False)
        return carry

    lax.fori_loop(0, lax.div(n - 1, 2), two_stages, None)

    @pl.when(n % 2 == 1)
    def _():
        stage(n - 1, 0, True)
        drain(n, 1)

    @pl.when(jnp.logical_and(n >= 2, n % 2 == 0))
    def _():
        stage(n - 2, 0, False)
        stage(n - 1, 1, True)
        drain(n, 0)

    for h in range(heads):
        out = acc_ref[h, :V_HEAD_DIM, :] / acc_ref[h, V_HEAD_DIM:V_HEAD_DIM + 1, :]
        o_ref[:, h * V_HEAD_DIM:(h + 1) * V_HEAD_DIM] = out.T.astype(BF16)


def _flash_call(q, k, vt, tq, heads):
    b, s, _ = k.shape
    return pl.pallas_call(
        functools.partial(_flash_kernel, tq=tq, heads=heads),
        grid=(b, N_HEADS // heads, s // tq),
        in_specs=[
            pl.BlockSpec((None, None, heads * HEAD_SLOT, tq), lambda bi, h, i: (bi, i, h, 0)),
            pl.BlockSpec((None, s, heads * HEAD_SLOT), lambda bi, h, i: (bi, 0, h)),
            pl.BlockSpec((None, s // tq, heads * V_HEAD_DIM, tq), lambda bi, h, i: (bi, 0, h, 0)),
        ],
        out_specs=pl.BlockSpec((None, tq, heads * V_HEAD_DIM), lambda bi, h, i: (bi, i, h)),
        out_shape=jax.ShapeDtypeStruct((b, s, N_HEADS * V_HEAD_DIM), BF16),
        scratch_shapes=[
            pltpu.VMEM((heads, 1, tq), F32),
            pltpu.VMEM((heads, V_HEAD_DIM + SUM_ROWS, tq), F32),
            pltpu.VMEM((2, heads, tq, tq), F32),
            pltpu.VMEM((2, heads, 1, tq), F32),
        ],
        compiler_params=_params("parallel", "parallel", "arbitrary"),
        name="flash",
    )(q, k, vt)


def _mix_kernel(x_ref, u_ref, halo_ref, attn_ref, gp_ref, ga_ref, wg_ref, sc_ref,
                wpo_ref, wao_ref, wout_ref, o_ref, *, tm, seq):
    t0 = (pl.program_id(0) * tm) % seq
    pooled = _pool_mixer(u_ref[...], halo_ref[...], t0, wg_ref, sc_ref)
    y_pool = _dot(pooled, wpo_ref[...])
    y_attn = _dot(attn_ref[...], wao_ref[...])
    mixed = gp_ref[...] * y_pool + ga_ref[...] * y_attn
    o_ref[...] = x_ref[...] + _dot(mixed.astype(BF16), wout_ref[...])


def _mix_call(x, u_pool, attn, gates, w_grp, scale, w_po, w_ao, w_out, tm, seq):
    m, d = x.shape
    width = u_pool.shape[1]
    halo_blocks = tm // POOL_HALO
    row = lambda i: (i, 0)
    fixed = lambda i: (0, 0)
    return pl.pallas_call(
        functools.partial(_mix_kernel, tm=tm, seq=seq),
        grid=(m // tm,),
        in_specs=[
            pl.BlockSpec((tm, d), row),
            pl.BlockSpec((tm, width), row),
            pl.BlockSpec((POOL_HALO, width), lambda i: (jnp.maximum(i * halo_blocks - 1, 0), 0)),
            pl.BlockSpec((tm, attn.shape[1]), row),
            pl.BlockSpec((tm, d), lambda i: (i, 0)),
            pl.BlockSpec((tm, d), lambda i: (i, 1)),
            pl.BlockSpec(w_grp.shape, lambda i: (0, 0, 0)),
            pl.BlockSpec((1, width), fixed),
            pl.BlockSpec(w_po.shape, fixed),
            pl.BlockSpec(w_ao.shape, fixed),
            pl.BlockSpec(w_out.shape, fixed),
        ],
        out_specs=pl.BlockSpec((tm, d), row),
        out_shape=jax.ShapeDtypeStruct((m, d), F32),
        compiler_params=_params("parallel"),
        name="mix",
    )(x, u_pool, u_pool, attn, gates, gates, w_grp, scale, w_po, w_ao, w_out)


def _ffn_kernel(x_ref, g_ref, wg_ref, wu_ref, wd_ref, o_ref, h_ref):
    @pl.when(pl.program_id(1) == 0)
    def _():
        x = x_ref[...]
        h_ref[...] = _rms_bf16(x, g_ref[...])
        o_ref[...] = x

    h = h_ref[...]
    a = _dot(h, wg_ref[...].astype(BF16))
    u = _dot(h, wu_ref[...].astype(BF16))
    act = a * jax.nn.sigmoid(a) * u
    o_ref[...] += _dot(act.astype(BF16), wd_ref[...].astype(BF16))


def _ffn_call(x, g, w_gate, w_up, w_down, tm, tf):
    m, d = x.shape
    f = w_gate.shape[1]
    return pl.pallas_call(
        _ffn_kernel,
        grid=(m // tm, f // tf),
        in_specs=[
            pl.BlockSpec((tm, d), lambda i, j: (i, 0), pipeline_mode=pl.Buffered(1)),
            pl.BlockSpec((1, d), lambda i, j: (0, 0)),
            pl.BlockSpec((d, tf), lambda i, j: (0, j)),
            pl.BlockSpec((d, tf), lambda i, j: (0, j)),
            pl.BlockSpec((tf, d), lambda i, j: (j, 0)),
        ],
        out_specs=pl.BlockSpec((tm, d), lambda i, j: (i, 0)),
        out_shape=jax.ShapeDtypeStruct((m, d), F32),
        scratch_shapes=[pltpu.VMEM((tm, d), BF16)],
        compiler_params=_params("parallel", "arbitrary", vmem_limit=FFN_VMEM_LIMIT),
        name="ffn",
    )(x, g, w_gate, w_up, w_down)


def _rope_slot(t):
    z = jnp.zeros(t.shape[:-1] + (ROPE_HALF,), t.dtype)
    return jnp.concatenate([t[..., :ROPE_HALF], z, t[..., ROPE_HALF:], z], axis=-1)


def _head_slot(t):
    return jnp.concatenate([t[..., :QK_NOPE_DIM], _rope_slot(t[..., QK_NOPE_DIM:])], axis=-1)


def kernel(x, positions, attn_norm_g, w_in, b_gate, q_a_norm_g, w_q_b, kv_a_norm_g, w_kv_b,
           q_norm_g, k_norm_g, w_attn_o, w_pool_grp, pool_scale, w_pool_o, w_out,
           ffn_norm_g, w_ffn_gate, w_ffn_up, w_ffn_down):
    b, s, d = x.shape
    depth = w_in.shape[0]
    m = b * s
    pool_width = w_pool_o.shape[1]

    inv_freq = ROPE_THETA ** (-jnp.arange(ROPE_HALF, dtype=F32) / ROPE_HALF)
    ang_t = positions.astype(F32).reshape(1, m) * inv_freq.reshape(ROPE_HALF, 1)
    cos_t = jnp.cos(ang_t)
    sin_t = jnp.sin(ang_t)

    xf = x.reshape(m, d)
    for l in range(depth):
        g_lat = jnp.concatenate([q_a_norm_g[l], kv_a_norm_g[l]]).reshape(1, -1)
        wqt = w_q_b[l].T.astype(BF16)
        wkv = w_kv_b[l].reshape(KV_LORA_RANK, N_HEADS, QK_NOPE_DIM + V_HEAD_DIM)
        wk = wkv[..., :QK_NOPE_DIM].reshape(KV_LORA_RANK, -1).astype(BF16)
        wvt = wkv[..., QK_NOPE_DIM:].reshape(KV_LORA_RANK, -1).T.astype(BF16)
        tq = 512
        gq = jnp.broadcast_to((q_norm_g[l] * (QK_HEAD_DIM ** -0.5 * LOG2_E)).reshape(-1, 1),
                              (QK_HEAD_DIM, tq))
        gk = _head_slot(k_norm_g[l]).reshape(1, HEAD_SLOT)
        g_attn_norm = attn_norm_g[l].reshape(1, d)

        lat, u_pool, gates, kr = _inproj_call(xf, g_attn_norm, w_in[l].T, g_lat,
                                              b_gate[l].reshape(1, -1), pool_width, tm=1024)
        qt, k, vt = _qkv_call(lat, kr, cos_t, sin_t, wqt, wk, wvt, gq, gk, tm=tq)
        attn = _flash_call(qt.reshape(b, s // tq, -1, tq), k.reshape(b, s, -1),
                           vt.reshape(b, s // tq, -1, tq), tq=tq, heads=4)
        xf = _mix_call(xf, u_pool, attn.reshape(m, -1), gates, w_pool_grp[l].astype(BF16),
                       pool_scale[l].reshape(1, -1), w_pool_o[l].astype(BF16),
                       w_attn_o[l].astype(BF16), w_out[l].astype(BF16), tm=256, seq=s)
        xf = _ffn_call(xf, ffn_norm_g[l].reshape(1, d), w_ffn_gate[l], w_ffn_up[l], w_ffn_down[l],
                       tm=1024, tf=512)
    return xf.reshape(b, s, d)
```

```python
import functools

import jax
import jax.numpy as jnp
from jax import lax
from jax.experimental import pallas as pl
from jax.experimental.pallas import tpu as pltpu

F32 = jnp.float32
BF16 = jnp.bfloat16

N_HEADS = 16
QK_NOPE_DIM = 128
QK_ROPE_DIM = 64
QK_HEAD_DIM = QK_NOPE_DIM + QK_ROPE_DIM
V_HEAD_DIM = 128
Q_LORA_RANK = 512
KV_LORA_RANK = 512
ROPE_THETA = 10000.0
POOL_WINDOWS = (2, 4, 8, 16)
POOL_GROUPS = 4
EPS = 1e-6

LANES = 128
SUBLANES = 8
HEAD_SLOT = 2 * LANES
ROPE_HALF = QK_ROPE_DIM // 2
POOL_HALO = 16
VMEM_LIMIT = 56 * 1024 * 1024
SUM_ROWS = 16
LOG2_E = 1.4426950408889634
_CONTRACT_LAST = (((1,), (1,)), ((), ()))


def _params(*sem):
    return pltpu.CompilerParams(dimension_semantics=sem, vmem_limit_bytes=VMEM_LIMIT)


def _dot(a, b):
    return jnp.dot(a, b, preferred_element_type=F32)


def _rms_bf16(x, g):
    ms = jnp.mean(x * x, axis=-1, keepdims=True)
    return (x * lax.rsqrt(ms + EPS) * g).astype(BF16)


def _rope(t, cos, sin):
    return t * cos + pltpu.roll(t, 2 * ROPE_HALF, 1) * sin


def _dot_t(a, w):
    return lax.dot_general(a, w.astype(BF16), _CONTRACT_LAST, preferred_element_type=F32)


def _inproj_kernel(x_ref, g_ref, w_ref, gl_ref, b_ref, wkr_ref,
                   lat_ref, up_ref, gate_ref, kr_ref, h_ref, *, n_lat, n_up):
    j = pl.program_id(1)

    @pl.when(j == 0)
    def _():
        h_ref[...] = _rms_bf16(x_ref[...], g_ref[...])
        wkr = wkr_ref[...]
        z = jnp.zeros((ROPE_HALF, wkr.shape[1]), F32)
        slot = jnp.concatenate([wkr[:ROPE_HALF], z, wkr[ROPE_HALF:], z], axis=0)
        kr_ref[...] = _dot_t(h_ref[...], slot)

    @pl.when(j < n_lat)
    def _():
        lat_ref[...] = _rms_bf16(_dot_t(h_ref[...], w_ref[...]), gl_ref[...])

    @pl.when(jnp.logical_and(j >= n_lat, j < n_lat + n_up))
    def _():
        up_ref[...] = _dot_t(h_ref[...], w_ref[...])

    @pl.when(j >= n_lat + n_up)
    def _():
        gate_ref[...] = jax.nn.sigmoid(_dot_t(h_ref[...], w_ref[...]) + b_ref[...]).astype(BF16)


def _inproj_call(x, g, w_t, g_lat, b_gate, pool_width, tm):
    m, d = x.shape
    tn = Q_LORA_RANK
    n_lat = g_lat.shape[1] // tn
    n_up = pool_width // tn
    n_gate = b_gate.shape[1] // tn
    o_kr = n_lat * tn
    o_up = o_kr + QK_ROPE_DIM
    clamp = lambda j, lo, n: jnp.clip(j - lo, 0, n - 1)
    sub = SUBLANES
    w_row = lambda j: sub * jnp.where(j < n_lat, j * (tn // sub),
                                      o_up // sub + (j - n_lat) * (tn // sub))
    return pl.pallas_call(
        functools.partial(_inproj_kernel, n_lat=n_lat, n_up=n_up),
        grid=(m // tm, n_lat + n_up + n_gate),
        in_specs=[
            pl.BlockSpec((tm, d), lambda i, j: (i, 0)),
            pl.BlockSpec((1, d), lambda i, j: (0, 0)),
            pl.BlockSpec((pl.Element(tn), pl.Element(d)), lambda i, j: (w_row(j), 0)),
            pl.BlockSpec((1, tn), lambda i, j: (0, clamp(j, 0, n_lat))),
            pl.BlockSpec((1, tn), lambda i, j: (0, clamp(j, n_lat + n_up, n_gate))),
            pl.BlockSpec((pl.Element(QK_ROPE_DIM), pl.Element(d)), lambda i, j: (o_kr, 0)),
        ],
        out_specs=[
            pl.BlockSpec((tm, tn), lambda i, j: (i, clamp(j, 0, n_lat))),
            pl.BlockSpec((tm, tn), lambda i, j: (i, clamp(j, n_lat, n_up))),
            pl.BlockSpec((tm, tn), lambda i, j: (i, clamp(j, n_lat + n_up, n_gate))),
            pl.BlockSpec((tm, LANES), lambda i, j: (i, 0)),
        ],
        out_shape=[
            jax.ShapeDtypeStruct((m, n_lat * tn), BF16),
            jax.ShapeDtypeStruct((m, n_up * tn), F32),
            jax.ShapeDtypeStruct((m, n_gate * tn), BF16),
            jax.ShapeDtypeStruct((m, LANES), F32),
        ],
        scratch_shapes=[pltpu.VMEM((tm, d), BF16)],
        compiler_params=_params("parallel", "arbitrary"),
        name="inproj",
    )(x, g, w_t, g_lat, b_gate, w_t)


def _pool_mixer(u, halo, t0, wg_ref, sc_ref):
    tm = u.shape[0]
    halo = jnp.where(t0 > 0, halo, 0.0)
    ext = jnp.concatenate([halo, u], axis=0)
    pos = lax.broadcasted_iota(jnp.int32, (tm, 1), 0) + t0
    gd = wg_ref.shape[1]
    out = []
    for g, w in enumerate(POOL_WINDOWS):
        cols = slice(g * gd, (g + 1) * gd)
        a = ext[:, cols]
        shift = 1
        while shift < w:
            a = a + pltpu.roll(a, shift, 0)
            shift *= 2
        count = jnp.minimum(pos + 1, w).astype(F32)
        pooled = a[POOL_HALO:, :] / count - u[:, cols]
        y = _dot(pooled.astype(BF16), wg_ref[g]) * sc_ref[:, cols]
        out.append(y.astype(BF16))
    return jnp.concatenate(out, axis=1)


def _qkv_kernel(cq_ref, ckv_ref, kr_ref, cos_ref, sin_ref, wqt_ref, wk_ref, wvt_ref,
                gq_ref, gk_ref, qt_ref, k_ref, vt_ref):
    inv_dim = 1.0 / QK_HEAD_DIM
    half = ROPE_HALF
    cq = cq_ref[...]
    ckv = ckv_ref[...]
    cos_t = cos_ref[...]
    sin_t = sin_ref[...]
    qt = lax.dot_general(wqt_ref[...], cq, _CONTRACT_LAST, preferred_element_type=F32)
    vt_ref[...] = lax.dot_general(wvt_ref[...], ckv, _CONTRACT_LAST,
                                  preferred_element_type=F32).astype(BF16)
    gq = gq_ref[...]
    zeros = jnp.zeros((half, qt.shape[1]), BF16)
    for h in range(N_HEADS):
        src = h * QK_HEAD_DIM
        dst = h * HEAD_SLOT
        qh = qt[src:src + QK_HEAD_DIM]
        inv = lax.rsqrt(jnp.sum(qh * qh, axis=0, keepdims=True) * inv_dim + EPS)
        qh = qh * inv * gq
        t1 = qh[QK_NOPE_DIM:QK_NOPE_DIM + half]
        t2 = qh[QK_NOPE_DIM + half:]
        qt_ref[dst:dst + QK_NOPE_DIM] = qh[:QK_NOPE_DIM].astype(BF16)
        qt_ref[dst + LANES:dst + LANES + half] = (t1 * cos_t - t2 * sin_t).astype(BF16)
        qt_ref[dst + LANES + half:dst + LANES + 2 * half] = zeros
        qt_ref[dst + LANES + 2 * half:dst + LANES + 3 * half] = (t2 * cos_t + t1 * sin_t).astype(BF16)
        qt_ref[dst + LANES + 3 * half:dst + HEAD_SLOT] = zeros

    cos_r = cos_t.T
    sin_r = sin_t.T
    cos = jnp.concatenate([cos_r] * 4, axis=1)
    sin = jnp.concatenate([-sin_r, -sin_r, sin_r, sin_r], axis=1)
    kv = _dot(ckv, wk_ref[...])
    gk_n, gk_r = gk_ref[:, :LANES], gk_ref[:, LANES:]
    kr = kr_ref[...]
    ss_kr = jnp.sum(kr * kr, axis=-1, keepdims=True)
    kr_rot = _rope(kr * gk_r, cos, sin)
    for h in range(N_HEADS):
        lo = h * HEAD_SLOT
        kn = kv[:, h * LANES:(h + 1) * LANES]
        ssk = jnp.sum(kn * kn, axis=-1, keepdims=True) + ss_kr
        invk = lax.rsqrt(ssk * inv_dim + EPS)
        k_ref[:, lo:lo + LANES] = (kn * invk * gk_n).astype(BF16)
        k_ref[:, lo + LANES:lo + HEAD_SLOT] = (kr_rot * invk).astype(BF16)


def _qkv_call(lat, kr, cos_t, sin_t, wqt, wk, wvt, gq, gk, tm):
    m = lat.shape[0]
    rank = Q_LORA_RANK
    wide = N_HEADS * HEAD_SLOT
    vw = N_HEADS * V_HEAD_DIM
    row = lambda i: (i, 0)
    col = lambda i: (0, i)
    fixed = lambda i: (0, 0)
    return pl.pallas_call(
        _qkv_kernel,
        grid=(m // tm,),
        in_specs=[
            pl.BlockSpec((tm, rank), lambda i: (i, 0)),
            pl.BlockSpec((tm, rank), lambda i: (i, 1)),
            pl.BlockSpec((tm, LANES), row),
            pl.BlockSpec((ROPE_HALF, tm), col),
            pl.BlockSpec((ROPE_HALF, tm), col),
            pl.BlockSpec(wqt.shape, fixed),
            pl.BlockSpec(wk.shape, fixed),
            pl.BlockSpec(wvt.shape, fixed),
            pl.BlockSpec(gq.shape, fixed),
            pl.BlockSpec((1, HEAD_SLOT), fixed),
        ],
        out_specs=[
            pl.BlockSpec((None, wide, tm), lambda i: (i, 0, 0)),
            pl.BlockSpec((tm, wide), row),
            pl.BlockSpec((None, vw, tm), lambda i: (i, 0, 0)),
        ],
        out_shape=[
            jax.ShapeDtypeStruct((m // tm, wide, tm), BF16),
            jax.ShapeDtypeStruct((m, wide), BF16),
            jax.ShapeDtypeStruct((m // tm, vw, tm), BF16),
        ],
        compiler_params=_params("parallel"),
        name="qkv",
    )(lat, lat, kr, cos_t, sin_t, wqt, wk, wvt, gq, gk)


def _flash_kernel(q_ref, k_ref, vt_ref, o_ref, m_ref, acc_ref, s_ref, mx_ref, *, tq, heads):
    n = pl.program_id(2)
    qs = [q_ref[h * HEAD_SLOT:(h + 1) * HEAD_SLOT, :] for h in range(heads)]
    ones = jnp.ones((SUM_ROWS, tq), BF16)
    m_ref[...] = jnp.full(m_ref.shape, -jnp.inf, F32)
    acc_ref[...] = jnp.zeros(acc_ref.shape, F32)

    def scores(h, c, buf, diagonal):
        start = pl.multiple_of(c * tq, tq)
        k = k_ref[pl.ds(start, tq), h * HEAD_SLOT:(h + 1) * HEAD_SLOT]
        s = _dot(k, qs[h])
        if diagonal:
            key = lax.broadcasted_iota(jnp.int32, (tq, tq), 0)
            qry = lax.broadcasted_iota(jnp.int32, (tq, tq), 1)
            s = jnp.where(key <= qry, s, -jnp.inf)
        s_ref[buf, h] = s
        mx_ref[buf, h] = jnp.max(s, axis=0, keepdims=True)

    def accumulate(h, c, buf):
        m_prev = m_ref[h]
        m_new = jnp.maximum(m_prev, mx_ref[buf, h])
        p = jnp.exp2(s_ref[buf, h] - m_new)
        alpha = jnp.exp2(m_prev - m_new)
        m_ref[h] = m_new
        vt = vt_ref[c, h * V_HEAD_DIM:(h + 1) * V_HEAD_DIM, :]
        vt = jnp.concatenate([vt, ones], axis=0)
        acc_ref[h] = alpha * acc_ref[h] + _dot(vt, p.astype(BF16))

    def stage(c, buf, next_diagonal):
        for h in range(heads):
            scores(h, c + 1, 1 - buf, next_diagonal)
            accumulate(h, c, buf)

    def drain(c, buf):
        for h in range(heads):
            accumulate(h, c, buf)

    @pl.when(n == 0)
    def _():
        for h in range(heads):
            scores(h, 0, 0, True)
        drain(0, 0)

    @pl.when(n > 0)
    def _():
        for h in range(heads):
            scores(h, 0, 0, False)

    def two_stages(i, carry):
        stage(2 * i, 0, False)
        stage(2 * i + 1, 1, False)
        return carry

    lax.fori_loop(0, lax.div(n - 1, 2), two_stages, None)

    @pl.when(n % 2 == 1)
    def _():
        stage(n - 1, 0, True)
        drain(n, 1)

    @pl.when(jnp.logical_and(n >= 2, n % 2 == 0))
    def _():
        stage(n - 2, 0, False)
        stage(n - 1, 1, True)
        drain(n, 0)

    for h in range(heads):
        out = acc_ref[h, :V_HEAD_DIM, :] / acc_ref[h, V_HEAD_DIM:V_HEAD_DIM + 1, :]
        o_ref[:, h * V_HEAD_DIM:(h + 1) * V_HEAD_DIM] = out.T.astype(BF16)


def _flash_call(q, k, vt, tq, heads):
    b, s, _ = k.shape
    return pl.pallas_call(
        functools.partial(_flash_kernel, tq=tq, heads=heads),
        grid=(b, N_HEADS // heads, s // tq),
        in_specs=[
            pl.BlockSpec((None, None, heads * HEAD_SLOT, tq), lambda bi, h, i: (bi, i, h, 0)),
            pl.BlockSpec((None, s, heads * HEAD_SLOT), lambda bi, h, i: (bi, 0, h)),
            pl.BlockSpec((None, s // tq, heads * V_HEAD_DIM, tq), lambda bi, h, i: (bi, 0, h, 0)),
        ],
        out_specs=pl.BlockSpec((None, tq, heads * V_HEAD_DIM), lambda bi, h, i: (bi, i, h)),
        out_shape=jax.ShapeDtypeStruct((b, s, N_HEADS * V_HEAD_DIM), BF16),
        scratch_shapes=[
            pltpu.VMEM((heads, 1, tq), F32),
            pltpu.VMEM((heads, V_HEAD_DIM + SUM_ROWS, tq), F32),
            pltpu.VMEM((2, heads, tq, tq), F32),
            pltpu.VMEM((2, heads, 1, tq), F32),
        ],
        compiler_params=_params("parallel", "parallel", "arbitrary"),
        name="flash",
    )(q, k, vt)


def _mix_kernel(x_ref, u_ref, halo_ref, attn_ref, gp_ref, ga_ref, wg_ref, sc_ref,
                wpo_ref, wao_ref, wout_ref, o_ref, *, tm, seq):
    t0 = (pl.program_id(0) * tm) % seq
    pooled = _pool_mixer(u_ref[...], halo_ref[...], t0, wg_ref, sc_ref)
    y_pool = _dot(pooled, wpo_ref[...])
    y_attn = _dot(attn_ref[...], wao_ref[...])
    mixed = gp_ref[...] * y_pool + ga_ref[...] * y_attn
    o_ref[...] = x_ref[...] + _dot(mixed.astype(BF16), wout_ref[...])


def _mix_call(x, u_pool, attn, gates, w_grp, scale, w_po, w_ao, w_out, tm, seq):
    m, d = x.shape
    width = u_pool.shape[1]
    halo_blocks = tm // POOL_HALO
    row = lambda i: (i, 0)
    fixed = lambda i: (0, 0)
    return pl.pallas_call(
        functools.partial(_mix_kernel, tm=tm, seq=seq),
        grid=(m // tm,),
        in_specs=[
            pl.BlockSpec((tm, d), row),
            pl.BlockSpec((tm, width), row),
            pl.BlockSpec((POOL_HALO, width), lambda i: (jnp.maximum(i * halo_blocks - 1, 0), 0)),
            pl.BlockSpec((tm, attn.shape[1]), row),
            pl.BlockSpec((tm, d), lambda i: (i, 0)),
            pl.BlockSpec((tm, d), lambda i: (i, 1)),
            pl.BlockSpec(w_grp.shape, lambda i: (0, 0, 0)),
            pl.BlockSpec((1, width), fixed),
            pl.BlockSpec(w_po.shape, fixed),
            pl.BlockSpec(w_ao.shape, fixed),
            pl.BlockSpec(w_out.shape, fixed),
        ],
        out_specs=pl.BlockSpec((tm, d), row),
        out_shape=jax.ShapeDtypeStruct((m, d), F32),
        compiler_params=_params("parallel"),
        name="mix",
    )(x, u_pool, u_pool, attn, gates, gates, w_grp, scale, w_po, w_ao, w_out)


def _ffn_kernel(x_ref, g_ref, wg_ref, wu_ref, wd_ref, o_ref, h_ref):
    @pl.when(pl.program_id(1) == 0)
    def _():
        x = x_ref[...]
        h_ref[...] = _rms_bf16(x, g_ref[...])
        o_ref[...] = x

    h = h_ref[...]
    a = _dot(h, wg_ref[...].astype(BF16))
    u = _dot(h, wu_ref[...].astype(BF16))
    act = a * jax.nn.sigmoid(a) * u
    o_ref[...] += _dot(act.astype(BF16), wd_ref[...].astype(BF16))


def _ffn_call(x, g, w_gate, w_up, w_down, tm, tf):
    m, d = x.shape
    f = w_gate.shape[1]
    return pl.pallas_call(
        _ffn_kernel,
        grid=(m // tm, f // tf),
        in_specs=[
            pl.BlockSpec((tm, d), lambda i, j: (i, 0)),
            pl.BlockSpec((1, d), lambda i, j: (0, 0)),
            pl.BlockSpec((d, tf), lambda i, j: (0, j)),
            pl.BlockSpec((d, tf), lambda i, j: (0, j)),
            pl.BlockSpec((tf, d), lambda i, j: (j, 0)),
        ],
        out_specs=pl.BlockSpec((tm, d), lambda i, j: (i, 0)),
        out_shape=jax.ShapeDtypeStruct((m, d), F32),
        scratch_shapes=[pltpu.VMEM((tm, d), BF16)],
        compiler_params=_params("parallel", "arbitrary"),
        name="ffn",
    )(x, g, w_gate, w_up, w_down)


def _rope_slot(t):
    z = jnp.zeros(t.shape[:-1] + (ROPE_HALF,), t.dtype)
    return jnp.concatenate([t[..., :ROPE_HALF], z, t[..., ROPE_HALF:], z], axis=-1)


def _head_slot(t):
    return jnp.concatenate([t[..., :QK_NOPE_DIM], _rope_slot(t[..., QK_NOPE_DIM:])], axis=-1)


def kernel(x, positions, attn_norm_g, w_in, b_gate, q_a_norm_g, w_q_b, kv_a_norm_g, w_kv_b,
           q_norm_g, k_norm_g, w_attn_o, w_pool_grp, pool_scale, w_pool_o, w_out,
           ffn_norm_g, w_ffn_gate, w_ffn_up, w_ffn_down):
    b, s, d = x.shape
    depth = w_in.shape[0]
    m = b * s
    pool_width = w_pool_o.shape[1]

    inv_freq = ROPE_THETA ** (-jnp.arange(ROPE_HALF, dtype=F32) / ROPE_HALF)
    ang_t = positions.astype(F32).reshape(1, m) * inv_freq.reshape(ROPE_HALF, 1)
    cos_t = jnp.cos(ang_t)
    sin_t = jnp.sin(ang_t)

    xf = x.reshape(m, d)
    for l in range(depth):
        g_lat = jnp.concatenate([q_a_norm_g[l], kv_a_norm_g[l]]).reshape(1, -1)
        wqt = w_q_b[l].T.astype(BF16)
        wkv = w_kv_b[l].reshape(KV_LORA_RANK, N_HEADS, QK_NOPE_DIM + V_HEAD_DIM)
        wk = wkv[..., :QK_NOPE_DIM].reshape(KV_LORA_RANK, -1).astype(BF16)
        wvt = wkv[..., QK_NOPE_DIM:].reshape(KV_LORA_RANK, -1).T.astype(BF16)
        tq = 512
        gq = jnp.broadcast_to((q_norm_g[l] * (QK_HEAD_DIM ** -0.5 * LOG2_E)).reshape(-1, 1),
                              (QK_HEAD_DIM, tq))
        gk = _head_slot(k_norm_g[l]).reshape(1, HEAD_SLOT)
        g_attn_norm = attn_norm_g[l].reshape(1, d)

        lat, u_pool, gates, kr = _inproj_call(xf, g_attn_norm, w_in[l].T, g_lat,
                                              b_gate[l].reshape(1, -1), pool_width, tm=1024)
        qt, k, vt = _qkv_call(lat, kr, cos_t, sin_t, wqt, wk, wvt, gq, gk, tm=tq)
        attn = _flash_call(qt.reshape(b, s // tq, -1, tq), k.reshape(b, s, -1),
                           vt.reshape(b, s // tq, -1, tq), tq=tq, heads=4)
        xf = _mix_call(xf, u_pool, attn.reshape(m, -1), gates, w_pool_grp[l].astype(BF16),
                       pool_scale[l].reshape(1, -1), w_pool_o[l].astype(BF16),
                       w_attn_o[l].astype(BF16), w_out[l].astype(BF16), tm=256, seq=s)
        xf = _ffn_call(xf, ffn_norm_g[l].reshape(1, d), w_ffn_gate[l], w_ffn_up[l], w_ffn_down[l],
                       tm=1024, tf=256)
    return xf.reshape(b, s, d)
```

```python
import functools

import jax
import jax.numpy as jnp
from jax import lax
from jax.experimental import pallas as pl
from jax.experimental.pallas import tpu as pltpu

F32 = jnp.float32
BF16 = jnp.bfloat16

N_HEADS = 16
QK_NOPE_DIM = 128
QK_ROPE_DIM = 64
QK_HEAD_DIM = QK_NOPE_DIM + QK_ROPE_DIM
V_HEAD_DIM = 128
Q_LORA_RANK = 512
KV_LORA_RANK = 512
ROPE_THETA = 10000.0
POOL_WINDOWS = (2, 4, 8, 16)
POOL_GROUPS = 4
EPS = 1e-6

LANES = 128
SUBLANES = 8
HEAD_SLOT = 2 * LANES
ROPE_HALF = QK_ROPE_DIM // 2
POOL_HALO = 16
VMEM_LIMIT = 56 * 1024 * 1024
SUM_ROWS = 16
LOG2_E = 1.4426950408889634
_CONTRACT_LAST = (((1,), (1,)), ((), ()))


def _params(*sem):
    return pltpu.CompilerParams(dimension_semantics=sem, vmem_limit_bytes=VMEM_LIMIT)


def _dot(a, b):
    return jnp.dot(a, b, preferred_element_type=F32)


def _rms_bf16(x, g):
    ms = jnp.mean(x * x, axis=-1, keepdims=True)
    return (x * lax.rsqrt(ms + EPS) * g).astype(BF16)


def _rope(t, cos, sin):
    return t * cos + pltpu.roll(t, 2 * ROPE_HALF, 1) * sin


def _dot_t(a, w):
    return lax.dot_general(a, w.astype(BF16), _CONTRACT_LAST, preferred_element_type=F32)


def _inproj_kernel(x_ref, g_ref, w_ref, gl_ref, b_ref, wkr_ref,
                   lat_ref, up_ref, gate_ref, kr_ref, h_ref, *, n_lat, n_up):
    j = pl.program_id(1)

    @pl.when(j == 0)
    def _():
        h_ref[...] = _rms_bf16(x_ref[...], g_ref[...])
        wkr = wkr_ref[...]
        z = jnp.zeros((ROPE_HALF, wkr.shape[1]), F32)
        slot = jnp.concatenate([wkr[:ROPE_HALF], z, wkr[ROPE_HALF:], z], axis=0)
        kr_ref[...] = _dot_t(h_ref[...], slot)

    @pl.when(j < n_lat)
    def _():
        lat_ref[...] = _rms_bf16(_dot_t(h_ref[...], w_ref[...]), gl_ref[...])

    @pl.when(jnp.logical_and(j >= n_lat, j < n_lat + n_up))
    def _():
        up_ref[...] = _dot_t(h_ref[...], w_ref[...])

    @pl.when(j >= n_lat + n_up)
    def _():
        gate_ref[...] = jax.nn.sigmoid(_dot_t(h_ref[...], w_ref[...]) + b_ref[...])


def _inproj_call(x, g, w_t, g_lat, b_gate, pool_width, tm):
    m, d = x.shape
    tn = Q_LORA_RANK
    n_lat = g_lat.shape[1] // tn
    n_up = pool_width // tn
    n_gate = b_gate.shape[1] // tn
    o_kr = n_lat * tn
    o_up = o_kr + QK_ROPE_DIM
    clamp = lambda j, lo, n: jnp.clip(j - lo, 0, n - 1)
    sub = SUBLANES
    w_row = lambda j: sub * jnp.where(j < n_lat, j * (tn // sub),
                                      o_up // sub + (j - n_lat) * (tn // sub))
    return pl.pallas_call(
        functools.partial(_inproj_kernel, n_lat=n_lat, n_up=n_up),
        grid=(m // tm, n_lat + n_up + n_gate),
        in_specs=[
            pl.BlockSpec((tm, d), lambda i, j: (i, 0)),
            pl.BlockSpec((1, d), lambda i, j: (0, 0)),
            pl.BlockSpec((pl.Element(tn), pl.Element(d)), lambda i, j: (w_row(j), 0)),
            pl.BlockSpec((1, tn), lambda i, j: (0, clamp(j, 0, n_lat))),
            pl.BlockSpec((1, tn), lambda i, j: (0, clamp(j, n_lat + n_up, n_gate))),
            pl.BlockSpec((pl.Element(QK_ROPE_DIM), pl.Element(d)), lambda i, j: (o_kr, 0)),
        ],
        out_specs=[
            pl.BlockSpec((tm, tn), lambda i, j: (i, clamp(j, 0, n_lat))),
            pl.BlockSpec((tm, tn), lambda i, j: (i, clamp(j, n_lat, n_up))),
            pl.BlockSpec((tm, tn), lambda i, j: (i, clamp(j, n_lat + n_up, n_gate))),
            pl.BlockSpec((tm, LANES), lambda i, j: (i, 0)),
        ],
        out_shape=[
            jax.ShapeDtypeStruct((m, n_lat * tn), BF16),
            jax.ShapeDtypeStruct((m, n_up * tn), F32),
            jax.ShapeDtypeStruct((m, n_gate * tn), F32),
            jax.ShapeDtypeStruct((m, LANES), F32),
        ],
        scratch_shapes=[pltpu.VMEM((tm, d), BF16)],
        compiler_params=_params("parallel", "arbitrary"),
        name="inproj",
    )(x, g, w_t, g_lat, b_gate, w_t)


def _pool_mixer(u, halo, t0, wg_ref, sc_ref):
    tm = u.shape[0]
    halo = jnp.where(t0 > 0, halo, 0.0)
    ext = jnp.concatenate([halo, u], axis=0)
    pos = lax.broadcasted_iota(jnp.int32, (tm, 1), 0) + t0
    gd = wg_ref.shape[1]
    out = []
    for g, w in enumerate(POOL_WINDOWS):
        cols = slice(g * gd, (g + 1) * gd)
        a = ext[:, cols]
        shift = 1
        while shift < w:
            a = a + pltpu.roll(a, shift, 0)
            shift *= 2
        count = jnp.minimum(pos + 1, w).astype(F32)
        pooled = a[POOL_HALO:, :] / count - u[:, cols]
        y = _dot(pooled.astype(BF16), wg_ref[g]) * sc_ref[:, cols]
        out.append(y.astype(BF16))
    return jnp.concatenate(out, axis=1)


def _qkv_kernel(cq_ref, ckv_ref, kr_ref, cos_ref, sin_ref, wqt_ref, wk_ref, wvt_ref,
                gq_ref, gk_ref, qt_ref, k_ref, vt_ref):
    inv_dim = 1.0 / QK_HEAD_DIM
    half = ROPE_HALF
    cq = cq_ref[...]
    ckv = ckv_ref[...]
    cos_t = cos_ref[...]
    sin_t = sin_ref[...]
    qt = lax.dot_general(wqt_ref[...], cq, _CONTRACT_LAST, preferred_element_type=F32)
    vt_ref[...] = lax.dot_general(wvt_ref[...], ckv, _CONTRACT_LAST,
                                  preferred_element_type=F32).astype(BF16)
    gq = gq_ref[...]
    zeros = jnp.zeros((half, qt.shape[1]), BF16)
    for h in range(N_HEADS):
        src = h * QK_HEAD_DIM
        dst = h * HEAD_SLOT
        qh = qt[src:src + QK_HEAD_DIM]
        inv = lax.rsqrt(jnp.sum(qh * qh, axis=0, keepdims=True) * inv_dim + EPS)
        qh = qh * inv * gq
        t1 = qh[QK_NOPE_DIM:QK_NOPE_DIM + half]
        t2 = qh[QK_NOPE_DIM + half:]
        qt_ref[dst:dst + QK_NOPE_DIM] = qh[:QK_NOPE_DIM].astype(BF16)
        qt_ref[dst + LANES:dst + LANES + half] = (t1 * cos_t - t2 * sin_t).astype(BF16)
        qt_ref[dst + LANES + half:dst + LANES + 2 * half] = zeros
        qt_ref[dst + LANES + 2 * half:dst + LANES + 3 * half] = (t2 * cos_t + t1 * sin_t).astype(BF16)
        qt_ref[dst + LANES + 3 * half:dst + HEAD_SLOT] = zeros

    cos_r = cos_t.T
    sin_r = sin_t.T
    cos = jnp.concatenate([cos_r] * 4, axis=1)
    sin = jnp.concatenate([-sin_r, -sin_r, sin_r, sin_r], axis=1)
    kv = _dot(ckv, wk_ref[...])
    gk_n, gk_r = gk_ref[:, :LANES], gk_ref[:, LANES:]
    kr = kr_ref[...]
    ss_kr = jnp.sum(kr * kr, axis=-1, keepdims=True)
    kr_rot = _rope(kr * gk_r, cos, sin)
    for h in range(N_HEADS):
        lo = h * HEAD_SLOT
        kn = kv[:, h * LANES:(h + 1) * LANES]
        ssk = jnp.sum(kn * kn, axis=-1, keepdims=True) + ss_kr
        invk = lax.rsqrt(ssk * inv_dim + EPS)
        k_ref[:, lo:lo + LANES] = (kn * invk * gk_n).astype(BF16)
        k_ref[:, lo + LANES:lo + HEAD_SLOT] = (kr_rot * invk).astype(BF16)


def _qkv_call(lat, kr, cos_t, sin_t, wqt, wk, wvt, gq, gk, tm):
    m = lat.shape[0]
    rank = Q_LORA_RANK
    wide = N_HEADS * HEAD_SLOT
    vw = N_HEADS * V_HEAD_DIM
    row = lambda i: (i, 0)
    col = lambda i: (0, i)
    fixed = lambda i: (0, 0)
    return pl.pallas_call(
        _qkv_kernel,
        grid=(m // tm,),
        in_specs=[
            pl.BlockSpec((tm, rank), lambda i: (i, 0)),
            pl.BlockSpec((tm, rank), lambda i: (i, 1)),
            pl.BlockSpec((tm, LANES), row),
            pl.BlockSpec((ROPE_HALF, tm), col),
            pl.BlockSpec((ROPE_HALF, tm), col),
            pl.BlockSpec(wqt.shape, fixed),
            pl.BlockSpec(wk.shape, fixed),
            pl.BlockSpec(wvt.shape, fixed),
            pl.BlockSpec(gq.shape, fixed),
            pl.BlockSpec((1, HEAD_SLOT), fixed),
        ],
        out_specs=[
            pl.BlockSpec((None, wide, tm), lambda i: (i, 0, 0)),
            pl.BlockSpec((tm, wide), row),
            pl.BlockSpec((None, vw, tm), lambda i: (i, 0, 0)),
        ],
        out_shape=[
            jax.ShapeDtypeStruct((m // tm, wide, tm), BF16),
            jax.ShapeDtypeStruct((m, wide), BF16),
            jax.ShapeDtypeStruct((m // tm, vw, tm), BF16),
        ],
        compiler_params=_params("parallel"),
        name="qkv",
    )(lat, lat, kr, cos_t, sin_t, wqt, wk, wvt, gq, gk)


def _flash_kernel(q_ref, k_ref, vt_ref, o_ref, m_ref, acc_ref, s_ref, mx_ref, *, tq, heads):
    n = pl.program_id(2)
    qs = [q_ref[h * HEAD_SLOT:(h + 1) * HEAD_SLOT, :] for h in range(heads)]
    ones = jnp.ones((SUM_ROWS, tq), BF16)
    m_ref[...] = jnp.full(m_ref.shape, -jnp.inf, F32)
    acc_ref[...] = jnp.zeros(acc_ref.shape, F32)

    half = tq // 2
    head_cols = lambda h: slice(h * HEAD_SLOT, (h + 1) * HEAD_SLOT)

    def v_rows(h, c):
        vt = vt_ref[c, h * V_HEAD_DIM:(h + 1) * V_HEAD_DIM, :]
        return jnp.concatenate([vt, ones], axis=0)

    def new_max(h, buf):
        m_prev = m_ref[h]
        m_new = jnp.maximum(m_prev, mx_ref[buf, h])
        m_ref[h] = m_new
        return m_new, jnp.exp2(m_prev - m_new)

    def scores(h, c, buf):
        start = pl.multiple_of(c * tq, tq)
        s = _dot(k_ref[pl.ds(start, tq), head_cols(h)], qs[h])
        s_ref[buf, h] = s
        mx_ref[buf, h] = jnp.max(s, axis=0, keepdims=True)

    def accumulate(h, c, buf):
        m_new, alpha = new_max(h, buf)
        p = jnp.exp2(s_ref[buf, h] - m_new)
        acc_ref[h] = alpha * acc_ref[h] + _dot(v_rows(h, c), p.astype(BF16))

    def scores_diagonal(h, c, buf):
        start = pl.multiple_of(c * tq, tq)
        causal = (lax.broadcasted_iota(jnp.int32, (half, tq), 0)
                  <= lax.broadcasted_iota(jnp.int32, (half, tq), 1))
        s0 = _dot(k_ref[pl.ds(start, half), head_cols(h)], qs[h])
        s1 = _dot(k_ref[pl.ds(start + half, half), head_cols(h)], qs[h][:, half:])
        s0 = jnp.where(causal, s0, -jnp.inf)
        s1 = jnp.where(causal[:, :half], s1, -jnp.inf)
        s_ref[buf, h, :half, :] = s0
        s_ref[buf, h, half:, half:] = s1
        m0 = jnp.max(s0, axis=0, keepdims=True)
        m1 = jnp.max(s1, axis=0, keepdims=True)
        mx_ref[buf, h] = jnp.concatenate([m0[:, :half], jnp.maximum(m0[:, half:], m1)], axis=1)

    def accumulate_diagonal(h, c, buf):
        m_new, alpha = new_max(h, buf)
        p0 = jnp.exp2(s_ref[buf, h, :half, :] - m_new).astype(BF16)
        p1 = jnp.exp2(s_ref[buf, h, half:, half:] - m_new[:, half:]).astype(BF16)
        vt = v_rows(h, c)
        acc = alpha * acc_ref[h] + _dot(vt[:, :half], p0)
        acc_ref[h, :, :half] = acc[:, :half]
        acc_ref[h, :, half:] = acc[:, half:] + _dot(vt[:, half:], p1)

    def stage(c, buf, next_diagonal):
        for h in range(heads):
            (scores_diagonal if next_diagonal else scores)(h, c + 1, 1 - buf)
            accumulate(h, c, buf)

    def drain(c, buf):
        for h in range(heads):
            accumulate_diagonal(h, c, buf)

    @pl.when(n == 0)
    def _():
        for h in range(heads):
            scores_diagonal(h, 0, 0)
        drain(0, 0)

    @pl.when(n > 0)
    def _():
        for h in range(heads):
            scores(h, 0, 0)

    def two_stages(i, carry):
        stage(2 * i, 0, False)
        stage(2 * i + 1, 1, False)
        return carry

    lax.fori_loop(0, lax.div(n - 1, 2), two_stages, None)

    @pl.when(n % 2 == 1)
    def _():
        stage(n - 1, 0, True)
        drain(n, 1)

    @pl.when(jnp.logical_and(n >= 2, n % 2 == 0))
    def _():
        stage(n - 2, 0, False)
        stage(n - 1, 1, True)
        drain(n, 0)

    for h in range(heads):
        out = acc_ref[h, :V_HEAD_DIM, :] / acc_ref[h, V_HEAD_DIM:V_HEAD_DIM + 1, :]
        o_ref[:, h * V_HEAD_DIM:(h + 1) * V_HEAD_DIM] = out.T.astype(BF16)


def _flash_call(q, k, vt, tq, heads):
    b, s, _ = k.shape
    return pl.pallas_call(
        functools.partial(_flash_kernel, tq=tq, heads=heads),
        grid=(b, N_HEADS // heads, s // tq),
        in_specs=[
            pl.BlockSpec((None, None, heads * HEAD_SLOT, tq), lambda bi, h, i: (bi, i, h, 0)),
            pl.BlockSpec((None, s, heads * HEAD_SLOT), lambda bi, h, i: (bi, 0, h)),
            pl.BlockSpec((None, s // tq, heads * V_HEAD_DIM, tq), lambda bi, h, i: (bi, 0, h, 0)),
        ],
        out_specs=pl.BlockSpec((None, tq, heads * V_HEAD_DIM), lambda bi, h, i: (bi, i, h)),
        out_shape=jax.ShapeDtypeStruct((b, s, N_HEADS * V_HEAD_DIM), BF16),
        scratch_shapes=[
            pltpu.VMEM((heads, 1, tq), F32),
            pltpu.VMEM((heads, V_HEAD_DIM + SUM_ROWS, tq), F32),
            pltpu.VMEM((2, heads, tq, tq), F32),
            pltpu.VMEM((2, heads, 1, tq), F32),
        ],
        compiler_params=_params("parallel", "parallel", "arbitrary"),
        name="flash",
    )(q, k, vt)


def _mix_kernel(x_ref, u_ref, halo_ref, attn_ref, gp_ref, ga_ref, wg_ref, sc_ref,
                wpo_ref, wao_ref, wout_ref, o_ref, *, tm, seq):
    t0 = (pl.program_id(0) * tm) % seq
    pooled = _pool_mixer(u_ref[...], halo_ref[...], t0, wg_ref, sc_ref)
    y_pool = _dot(pooled, wpo_ref[...])
    y_attn = _dot(attn_ref[...], wao_ref[...])
    mixed = gp_ref[...] * y_pool + ga_ref[...] * y_attn
    o_ref[...] = x_ref[...] + _dot(mixed.astype(BF16), wout_ref[...])


def _mix_call(x, u_pool, attn, gates, w_grp, scale, w_po, w_ao, w_out, tm, seq):
    m, d = x.shape
    width = u_pool.shape[1]
    halo_blocks = tm // POOL_HALO
    row = lambda i: (i, 0)
    fixed = lambda i: (0, 0)
    return pl.pallas_call(
        functools.partial(_mix_kernel, tm=tm, seq=seq),
        grid=(m // tm,),
        in_specs=[
            pl.BlockSpec((tm, d), row),
            pl.BlockSpec((tm, width), row),
            pl.BlockSpec((POOL_HALO, width), lambda i: (jnp.maximum(i * halo_blocks - 1, 0), 0)),
            pl.BlockSpec((tm, attn.shape[1]), row),
            pl.BlockSpec((tm, d), lambda i: (i, 0)),
            pl.BlockSpec((tm, d), lambda i: (i, 1)),
            pl.BlockSpec(w_grp.shape, lambda i: (0, 0, 0)),
            pl.BlockSpec((1, width), fixed),
            pl.BlockSpec(w_po.shape, fixed),
            pl.BlockSpec(w_ao.shape, fixed),
            pl.BlockSpec(w_out.shape, fixed),
        ],
        out_specs=pl.BlockSpec((tm, d), row),
        out_shape=jax.ShapeDtypeStruct((m, d), F32),
        compiler_params=_params("parallel"),
        name="mix",
    )(x, u_pool, u_pool, attn, gates, gates, w_grp, scale, w_po, w_ao, w_out)


def _ffn_kernel(x_ref, g_ref, wg_ref, wu_ref, wd_ref, o_ref, h_ref):
    @pl.when(pl.program_id(1) == 0)
    def _():
        x = x_ref[...]
        h_ref[...] = _rms_bf16(x, g_ref[...])
        o_ref[...] = x

    h = h_ref[...]
    a = _dot(h, wg_ref[...].astype(BF16))
    u = _dot(h, wu_ref[...].astype(BF16))
    act = a * jax.nn.sigmoid(a) * u
    o_ref[...] += _dot(act.astype(BF16), wd_ref[...].astype(BF16))


def _ffn_call(x, g, w_gate, w_up, w_down, tm, tf):
    m, d = x.shape
    f = w_gate.shape[1]
    return pl.pallas_call(
        _ffn_kernel,
        grid=(m // tm, f // tf),
        in_specs=[
            pl.BlockSpec((tm, d), lambda i, j: (i, 0)),
            pl.BlockSpec((1, d), lambda i, j: (0, 0)),
            pl.BlockSpec((d, tf), lambda i, j: (0, j)),
            pl.BlockSpec((d, tf), lambda i, j: (0, j)),
            pl.BlockSpec((tf, d), lambda i, j: (j, 0)),
        ],
        out_specs=pl.BlockSpec((tm, d), lambda i, j: (i, 0)),
        out_shape=jax.ShapeDtypeStruct((m, d), F32),
        scratch_shapes=[pltpu.VMEM((tm, d), BF16)],
        compiler_params=_params("parallel", "arbitrary"),
        name="ffn",
    )(x, g, w_gate, w_up, w_down)


def _rope_slot(t):
    z = jnp.zeros(t.shape[:-1] + (ROPE_HALF,), t.dtype)
    return jnp.concatenate([t[..., :ROPE_HALF], z, t[..., ROPE_HALF:], z], axis=-1)


def _head_slot(t):
    return jnp.concatenate([t[..., :QK_NOPE_DIM], _rope_slot(t[..., QK_NOPE_DIM:])], axis=-1)


def kernel(x, positions, attn_norm_g, w_in, b_gate, q_a_norm_g, w_q_b, kv_a_norm_g, w_kv_b,
           q_norm_g, k_norm_g, w_attn_o, w_pool_grp, pool_scale, w_pool_o, w_out,
           ffn_norm_g, w_ffn_gate, w_ffn_up, w_ffn_down):
    b, s, d = x.shape
    depth = w_in.shape[0]
    m = b * s
    pool_width = w_pool_o.shape[1]

    inv_freq = ROPE_THETA ** (-jnp.arange(ROPE_HALF, dtype=F32) / ROPE_HALF)
    ang_t = positions.astype(F32).reshape(1, m) * inv_freq.reshape(ROPE_HALF, 1)
    cos_t = jnp.cos(ang_t)
    sin_t = jnp.sin(ang_t)

    xf = x.reshape(m, d)
    for l in range(depth):
        g_lat = jnp.concatenate([q_a_norm_g[l], kv_a_norm_g[l]]).reshape(1, -1)
        wqt = w_q_b[l].T.astype(BF16)
        wkv = w_kv_b[l].reshape(KV_LORA_RANK, N_HEADS, QK_NOPE_DIM + V_HEAD_DIM)
        wk = wkv[..., :QK_NOPE_DIM].reshape(KV_LORA_RANK, -1).astype(BF16)
        wvt = wkv[..., QK_NOPE_DIM:].reshape(KV_LORA_RANK, -1).T.astype(BF16)
        tq = 512
        gq = jnp.broadcast_to((q_norm_g[l] * (QK_HEAD_DIM ** -0.5 * LOG2_E)).reshape(-1, 1),
                              (QK_HEAD_DIM, tq))
        gk = _head_slot(k_norm_g[l]).reshape(1, HEAD_SLOT)
        g_attn_norm = attn_norm_g[l].reshape(1, d)

        lat, u_pool, gates, kr = _inproj_call(xf, g_attn_norm, w_in[l].T, g_lat,
                                              b_gate[l].reshape(1, -1), pool_width, tm=1024)
        qt, k, vt = _qkv_call(lat, kr, cos_t, sin_t, wqt, wk, wvt, gq, gk, tm=tq)
        attn = _flash_call(qt.reshape(b, s // tq, -1, tq), k.reshape(b, s, -1),
                           vt.reshape(b, s // tq, -1, tq), tq=tq, heads=4)
        xf = _mix_call(xf, u_pool, attn.reshape(m, -1), gates, w_pool_grp[l].astype(BF16),
                       pool_scale[l].reshape(1, -1), w_pool_o[l].astype(BF16),
                       w_attn_o[l].astype(BF16), w_out[l].astype(BF16), tm=256, seq=s)
        xf = _ffn_call(xf, ffn_norm_g[l].reshape(1, d), w_ffn_gate[l], w_ffn_up[l], w_ffn_down[l],
                       tm=1024, tf=256)
    return xf.reshape(b, s, d)
```

```python
import functools

import jax
import jax.numpy as jnp
from jax import lax
from jax.experimental import pallas as pl
from jax.experimental.pallas import tpu as pltpu

F32 = jnp.float32
BF16 = jnp.bfloat16

N_HEADS = 16
QK_NOPE_DIM = 128
QK_ROPE_DIM = 64
QK_HEAD_DIM = QK_NOPE_DIM + QK_ROPE_DIM
V_HEAD_DIM = 128
Q_LORA_RANK = 512
KV_LORA_RANK = 512
ROPE_THETA = 10000.0
POOL_WINDOWS = (2, 4, 8, 16)
POOL_GROUPS = 4
EPS = 1e-6

LANES = 128
SUBLANES = 8
HEAD_SLOT = 2 * LANES
ROPE_HALF = QK_ROPE_DIM // 2
POOL_HALO = 16
VMEM_LIMIT = 56 * 1024 * 1024
FFN_VMEM_LIMIT = 62 * 1024 * 1024
FFN_RESIDUAL_STEPS = 8
SUM_ROWS = 16
LOG2_E = 1.4426950408889634
_CONTRACT_LAST = (((1,), (1,)), ((), ()))


def _params(*sem):
    return pltpu.CompilerParams(dimension_semantics=sem, vmem_limit_bytes=VMEM_LIMIT)


def _dot(a, b):
    return jnp.dot(a, b, preferred_element_type=F32)


def _rms_bf16(x, g):
    ms = jnp.mean(x * x, axis=-1, keepdims=True)
    return (x * lax.rsqrt(ms + EPS) * g).astype(BF16)


def _rope(t, cos, sin):
    return t * cos + pltpu.roll(t, 2 * ROPE_HALF, 1) * sin


def _dot_t(a, w):
    return lax.dot_general(a, w.astype(BF16), _CONTRACT_LAST, preferred_element_type=F32)


def _inproj_kernel(x_ref, g_ref, w_ref, gl_ref, b_ref, wkr_ref,
                   lat_ref, up_ref, gate_ref, kr_ref, h_ref, *, n_lat, n_up):
    j = pl.program_id(1)

    @pl.when(j == 0)
    def _():
        h_ref[...] = _rms_bf16(x_ref[...], g_ref[...])
        wkr = wkr_ref[...]
        z = jnp.zeros((ROPE_HALF, wkr.shape[1]), F32)
        slot = jnp.concatenate([wkr[:ROPE_HALF], z, wkr[ROPE_HALF:], z], axis=0)
        kr_ref[...] = _dot_t(h_ref[...], slot)

    @pl.when(j < n_lat)
    def _():
        lat_ref[...] = _rms_bf16(_dot_t(h_ref[...], w_ref[...]), gl_ref[...])

    @pl.when(jnp.logical_and(j >= n_lat, j < n_lat + n_up))
    def _():
        up_ref[...] = _dot_t(h_ref[...], w_ref[...])

    @pl.when(j >= n_lat + n_up)
    def _():
        gate_ref[...] = jax.nn.sigmoid(_dot_t(h_ref[...], w_ref[...]) + b_ref[...])


def _inproj_call(x, g, w_t, g_lat, b_gate, pool_width, tm):
    m, d = x.shape
    tn = Q_LORA_RANK
    n_lat = g_lat.shape[1] // tn
    n_up = pool_width // tn
    n_gate = b_gate.shape[1] // tn
    o_kr = n_lat * tn
    o_up = o_kr + QK_ROPE_DIM
    clamp = lambda j, lo, n: jnp.clip(j - lo, 0, n - 1)
    sub = SUBLANES
    w_row = lambda j: sub * jnp.where(j < n_lat, j * (tn // sub),
                                      o_up // sub + (j - n_lat) * (tn // sub))
    return pl.pallas_call(
        functools.partial(_inproj_kernel, n_lat=n_lat, n_up=n_up),
        grid=(m // tm, n_lat + n_up + n_gate),
        in_specs=[
            pl.BlockSpec((tm, d), lambda i, j: (i, 0)),
            pl.BlockSpec((1, d), lambda i, j: (0, 0)),
            pl.BlockSpec((pl.Element(tn), pl.Element(d)), lambda i, j: (w_row(j), 0)),
            pl.BlockSpec((1, tn), lambda i, j: (0, clamp(j, 0, n_lat))),
            pl.BlockSpec((1, tn), lambda i, j: (0, clamp(j, n_lat + n_up, n_gate))),
            pl.BlockSpec((pl.Element(QK_ROPE_DIM), pl.Element(d)), lambda i, j: (o_kr, 0)),
        ],
        out_specs=[
            pl.BlockSpec((tm, tn), lambda i, j: (i, clamp(j, 0, n_lat))),
            pl.BlockSpec((tm, tn), lambda i, j: (i, clamp(j, n_lat, n_up))),
            pl.BlockSpec((tm, tn), lambda i, j: (i, clamp(j, n_lat + n_up, n_gate))),
            pl.BlockSpec((tm, LANES), lambda i, j: (i, 0)),
        ],
        out_shape=[
            jax.ShapeDtypeStruct((m, n_lat * tn), BF16),
            jax.ShapeDtypeStruct((m, n_up * tn), F32),
            jax.ShapeDtypeStruct((m, n_gate * tn), F32),
            jax.ShapeDtypeStruct((m, LANES), F32),
        ],
        scratch_shapes=[pltpu.VMEM((tm, d), BF16)],
        compiler_params=_params("parallel", "arbitrary"),
        name="inproj",
    )(x, g, w_t, g_lat, b_gate, w_t)


def _pool_mixer(u, halo, t0, wg_ref, sc_ref):
    tm = u.shape[0]
    halo = jnp.where(t0 > 0, halo, 0.0)
    ext = jnp.concatenate([halo, u], axis=0)
    pos = lax.broadcasted_iota(jnp.int32, (tm, 1), 0) + t0
    gd = wg_ref.shape[1]
    out = []
    for g, w in enumerate(POOL_WINDOWS):
        cols = slice(g * gd, (g + 1) * gd)
        a = ext[:, cols]
        shift = 1
        while shift < w:
            a = a + pltpu.roll(a, shift, 0)
            shift *= 2
        count = jnp.minimum(pos + 1, w).astype(F32)
        pooled = a[POOL_HALO:, :] / count - u[:, cols]
        y = _dot(pooled.astype(BF16), wg_ref[g]) * sc_ref[:, cols]
        out.append(y.astype(BF16))
    return jnp.concatenate(out, axis=1)


def _qkv_kernel(cq_ref, ckv_ref, kr_ref, cos_ref, sin_ref, wqt_ref, wk_ref, wvt_ref,
                gq_ref, gk_ref, qt_ref, k_ref, vt_ref):
    inv_dim = 1.0 / QK_HEAD_DIM
    half = ROPE_HALF
    cq = cq_ref[...]
    ckv = ckv_ref[...]
    cos_t = cos_ref[...]
    sin_t = sin_ref[...]
    qt = lax.dot_general(wqt_ref[...], cq, _CONTRACT_LAST, preferred_element_type=F32)
    vt_ref[...] = lax.dot_general(wvt_ref[...], ckv, _CONTRACT_LAST,
                                  preferred_element_type=F32).astype(BF16)
    gq = gq_ref[...]
    zeros = jnp.zeros((half, qt.shape[1]), BF16)
    for h in range(N_HEADS):
        src = h * QK_HEAD_DIM
        dst = h * HEAD_SLOT
        qh = qt[src:src + QK_HEAD_DIM]
        inv = lax.rsqrt(jnp.sum(qh * qh, axis=0, keepdims=True) * inv_dim + EPS)
        qh = qh * inv * gq
        t1 = qh[QK_NOPE_DIM:QK_NOPE_DIM + half]
        t2 = qh[QK_NOPE_DIM + half:]
        qt_ref[dst:dst + QK_NOPE_DIM] = qh[:QK_NOPE_DIM].astype(BF16)
        qt_ref[dst + LANES:dst + LANES + half] = (t1 * cos_t - t2 * sin_t).astype(BF16)
        qt_ref[dst + LANES + half:dst + LANES + 2 * half] = zeros
        qt_ref[dst + LANES + 2 * half:dst + LANES + 3 * half] = (t2 * cos_t + t1 * sin_t).astype(BF16)
        qt_ref[dst + LANES + 3 * half:dst + HEAD_SLOT] = zeros

    cos_r = cos_t.T
    sin_r = sin_t.T
    cos = jnp.concatenate([cos_r] * 4, axis=1)
    sin = jnp.concatenate([-sin_r, -sin_r, sin_r, sin_r], axis=1)
    kv = _dot(ckv, wk_ref[...])
    gk_n, gk_r = gk_ref[:, :LANES], gk_ref[:, LANES:]
    kr = kr_ref[...]
    ss_kr = jnp.sum(kr * kr, axis=-1, keepdims=True)
    kr_rot = _rope(kr * gk_r, cos, sin)
    for h in range(N_HEADS):
        lo = h * HEAD_SLOT
        kn = kv[:, h * LANES:(h + 1) * LANES]
        ssk = jnp.sum(kn * kn, axis=-1, keepdims=True) + ss_kr
        invk = lax.rsqrt(ssk * inv_dim + EPS)
        k_ref[:, lo:lo + LANES] = (kn * invk * gk_n).astype(BF16)
        k_ref[:, lo + LANES:lo + HEAD_SLOT] = (kr_rot * invk).astype(BF16)


def _qkv_call(lat, kr, cos_t, sin_t, wqt, wk, wvt, gq, gk, tm):
    m = lat.shape[0]
    rank = Q_LORA_RANK
    wide = N_HEADS * HEAD_SLOT
    vw = N_HEADS * V_HEAD_DIM
    row = lambda i: (i, 0)
    col = lambda i: (0, i)
    fixed = lambda i: (0, 0)
    return pl.pallas_call(
        _qkv_kernel,
        grid=(m // tm,),
        in_specs=[
            pl.BlockSpec((tm, rank), lambda i: (i, 0)),
            pl.BlockSpec((tm, rank), lambda i: (i, 1)),
            pl.BlockSpec((tm, LANES), row),
            pl.BlockSpec((ROPE_HALF, tm), col),
            pl.BlockSpec((ROPE_HALF, tm), col),
            pl.BlockSpec(wqt.shape, fixed),
            pl.BlockSpec(wk.shape, fixed),
            pl.BlockSpec(wvt.shape, fixed),
            pl.BlockSpec(gq.shape, fixed),
            pl.BlockSpec((1, HEAD_SLOT), fixed),
        ],
        out_specs=[
            pl.BlockSpec((None, wide, tm), lambda i: (i, 0, 0)),
            pl.BlockSpec((tm, wide), row),
            pl.BlockSpec((None, vw, tm), lambda i: (i, 0, 0)),
        ],
        out_shape=[
            jax.ShapeDtypeStruct((m // tm, wide, tm), BF16),
            jax.ShapeDtypeStruct((m, wide), BF16),
            jax.ShapeDtypeStruct((m // tm, vw, tm), BF16),
        ],
        compiler_params=_params("parallel"),
        name="qkv",
    )(lat, lat, kr, cos_t, sin_t, wqt, wk, wvt, gq, gk)


def _flash_kernel(q_ref, k_ref, vt_ref, o_ref, m_ref, acc_ref, s_ref, mx_ref, *, tq, heads):
    n = pl.program_id(2)
    qs = [q_ref[h * HEAD_SLOT:(h + 1) * HEAD_SLOT, :] for h in range(heads)]
    ones = jnp.ones((SUM_ROWS, tq), BF16)
    m_ref[...] = jnp.full(m_ref.shape, -jnp.inf, F32)
    acc_ref[...] = jnp.zeros(acc_ref.shape, F32)

    half = tq // 2
    head_cols = lambda h: slice(h * HEAD_SLOT, (h + 1) * HEAD_SLOT)

    def v_rows(h, c):
        vt = vt_ref[c, h * V_HEAD_DIM:(h + 1) * V_HEAD_DIM, :]
        return jnp.concatenate([vt, ones], axis=0)

    def new_max(h, buf):
        m_prev = m_ref[h]
        m_new = jnp.maximum(m_prev, mx_ref[buf, h])
        m_ref[h] = m_new
        return m_new, jnp.exp2(m_prev - m_new)

    def scores(h, c, buf):
        start = pl.multiple_of(c * tq, tq)
        s = _dot(k_ref[pl.ds(start, tq), head_cols(h)], qs[h])
        s_ref[buf, h] = s
        mx_ref[buf, h] = jnp.max(s, axis=0, keepdims=True)

    def accumulate(h, c, buf):
        m_new, alpha = new_max(h, buf)
        p = jnp.exp2(s_ref[buf, h] - m_new)
        acc_ref[h] = alpha * acc_ref[h] + _dot(v_rows(h, c), p.astype(BF16))

    def scores_diagonal(h, c, buf):
        start = pl.multiple_of(c * tq, tq)
        causal = (lax.broadcasted_iota(jnp.int32, (half, tq), 0)
                  <= lax.broadcasted_iota(jnp.int32, (half, tq), 1))
        s0 = _dot(k_ref[pl.ds(start, half), head_cols(h)], qs[h])
        s1 = _dot(k_ref[pl.ds(start + half, half), head_cols(h)], qs[h][:, half:])
        s0 = jnp.where(causal, s0, -jnp.inf)
        s1 = jnp.where(causal[:, :half], s1, -jnp.inf)
        s_ref[buf, h, :half, :] = s0
        s_ref[buf, h, half:, half:] = s1
        m0 = jnp.max(s0, axis=0, keepdims=True)
        m1 = jnp.max(s1, axis=0, keepdims=True)
        mx_ref[buf, h] = jnp.concatenate([m0[:, :half], jnp.maximum(m0[:, half:], m1)], axis=1)

    def accumulate_diagonal(h, c, buf):
        m_new, alpha = new_max(h, buf)
        p0 = jnp.exp2(s_ref[buf, h, :half, :] - m_new).astype(BF16)
        p1 = jnp.exp2(s_ref[buf, h, half:, half:] - m_new[:, half:]).astype(BF16)
        vt = v_rows(h, c)
        acc = alpha * acc_ref[h] + _dot(vt[:, :half], p0)
        acc_ref[h, :, :half] = acc[:, :half]
        acc_ref[h, :, half:] = acc[:, half:] + _dot(vt[:, half:], p1)

    def stage(c, buf, next_diagonal):
        for h in range(heads):
            (scores_diagonal if next_diagonal else scores)(h, c + 1, 1 - buf)
            accumulate(h, c, buf)

    def drain(c, buf):
        for h in range(heads):
            accumulate_diagonal(h, c, buf)

    @pl.when(n == 0)
    def _():
        for h in range(heads):
            scores_diagonal(h, 0, 0)
        drain(0, 0)

    @pl.when(n > 0)
    def _():
        for h in range(heads):
            scores(h, 0, 0)

    def two_stages(i, carry):
        stage(2 * i, 0, False)
        stage(2 * i + 1, 1, False)
        return carry

    lax.fori_loop(0, lax.div(n - 1, 2), two_stages, None)

    @pl.when(n % 2 == 1)
    def _():
        stage(n - 1, 0, True)
        drain(n, 1)

    @pl.when(jnp.logical_and(n >= 2, n % 2 == 0))
    def _():
        stage(n - 2, 0, False)
        stage(n - 1, 1, True)
        drain(n, 0)

    for h in range(heads):
        out = acc_ref[h, :V_HEAD_DIM, :] / acc_ref[h, V_HEAD_DIM:V_HEAD_DIM + 1, :]
        o_ref[:, h * V_HEAD_DIM:(h + 1) * V_HEAD_DIM] = out.T.astype(BF16)


def _flash_call(q, k, vt, tq, heads):
    b, s, _ = k.shape
    return pl.pallas_call(
        functools.partial(_flash_kernel, tq=tq, heads=heads),
        grid=(b, N_HEADS // heads, s // tq),
        in_specs=[
            pl.BlockSpec((None, None, heads * HEAD_SLOT, tq), lambda bi, h, i: (bi, i, h, 0)),
            pl.BlockSpec((None, s, heads * HEAD_SLOT), lambda bi, h, i: (bi, 0, h)),
            pl.BlockSpec((None, s // tq, heads * V_HEAD_DIM, tq), lambda bi, h, i: (bi, 0, h, 0)),
        ],
        out_specs=pl.BlockSpec((None, tq, heads * V_HEAD_DIM), lambda bi, h, i: (bi, i, h)),
        out_shape=jax.ShapeDtypeStruct((b, s, N_HEADS * V_HEAD_DIM), BF16),
        scratch_shapes=[
            pltpu.VMEM((heads, 1, tq), F32),
            pltpu.VMEM((heads, V_HEAD_DIM + SUM_ROWS, tq), F32),
            pltpu.VMEM((2, heads, tq, tq), F32),
            pltpu.VMEM((2, heads, 1, tq), F32),
        ],
        compiler_params=_params("parallel", "parallel", "arbitrary"),
        name="flash",
    )(q, k, vt)


def _mix_kernel(x_ref, u_ref, halo_ref, attn_ref, gp_ref, ga_ref, wg_ref, sc_ref,
                wpo_ref, wao_ref, wout_ref, g2_ref, o_ref, h2_ref, *, tm, seq):
    t0 = (pl.program_id(0) * tm) % seq
    pooled = _pool_mixer(u_ref[...], halo_ref[...], t0, wg_ref, sc_ref)
    y_pool = _dot(pooled, wpo_ref[...])
    y_attn = _dot(attn_ref[...], wao_ref[...])
    mixed = gp_ref[...] * y_pool + ga_ref[...] * y_attn
    x1 = x_ref[...] + _dot(mixed.astype(BF16), wout_ref[...])
    o_ref[...] = x1
    h2_ref[...] = _rms_bf16(x1, g2_ref[...])


def _mix_call(x, u_pool, attn, gates, w_grp, scale, w_po, w_ao, w_out, g_ffn, tm, seq):
    m, d = x.shape
    width = u_pool.shape[1]
    halo_blocks = tm // POOL_HALO
    row = lambda i: (i, 0)
    fixed = lambda i: (0, 0)
    return pl.pallas_call(
        functools.partial(_mix_kernel, tm=tm, seq=seq),
        grid=(m // tm,),
        in_specs=[
            pl.BlockSpec((tm, d), row),
            pl.BlockSpec((tm, width), row),
            pl.BlockSpec((POOL_HALO, width), lambda i: (jnp.maximum(i * halo_blocks - 1, 0), 0)),
            pl.BlockSpec((tm, attn.shape[1]), row),
            pl.BlockSpec((tm, d), lambda i: (i, 0)),
            pl.BlockSpec((tm, d), lambda i: (i, 1)),
            pl.BlockSpec(w_grp.shape, lambda i: (0, 0, 0)),
            pl.BlockSpec((1, width), fixed),
            pl.BlockSpec(w_po.shape, fixed),
            pl.BlockSpec(w_ao.shape, fixed),
            pl.BlockSpec(w_out.shape, fixed),
            pl.BlockSpec((1, d), fixed),
        ],
        out_specs=[pl.BlockSpec((tm, d), row), pl.BlockSpec((tm, d), row)],
        out_shape=[jax.ShapeDtypeStruct((m, d), F32), jax.ShapeDtypeStruct((m, d), BF16)],
        compiler_params=_params("parallel"),
        name="mix",
    )(x, u_pool, u_pool, attn, gates, gates, w_grp, scale, w_po, w_ao, w_out, g_ffn)


def _ffn_kernel(h_ref, x_ref, wg_ref, wu_ref, wd_ref, o_ref, *, res_steps):
    j = pl.program_id(1)

    def step(first):
        h = h_ref[...]
        a = _dot(h, wg_ref[...].astype(BF16))
        u = _dot(h, wu_ref[...].astype(BF16))
        act = a * jax.nn.sigmoid(a) * u
        down = _dot(act.astype(BF16), wd_ref[...].astype(BF16))
        if first:
            o_ref[...] = down
        else:
            o_ref[...] += down

    pl.when(j == 0)(lambda: step(True))
    pl.when(j > 0)(lambda: step(False))

    slab = x_ref.shape[1]
    for c in range(res_steps):
        @pl.when(j == c)
        def _():
            o_ref[:, c * slab:(c + 1) * slab] += x_ref[...]


def _ffn_call(h, x, w_gate, w_up, w_down, tm, tf):
    m, d = x.shape
    f = w_gate.shape[1]
    res_steps = FFN_RESIDUAL_STEPS
    assert f // tf >= res_steps
    return pl.pallas_call(
        functools.partial(_ffn_kernel, res_steps=res_steps),
        grid=(m // tm, f // tf),
        in_specs=[
            pl.BlockSpec((tm, d), lambda i, j: (i, 0)),
            pl.BlockSpec((tm, d // res_steps), lambda i, j: (i, jnp.minimum(j, res_steps - 1))),
            pl.BlockSpec((d, tf), lambda i, j: (0, j)),
            pl.BlockSpec((d, tf), lambda i, j: (0, j)),
            pl.BlockSpec((tf, d), lambda i, j: (j, 0)),
        ],
        out_specs=pl.BlockSpec((tm, d), lambda i, j: (i, 0)),
        out_shape=jax.ShapeDtypeStruct((m, d), F32),
        compiler_params=pltpu.CompilerParams(dimension_semantics=("parallel", "arbitrary"),
                                             vmem_limit_bytes=FFN_VMEM_LIMIT),
        name="ffn",
    )(h, x, w_gate, w_up, w_down)


def _rope_slot(t):
    z = jnp.zeros(t.shape[:-1] + (ROPE_HALF,), t.dtype)
    return jnp.concatenate([t[..., :ROPE_HALF], z, t[..., ROPE_HALF:], z], axis=-1)


def _head_slot(t):
    return jnp.concatenate([t[..., :QK_NOPE_DIM], _rope_slot(t[..., QK_NOPE_DIM:])], axis=-1)


def kernel(x, positions, attn_norm_g, w_in, b_gate, q_a_norm_g, w_q_b, kv_a_norm_g, w_kv_b,
           q_norm_g, k_norm_g, w_attn_o, w_pool_grp, pool_scale, w_pool_o, w_out,
           ffn_norm_g, w_ffn_gate, w_ffn_up, w_ffn_down):
    b, s, d = x.shape
    depth = w_in.shape[0]
    m = b * s
    pool_width = w_pool_o.shape[1]

    inv_freq = ROPE_THETA ** (-jnp.arange(ROPE_HALF, dtype=F32) / ROPE_HALF)
    ang_t = positions.astype(F32).reshape(1, m) * inv_freq.reshape(ROPE_HALF, 1)
    cos_t = jnp.cos(ang_t)
    sin_t = jnp.sin(ang_t)

    xf = x.reshape(m, d)
    for l in range(depth):
        g_lat = jnp.concatenate([q_a_norm_g[l], kv_a_norm_g[l]]).reshape(1, -1)
        wqt = w_q_b[l].T.astype(BF16)
        wkv = w_kv_b[l].reshape(KV_LORA_RANK, N_HEADS, QK_NOPE_DIM + V_HEAD_DIM)
        wk = wkv[..., :QK_NOPE_DIM].reshape(KV_LORA_RANK, -1).astype(BF16)
        wvt = wkv[..., QK_NOPE_DIM:].reshape(KV_LORA_RANK, -1).T.astype(BF16)
        tq = 512
        gq = jnp.broadcast_to((q_norm_g[l] * (QK_HEAD_DIM ** -0.5 * LOG2_E)).reshape(-1, 1),
                              (QK_HEAD_DIM, tq))
        gk = _head_slot(k_norm_g[l]).reshape(1, HEAD_SLOT)
        g_attn_norm = attn_norm_g[l].reshape(1, d)

        lat, u_pool, gates, kr = _inproj_call(xf, g_attn_norm, w_in[l].T, g_lat,
                                              b_gate[l].reshape(1, -1), pool_width, tm=1024)
        qt, k, vt = _qkv_call(lat, kr, cos_t, sin_t, wqt, wk, wvt, gq, gk, tm=tq)
        attn = _flash_call(qt.reshape(b, s // tq, -1, tq), k.reshape(b, s, -1),
                           vt.reshape(b, s // tq, -1, tq), tq=tq, heads=4)
        xf, h2 = _mix_call(xf, u_pool, attn.reshape(m, -1), gates, w_pool_grp[l].astype(BF16),
                           pool_scale[l].reshape(1, -1), w_pool_o[l].astype(BF16),
                           w_attn_o[l].astype(BF16), w_out[l].astype(BF16),
                           ffn_norm_g[l].reshape(1, d), tm=256, seq=s)
        xf = _ffn_call(h2, xf, w_ffn_gate[l], w_ffn_up[l], w_ffn_down[l], tm=1024, tf=512)
    return xf.reshape(b, s, d)
```

```python
import functools

import jax
import jax.numpy as jnp
from jax import lax
from jax.experimental import pallas as pl
from jax.experimental.pallas import tpu as pltpu

F32 = jnp.float32
BF16 = jnp.bfloat16

N_HEADS = 16
QK_NOPE_DIM = 128
QK_ROPE_DIM = 64
QK_HEAD_DIM = QK_NOPE_DIM + QK_ROPE_DIM
V_HEAD_DIM = 128
Q_LORA_RANK = 512
KV_LORA_RANK = 512
ROPE_THETA = 10000.0
POOL_WINDOWS = (2, 4, 8, 16)
POOL_GROUPS = 4
EPS = 1e-6

LANES = 128
SUBLANES = 8
HEAD_SLOT = 2 * LANES
ROPE_HALF = QK_ROPE_DIM // 2
POOL_HALO = 16
VMEM_LIMIT = 56 * 1024 * 1024
FFN_VMEM_LIMIT = 62 * 1024 * 1024
FFN_RESIDUAL_STEPS = 8

INPROJ_ROWS = 1024
ATTN_TILE = 512
ATTN_HEADS_PER_STEP = 4
MIX_ROWS = 256
FFN_ROWS, FFN_COLS = 1024, 512
SUM_ROWS = 16
LOG2_E = 1.4426950408889634
_CONTRACT_LAST = (((1,), (1,)), ((), ()))


def _params(*sem):
    return pltpu.CompilerParams(dimension_semantics=sem, vmem_limit_bytes=VMEM_LIMIT)


def _dot(a, b):
    return jnp.dot(a, b, preferred_element_type=F32)


def _rms_bf16(x, g):
    ms = jnp.mean(x * x, axis=-1, keepdims=True)
    return (x * lax.rsqrt(ms + EPS) * g).astype(BF16)


def _rope(t, cos, sin):
    return t * cos + pltpu.roll(t, 2 * ROPE_HALF, 1) * sin


def _dot_t(a, w):
    return lax.dot_general(a, w.astype(BF16), _CONTRACT_LAST, preferred_element_type=F32)


def _inproj_kernel(x_ref, g_ref, w_ref, gl_ref, b_ref, wkr_ref,
                   lat_ref, up_ref, gate_ref, kr_ref, h_ref, *, n_lat, n_up):
    j = pl.program_id(1)

    @pl.when(j == 0)
    def _():
        h_ref[...] = _rms_bf16(x_ref[...], g_ref[...])
        wkr = wkr_ref[...]
        z = jnp.zeros((ROPE_HALF, wkr.shape[1]), F32)
        slot = jnp.concatenate([wkr[:ROPE_HALF], z, wkr[ROPE_HALF:], z], axis=0)
        kr_ref[...] = _dot_t(h_ref[...], slot)

    @pl.when(j < n_lat)
    def _():
        lat_ref[...] = _rms_bf16(_dot_t(h_ref[...], w_ref[...]), gl_ref[...])

    @pl.when(jnp.logical_and(j >= n_lat, j < n_lat + n_up))
    def _():
        up_ref[...] = _dot_t(h_ref[...], w_ref[...])

    @pl.when(j >= n_lat + n_up)
    def _():
        gate_ref[...] = jax.nn.sigmoid(_dot_t(h_ref[...], w_ref[...]) + b_ref[...])


def _inproj_call(x, g, w_t, g_lat, b_gate, pool_width, tm):
    m, d = x.shape
    tn = Q_LORA_RANK
    n_lat = g_lat.shape[1] // tn
    n_up = pool_width // tn
    n_gate = b_gate.shape[1] // tn
    o_kr = n_lat * tn
    o_up = o_kr + QK_ROPE_DIM
    clamp = lambda j, lo, n: jnp.clip(j - lo, 0, n - 1)
    sub = SUBLANES
    w_row = lambda j: sub * jnp.where(j < n_lat, j * (tn // sub),
                                      o_up // sub + (j - n_lat) * (tn // sub))
    return pl.pallas_call(
        functools.partial(_inproj_kernel, n_lat=n_lat, n_up=n_up),
        grid=(m // tm, n_lat + n_up + n_gate),
        in_specs=[
            pl.BlockSpec((tm, d), lambda i, j: (i, 0)),
            pl.BlockSpec((1, d), lambda i, j: (0, 0)),
            pl.BlockSpec((pl.Element(tn), pl.Element(d)), lambda i, j: (w_row(j), 0)),
            pl.BlockSpec((1, tn), lambda i, j: (0, clamp(j, 0, n_lat))),
            pl.BlockSpec((1, tn), lambda i, j: (0, clamp(j, n_lat + n_up, n_gate))),
            pl.BlockSpec((pl.Element(QK_ROPE_DIM), pl.Element(d)), lambda i, j: (o_kr, 0)),
        ],
        out_specs=[
            pl.BlockSpec((tm, tn), lambda i, j: (i, clamp(j, 0, n_lat))),
            pl.BlockSpec((tm, tn), lambda i, j: (i, clamp(j, n_lat, n_up))),
            pl.BlockSpec((tm, tn), lambda i, j: (i, clamp(j, n_lat + n_up, n_gate))),
            pl.BlockSpec((tm, LANES), lambda i, j: (i, 0)),
        ],
        out_shape=[
            jax.ShapeDtypeStruct((m, n_lat * tn), BF16),
            jax.ShapeDtypeStruct((m, n_up * tn), F32),
            jax.ShapeDtypeStruct((m, n_gate * tn), F32),
            jax.ShapeDtypeStruct((m, LANES), F32),
        ],
        scratch_shapes=[pltpu.VMEM((tm, d), BF16)],
        compiler_params=_params("parallel", "arbitrary"),
        name="inproj",
    )(x, g, w_t, g_lat, b_gate, w_t)


def _pool_mixer(u, halo, t0, wg_ref, sc_ref):
    tm = u.shape[0]
    halo = jnp.where(t0 > 0, halo, 0.0)
    ext = jnp.concatenate([halo, u], axis=0)
    pos = lax.broadcasted_iota(jnp.int32, (tm, 1), 0) + t0
    gd = wg_ref.shape[1]
    out = []
    for g, w in enumerate(POOL_WINDOWS):
        cols = slice(g * gd, (g + 1) * gd)
        a = ext[:, cols]
        shift = 1
        while shift < w:
            a = a + pltpu.roll(a, shift, 0)
            shift *= 2
        count = jnp.minimum(pos + 1, w).astype(F32)
        pooled = a[POOL_HALO:, :] / count - u[:, cols]
        y = _dot(pooled.astype(BF16), wg_ref[g]) * sc_ref[:, cols]
        out.append(y.astype(BF16))
    return jnp.concatenate(out, axis=1)


def _qkv_kernel(cq_ref, ckv_ref, kr_ref, cos_ref, sin_ref, wqt_ref, wk_ref, wvt_ref,
                gq_ref, gk_ref, qt_ref, k_ref, vt_ref):
    inv_dim = 1.0 / QK_HEAD_DIM
    half = ROPE_HALF
    cq = cq_ref[...]
    ckv = ckv_ref[...]
    cos_t = cos_ref[...]
    sin_t = sin_ref[...]
    qt = lax.dot_general(wqt_ref[...], cq, _CONTRACT_LAST, preferred_element_type=F32)
    vt_ref[...] = lax.dot_general(wvt_ref[...], ckv, _CONTRACT_LAST,
                                  preferred_element_type=F32).astype(BF16)
    gq = gq_ref[...]
    zeros = jnp.zeros((half, qt.shape[1]), BF16)
    for h in range(N_HEADS):
        src = h * QK_HEAD_DIM
        dst = h * HEAD_SLOT
        qh = qt[src:src + QK_HEAD_DIM]
        inv = lax.rsqrt(jnp.sum(qh * qh, axis=0, keepdims=True) * inv_dim + EPS)
        qh = qh * inv * gq
        t1 = qh[QK_NOPE_DIM:QK_NOPE_DIM + half]
        t2 = qh[QK_NOPE_DIM + half:]
        qt_ref[dst:dst + QK_NOPE_DIM] = qh[:QK_NOPE_DIM].astype(BF16)
        qt_ref[dst + LANES:dst + LANES + half] = (t1 * cos_t - t2 * sin_t).astype(BF16)
        qt_ref[dst + LANES + half:dst + LANES + 2 * half] = zeros
        qt_ref[dst + LANES + 2 * half:dst + LANES + 3 * half] = (t2 * cos_t + t1 * sin_t).astype(BF16)
        qt_ref[dst + LANES + 3 * half:dst + HEAD_SLOT] = zeros

    cos_r = cos_t.T
    sin_r = sin_t.T
    cos = jnp.concatenate([cos_r] * 4, axis=1)
    sin = jnp.concatenate([-sin_r, -sin_r, sin_r, sin_r], axis=1)
    kv = _dot(ckv, wk_ref[...])
    gk_n, gk_r = gk_ref[:, :LANES], gk_ref[:, LANES:]
    kr = kr_ref[...]
    ss_kr = jnp.sum(kr * kr, axis=-1, keepdims=True)
    kr_rot = _rope(kr * gk_r, cos, sin)
    for h in range(N_HEADS):
        lo = h * HEAD_SLOT
        kn = kv[:, h * LANES:(h + 1) * LANES]
        ssk = jnp.sum(kn * kn, axis=-1, keepdims=True) + ss_kr
        invk = lax.rsqrt(ssk * inv_dim + EPS)
        k_ref[:, lo:lo + LANES] = (kn * invk * gk_n).astype(BF16)
        k_ref[:, lo + LANES:lo + HEAD_SLOT] = (kr_rot * invk).astype(BF16)


def _qkv_call(lat, kr, cos_t, sin_t, wqt, wk, wvt, gq, gk, tm):
    m = lat.shape[0]
    rank = Q_LORA_RANK
    wide = N_HEADS * HEAD_SLOT
    vw = N_HEADS * V_HEAD_DIM
    row = lambda i: (i, 0)
    col = lambda i: (0, i)
    fixed = lambda i: (0, 0)
    return pl.pallas_call(
        _qkv_kernel,
        grid=(m // tm,),
        in_specs=[
            pl.BlockSpec((tm, rank), lambda i: (i, 0)),
            pl.BlockSpec((tm, rank), lambda i: (i, 1)),
            pl.BlockSpec((tm, LANES), row),
            pl.BlockSpec((ROPE_HALF, tm), col),
            pl.BlockSpec((ROPE_HALF, tm), col),
            pl.BlockSpec(wqt.shape, fixed),
            pl.BlockSpec(wk.shape, fixed),
            pl.BlockSpec(wvt.shape, fixed),
            pl.BlockSpec(gq.shape, fixed),
            pl.BlockSpec((1, HEAD_SLOT), fixed),
        ],
        out_specs=[
            pl.BlockSpec((None, wide, tm), lambda i: (i, 0, 0)),
            pl.BlockSpec((tm, wide), row),
            pl.BlockSpec((None, vw, tm), lambda i: (i, 0, 0)),
        ],
        out_shape=[
            jax.ShapeDtypeStruct((m // tm, wide, tm), BF16),
            jax.ShapeDtypeStruct((m, wide), BF16),
            jax.ShapeDtypeStruct((m // tm, vw, tm), BF16),
        ],
        compiler_params=_params("parallel"),
        name="qkv",
    )(lat, lat, kr, cos_t, sin_t, wqt, wk, wvt, gq, gk)


def _flash_kernel(q_ref, k_ref, vt_ref, o_ref, m_ref, acc_ref, s_ref, mx_ref, *, tq, heads):
    n = pl.program_id(2)
    ones = jnp.ones((SUM_ROWS, tq), BF16)
    m_ref[...] = jnp.full(m_ref.shape, -jnp.inf, F32)
    acc_ref[...] = jnp.zeros(acc_ref.shape, F32)

    half = tq // 2
    head_cols = lambda h: slice(h * HEAD_SLOT, (h + 1) * HEAD_SLOT)

    def v_rows(h, c):
        vt = vt_ref[c, h * V_HEAD_DIM:(h + 1) * V_HEAD_DIM, :]
        return jnp.concatenate([vt, ones], axis=0)

    def new_max(h, buf):
        m_prev = m_ref[h]
        m_new = jnp.maximum(m_prev, mx_ref[buf, h])
        m_ref[h] = m_new
        return m_new, jnp.exp2(m_prev - m_new)

    def scores(h, c, buf):
        start = pl.multiple_of(c * tq, tq)
        s = _dot(k_ref[pl.ds(start, tq), head_cols(h)], q_ref[head_cols(h), :])
        s_ref[buf, h] = s
        mx_ref[buf, h] = jnp.max(s, axis=0, keepdims=True)

    def accumulate(h, c, buf):
        m_new, alpha = new_max(h, buf)
        p = jnp.exp2(s_ref[buf, h] - m_new)
        acc_ref[h] = alpha * acc_ref[h] + _dot(v_rows(h, c), p.astype(BF16))

    def scores_diagonal(h, c, buf):
        start = pl.multiple_of(c * tq, tq)
        causal = (lax.broadcasted_iota(jnp.int32, (half, tq), 0)
                  <= lax.broadcasted_iota(jnp.int32, (half, tq), 1))
        q_t = q_ref[head_cols(h), :]
        s0 = _dot(k_ref[pl.ds(start, half), head_cols(h)], q_t)
        s1 = _dot(k_ref[pl.ds(start + half, half), head_cols(h)], q_t[:, half:])
        s0 = jnp.where(causal, s0, -jnp.inf)
        s1 = jnp.where(causal[:, :half], s1, -jnp.inf)
        s_ref[buf, h, :half, :] = s0
        s_ref[buf, h, half:, half:] = s1
        m0 = jnp.max(s0, axis=0, keepdims=True)
        m1 = jnp.max(s1, axis=0, keepdims=True)
        mx_ref[buf, h] = jnp.concatenate([m0[:, :half], jnp.maximum(m0[:, half:], m1)], axis=1)

    def accumulate_diagonal(h, c, buf):
        m_new, alpha = new_max(h, buf)
        p0 = jnp.exp2(s_ref[buf, h, :half, :] - m_new).astype(BF16)
        p1 = jnp.exp2(s_ref[buf, h, half:, half:] - m_new[:, half:]).astype(BF16)
        vt = v_rows(h, c)
        acc = alpha * acc_ref[h] + _dot(vt[:, :half], p0)
        acc_ref[h, :, :half] = acc[:, :half]
        acc_ref[h, :, half:] = acc[:, half:] + _dot(vt[:, half:], p1)

    def stage(c, buf, next_diagonal):
        for h in range(heads):
            (scores_diagonal if next_diagonal else scores)(h, c + 1, 1 - buf)
            accumulate(h, c, buf)

    def drain(c, buf):
        for h in range(heads):
            accumulate_diagonal(h, c, buf)

    @pl.when(n == 0)
    def _():
        for h in range(heads):
            scores_diagonal(h, 0, 0)
        drain(0, 0)

    @pl.when(n > 0)
    def _():
        for h in range(heads):
            scores(h, 0, 0)

    def two_stages(i, carry):
        stage(2 * i, 0, False)
        stage(2 * i + 1, 1, False)
        return carry

    lax.fori_loop(0, lax.div(n - 1, 2), two_stages, None)

    @pl.when(n % 2 == 1)
    def _():
        stage(n - 1, 0, True)
        drain(n, 1)

    @pl.when(jnp.logical_and(n >= 2, n % 2 == 0))
    def _():
        stage(n - 2, 0, False)
        stage(n - 1, 1, True)
        drain(n, 0)

    for h in range(heads):
        out = acc_ref[h, :V_HEAD_DIM, :] / acc_ref[h, V_HEAD_DIM:V_HEAD_DIM + 1, :]
        o_ref[:, h * V_HEAD_DIM:(h + 1) * V_HEAD_DIM] = out.T.astype(BF16)


def _flash_call(q, k, vt, tq, heads):
    b, s, _ = k.shape
    return pl.pallas_call(
        functools.partial(_flash_kernel, tq=tq, heads=heads),
        grid=(b, N_HEADS // heads, s // tq),
        in_specs=[
            pl.BlockSpec((None, None, heads * HEAD_SLOT, tq), lambda bi, h, i: (bi, i, h, 0)),
            pl.BlockSpec((None, s, heads * HEAD_SLOT), lambda bi, h, i: (bi, 0, h)),
            pl.BlockSpec((None, s // tq, heads * V_HEAD_DIM, tq), lambda bi, h, i: (bi, 0, h, 0)),
        ],
        out_specs=pl.BlockSpec((None, tq, heads * V_HEAD_DIM), lambda bi, h, i: (bi, i, h)),
        out_shape=jax.ShapeDtypeStruct((b, s, N_HEADS * V_HEAD_DIM), BF16),
        scratch_shapes=[
            pltpu.VMEM((heads, 1, tq), F32),
            pltpu.VMEM((heads, V_HEAD_DIM + SUM_ROWS, tq), F32),
            pltpu.VMEM((2, heads, tq, tq), F32),
            pltpu.VMEM((2, heads, 1, tq), F32),
        ],
        compiler_params=_params("parallel", "parallel", "arbitrary"),
        name="flash",
    )(q, k, vt)


def _mix_kernel(x_ref, u_ref, halo_ref, attn_ref, gp_ref, ga_ref, wg_ref, sc_ref,
                wpo_ref, wao_ref, wout_ref, g2_ref, o_ref, h2_ref, *, tm, seq):
    t0 = (pl.program_id(0) * tm) % seq
    pooled = _pool_mixer(u_ref[...], halo_ref[...], t0, wg_ref, sc_ref)
    y_pool = _dot(pooled, wpo_ref[...])
    y_attn = _dot(attn_ref[...], wao_ref[...])
    mixed = gp_ref[...] * y_pool + ga_ref[...] * y_attn
    x1 = x_ref[...] + _dot(mixed.astype(BF16), wout_ref[...])
    o_ref[...] = x1
    h2_ref[...] = _rms_bf16(x1, g2_ref[...])


def _mix_call(x, u_pool, attn, gates, w_grp, scale, w_po, w_ao, w_out, g_ffn, tm, seq):
    m, d = x.shape
    width = u_pool.shape[1]
    halo_blocks = tm // POOL_HALO
    row = lambda i: (i, 0)
    fixed = lambda i: (0, 0)
    return pl.pallas_call(
        functools.partial(_mix_kernel, tm=tm, seq=seq),
        grid=(m // tm,),
        in_specs=[
            pl.BlockSpec((tm, d), row),
            pl.BlockSpec((tm, width), row),
            pl.BlockSpec((POOL_HALO, width), lambda i: (jnp.maximum(i * halo_blocks - 1, 0), 0)),
            pl.BlockSpec((tm, attn.shape[1]), row),
            pl.BlockSpec((tm, d), lambda i: (i, 0)),
            pl.BlockSpec((tm, d), lambda i: (i, 1)),
            pl.BlockSpec(w_grp.shape, lambda i: (0, 0, 0)),
            pl.BlockSpec((1, width), fixed),
            pl.BlockSpec(w_po.shape, fixed),
            pl.BlockSpec(w_ao.shape, fixed),
            pl.BlockSpec(w_out.shape, fixed),
            pl.BlockSpec((1, d), fixed),
        ],
        out_specs=[pl.BlockSpec((tm, d), row), pl.BlockSpec((tm, d), row)],
        out_shape=[jax.ShapeDtypeStruct((m, d), F32), jax.ShapeDtypeStruct((m, d), BF16)],
        compiler_params=_params("parallel"),
        name="mix",
    )(x, u_pool, u_pool, attn, gates, gates, w_grp, scale, w_po, w_ao, w_out, g_ffn)


def _ffn_kernel(h_ref, x_ref, wg_ref, wu_ref, wd_ref, o_ref, *, res_steps):
    j = pl.program_id(1)

    def step(first):
        h = h_ref[...]
        a = _dot(h, wg_ref[...].astype(BF16))
        u = _dot(h, wu_ref[...].astype(BF16))
        act = a * jax.nn.sigmoid(a) * u
        down = _dot(act.astype(BF16), wd_ref[...].astype(BF16))
        if first:
            o_ref[...] = down
        else:
            o_ref[...] += down

    pl.when(j == 0)(lambda: step(True))
    pl.when(j > 0)(lambda: step(False))

    slab = x_ref.shape[1]
    for c in range(res_steps):
        @pl.when(j == c)
        def _():
            o_ref[:, c * slab:(c + 1) * slab] += x_ref[...]


def _ffn_call(h, x, w_gate, w_up, w_down, tm, tf):
    m, d = x.shape
    f = w_gate.shape[1]
    res_steps = FFN_RESIDUAL_STEPS
    assert f // tf >= res_steps
    return pl.pallas_call(
        functools.partial(_ffn_kernel, res_steps=res_steps),
        grid=(m // tm, f // tf),
        in_specs=[
            pl.BlockSpec((tm, d), lambda i, j: (i, 0)),
            pl.BlockSpec((tm, d // res_steps), lambda i, j: (i, jnp.minimum(j, res_steps - 1))),
            pl.BlockSpec((d, tf), lambda i, j: (0, j)),
            pl.BlockSpec((d, tf), lambda i, j: (0, j)),
            pl.BlockSpec((tf, d), lambda i, j: (j, 0)),
        ],
        out_specs=pl.BlockSpec((tm, d), lambda i, j: (i, 0)),
        out_shape=jax.ShapeDtypeStruct((m, d), F32),
        compiler_params=pltpu.CompilerParams(dimension_semantics=("parallel", "arbitrary"),
                                             vmem_limit_bytes=FFN_VMEM_LIMIT),
        name="ffn",
    )(h, x, w_gate, w_up, w_down)


def _rope_slot(t):
    z = jnp.zeros(t.shape[:-1] + (ROPE_HALF,), t.dtype)
    return jnp.concatenate([t[..., :ROPE_HALF], z, t[..., ROPE_HALF:], z], axis=-1)


def _head_slot(t):
    return jnp.concatenate([t[..., :QK_NOPE_DIM], _rope_slot(t[..., QK_NOPE_DIM:])], axis=-1)


def kernel(x, positions, attn_norm_g, w_in, b_gate, q_a_norm_g, w_q_b, kv_a_norm_g, w_kv_b,
           q_norm_g, k_norm_g, w_attn_o, w_pool_grp, pool_scale, w_pool_o, w_out,
           ffn_norm_g, w_ffn_gate, w_ffn_up, w_ffn_down):
    b, s, d = x.shape
    depth = w_in.shape[0]
    m = b * s
    pool_width = w_pool_o.shape[1]

    inv_freq = ROPE_THETA ** (-jnp.arange(ROPE_HALF, dtype=F32) / ROPE_HALF)
    ang_t = positions.astype(F32).reshape(1, m) * inv_freq.reshape(ROPE_HALF, 1)
    cos_t = jnp.cos(ang_t)
    sin_t = jnp.sin(ang_t)

    xf = x.reshape(m, d)
    for l in range(depth):
        g_lat = jnp.concatenate([q_a_norm_g[l], kv_a_norm_g[l]]).reshape(1, -1)
        wqt = w_q_b[l].T.astype(BF16)
        wkv = w_kv_b[l].reshape(KV_LORA_RANK, N_HEADS, QK_NOPE_DIM + V_HEAD_DIM)
        wk = wkv[..., :QK_NOPE_DIM].reshape(KV_LORA_RANK, -1).astype(BF16)
        wvt = wkv[..., QK_NOPE_DIM:].reshape(KV_LORA_RANK, -1).T.astype(BF16)
        tq = ATTN_TILE
        gq = jnp.broadcast_to((q_norm_g[l] * (QK_HEAD_DIM ** -0.5 * LOG2_E)).reshape(-1, 1),
                              (QK_HEAD_DIM, tq))
        gk = _head_slot(k_norm_g[l]).reshape(1, HEAD_SLOT)
        g_attn_norm = attn_norm_g[l].reshape(1, d)

        lat, u_pool, gates, kr = _inproj_call(xf, g_attn_norm, w_in[l].T, g_lat,
                                              b_gate[l].reshape(1, -1), pool_width, tm=INPROJ_ROWS)
        qt, k, vt = _qkv_call(lat, kr, cos_t, sin_t, wqt, wk, wvt, gq, gk, tm=tq)
        attn = _flash_call(qt.reshape(b, s // tq, -1, tq), k.reshape(b, s, -1),
                           vt.reshape(b, s // tq, -1, tq), tq=tq, heads=ATTN_HEADS_PER_STEP)
        xf, h2 = _mix_call(xf, u_pool, attn.reshape(m, -1), gates, w_pool_grp[l].astype(BF16),
                           pool_scale[l].reshape(1, -1), w_pool_o[l].astype(BF16),
                           w_attn_o[l].astype(BF16), w_out[l].astype(BF16),
                           ffn_norm_g[l].reshape(1, d), tm=MIX_ROWS, seq=s)
        xf = _ffn_call(h2, xf, w_ffn_gate[l], w_ffn_up[l], w_ffn_down[l], tm=FFN_ROWS, tf=FFN_COLS)
    return xf.reshape(b, s, d)
```

```python
import functools

import jax
import jax.numpy as jnp
from jax import lax
from jax.experimental import pallas as pl
from jax.experimental.pallas import tpu as pltpu

F32 = jnp.float32
BF16 = jnp.bfloat16

N_HEADS = 16
QK_NOPE_DIM = 128
QK_ROPE_DIM = 64
QK_HEAD_DIM = QK_NOPE_DIM + QK_ROPE_DIM
V_HEAD_DIM = 128
Q_LORA_RANK = 512
KV_LORA_RANK = 512
ROPE_THETA = 10000.0
POOL_WINDOWS = (2, 4, 8, 16)
EPS = 1e-6

LANES = 128
SUBLANES = 8
HEAD_SLOT = 2 * LANES
ROPE_HALF = QK_ROPE_DIM // 2
POOL_HALO = 16
VMEM_LIMIT = 56 * 1024 * 1024
FFN_VMEM_LIMIT = 62 * 1024 * 1024
FFN_RESIDUAL_STEPS = 8

INPROJ_ROWS = 1024
ATTN_TILE = 512
ATTN_HEADS_PER_STEP = 4
MIX_ROWS = 256
FFN_ROWS, FFN_COLS = 1024, 512
SUM_ROWS = 16
LOG2_E = 1.4426950408889634
_CONTRACT_LAST = (((1,), (1,)), ((), ()))


def _params(*sem):
    return pltpu.CompilerParams(dimension_semantics=sem, vmem_limit_bytes=VMEM_LIMIT)


def _dot(a, b):
    return jnp.dot(a, b, preferred_element_type=F32)


def _sigmoid(x):
    return 0.5 * jnp.tanh(0.5 * x) + 0.5


def _silu(x):
    t = 0.5 * x
    return t + t * jnp.tanh(t)


def _rms_bf16(x, g):
    ms = jnp.mean(x * x, axis=-1, keepdims=True)
    return (x * lax.rsqrt(ms + EPS) * g).astype(BF16)


def _rope(t, cos, sin):
    return t * cos + pltpu.roll(t, 2 * ROPE_HALF, 1) * sin


def _dot_t(a, w):
    return lax.dot_general(a, w.astype(BF16), _CONTRACT_LAST, preferred_element_type=F32)


def _inproj_kernel(x_ref, g_ref, w_ref, gl_ref, b_ref, wkr_ref,
                   lat_ref, up_ref, gate_ref, kr_ref, h_ref, *, n_lat, n_up):
    j = pl.program_id(1)

    @pl.when(j == 0)
    def _():
        h_ref[...] = _rms_bf16(x_ref[...], g_ref[...])
        wkr = wkr_ref[...]
        z = jnp.zeros((ROPE_HALF, wkr.shape[1]), F32)
        slot = jnp.concatenate([wkr[:ROPE_HALF], z, wkr[ROPE_HALF:], z], axis=0)
        kr_ref[...] = _dot_t(h_ref[...], slot)

    @pl.when(j < n_lat)
    def _():
        lat_ref[...] = _rms_bf16(_dot_t(h_ref[...], w_ref[...]), gl_ref[...])

    @pl.when(jnp.logical_and(j >= n_lat, j < n_lat + n_up))
    def _():
        up_ref[...] = _dot_t(h_ref[...], w_ref[...])

    @pl.when(j >= n_lat + n_up)
    def _():
        gate_ref[...] = _sigmoid(_dot_t(h_ref[...], w_ref[...]) + b_ref[...])


def _inproj_call(x, g, w_t, g_lat, b_gate, pool_width, tm):
    m, d = x.shape
    tn = Q_LORA_RANK
    n_lat = g_lat.shape[1] // tn
    n_up = pool_width // tn
    n_gate = b_gate.shape[1] // tn
    o_kr = n_lat * tn
    o_up = o_kr + QK_ROPE_DIM
    clamp = lambda j, lo, n: jnp.clip(j - lo, 0, n - 1)
    sub = SUBLANES
    w_row = lambda j: sub * jnp.where(j < n_lat, j * (tn // sub),
                                      o_up // sub + (j - n_lat) * (tn // sub))
    return pl.pallas_call(
        functools.partial(_inproj_kernel, n_lat=n_lat, n_up=n_up),
        grid=(m // tm, n_lat + n_up + n_gate),
        in_specs=[
            pl.BlockSpec((tm, d), lambda i, j: (i, 0)),
            pl.BlockSpec((1, d), lambda i, j: (0, 0)),
            pl.BlockSpec((pl.Element(tn), pl.Element(d)), lambda i, j: (w_row(j), 0)),
            pl.BlockSpec((1, tn), lambda i, j: (0, clamp(j, 0, n_lat))),
            pl.BlockSpec((1, tn), lambda i, j: (0, clamp(j, n_lat + n_up, n_gate))),
            pl.BlockSpec((pl.Element(QK_ROPE_DIM), pl.Element(d)), lambda i, j: (o_kr, 0)),
        ],
        out_specs=[
            pl.BlockSpec((tm, tn), lambda i, j: (i, clamp(j, 0, n_lat))),
            pl.BlockSpec((tm, tn), lambda i, j: (i, clamp(j, n_lat, n_up))),
            pl.BlockSpec((tm, tn), lambda i, j: (i, clamp(j, n_lat + n_up, n_gate))),
            pl.BlockSpec((tm, LANES), lambda i, j: (i, 0)),
        ],
        out_shape=[
            jax.ShapeDtypeStruct((m, n_lat * tn), BF16),
            jax.ShapeDtypeStruct((m, n_up * tn), F32),
            jax.ShapeDtypeStruct((m, n_gate * tn), F32),
            jax.ShapeDtypeStruct((m, LANES), F32),
        ],
        scratch_shapes=[pltpu.VMEM((tm, d), BF16)],
        compiler_params=_params("parallel", "arbitrary"),
        name="inproj",
    )(x, g, w_t, g_lat, b_gate, w_t)


def _pool_mixer(u, halo, t0, wg_ref, sc_ref):
    tm = u.shape[0]
    halo = jnp.where(t0 > 0, halo, 0.0)
    ext = jnp.concatenate([halo, u], axis=0)
    pos = lax.broadcasted_iota(jnp.int32, (tm, 1), 0) + t0
    gd = wg_ref.shape[1]
    out = []
    for g, w in enumerate(POOL_WINDOWS):
        cols = slice(g * gd, (g + 1) * gd)
        a = ext[:, cols]
        shift = 1
        while shift < w:
            a = a + pltpu.roll(a, shift, 0)
            shift *= 2
        count = jnp.minimum(pos + 1, w).astype(F32)
        pooled = a[POOL_HALO:, :] / count - u[:, cols]
        y = _dot(pooled.astype(BF16), wg_ref[g]) * sc_ref[:, cols]
        out.append(y.astype(BF16))
    return jnp.concatenate(out, axis=1)


def _qkv_kernel(cq_ref, ckv_ref, kr_ref, cos_ref, sin_ref, wqt_ref, wk_ref, wvt_ref,
                gq_ref, gk_ref, qt_ref, k_ref, vt_ref):
    inv_dim = 1.0 / QK_HEAD_DIM
    half = ROPE_HALF
    cq = cq_ref[...]
    ckv = ckv_ref[...]
    cos_t = cos_ref[...]
    sin_t = sin_ref[...]
    qt = lax.dot_general(wqt_ref[...], cq, _CONTRACT_LAST, preferred_element_type=F32)
    vt_ref[...] = lax.dot_general(wvt_ref[...], ckv, _CONTRACT_LAST,
                                  preferred_element_type=F32).astype(BF16)
    gq = gq_ref[...]
    zeros = jnp.zeros((half, qt.shape[1]), BF16)
    for h in range(N_HEADS):
        src = h * QK_HEAD_DIM
        dst = h * HEAD_SLOT
        qh = qt[src:src + QK_HEAD_DIM]
        inv = lax.rsqrt(jnp.sum(qh * qh, axis=0, keepdims=True) * inv_dim + EPS)
        qh = qh * inv * gq
        t1 = qh[QK_NOPE_DIM:QK_NOPE_DIM + half]
        t2 = qh[QK_NOPE_DIM + half:]
        qt_ref[dst:dst + QK_NOPE_DIM] = qh[:QK_NOPE_DIM].astype(BF16)
        qt_ref[dst + LANES:dst + LANES + half] = (t1 * cos_t - t2 * sin_t).astype(BF16)
        qt_ref[dst + LANES + half:dst + LANES + 2 * half] = zeros
        qt_ref[dst + LANES + 2 * half:dst + LANES + 3 * half] = (t2 * cos_t + t1 * sin_t).astype(BF16)
        qt_ref[dst + LANES + 3 * half:dst + HEAD_SLOT] = zeros

    cos_r = cos_t.T
    sin_r = sin_t.T
    cos = jnp.concatenate([cos_r] * 4, axis=1)
    sin = jnp.concatenate([-sin_r, -sin_r, sin_r, sin_r], axis=1)
    kv = _dot(ckv, wk_ref[...])
    gk_n, gk_r = gk_ref[:, :LANES], gk_ref[:, LANES:]
    kr = kr_ref[...]
    ss_kr = jnp.sum(kr * kr, axis=-1, keepdims=True)
    kr_rot = _rope(kr * gk_r, cos, sin)
    for h in range(N_HEADS):
        lo = h * HEAD_SLOT
        kn = kv[:, h * LANES:(h + 1) * LANES]
        ssk = jnp.sum(kn * kn, axis=-1, keepdims=True) + ss_kr
        invk = lax.rsqrt(ssk * inv_dim + EPS)
        k_ref[:, lo:lo + LANES] = (kn * invk * gk_n).astype(BF16)
        k_ref[:, lo + LANES:lo + HEAD_SLOT] = (kr_rot * invk).astype(BF16)


def _qkv_call(lat, kr, cos_t, sin_t, wqt, wk, wvt, gq, gk, tm):
    m = lat.shape[0]
    rank = Q_LORA_RANK
    wide = N_HEADS * HEAD_SLOT
    vw = N_HEADS * V_HEAD_DIM
    row = lambda i: (i, 0)
    col = lambda i: (0, i)
    fixed = lambda i: (0, 0)
    return pl.pallas_call(
        _qkv_kernel,
        grid=(m // tm,),
        in_specs=[
            pl.BlockSpec((tm, rank), lambda i: (i, 0)),
            pl.BlockSpec((tm, rank), lambda i: (i, 1)),
            pl.BlockSpec((tm, LANES), row),
            pl.BlockSpec((ROPE_HALF, tm), col),
            pl.BlockSpec((ROPE_HALF, tm), col),
            pl.BlockSpec(wqt.shape, fixed),
            pl.BlockSpec(wk.shape, fixed),
            pl.BlockSpec(wvt.shape, fixed),
            pl.BlockSpec(gq.shape, fixed),
            pl.BlockSpec((1, HEAD_SLOT), fixed),
        ],
        out_specs=[
            pl.BlockSpec((None, wide, tm), lambda i: (i, 0, 0)),
            pl.BlockSpec((tm, wide), row),
            pl.BlockSpec((None, vw, tm), lambda i: (i, 0, 0)),
        ],
        out_shape=[
            jax.ShapeDtypeStruct((m // tm, wide, tm), BF16),
            jax.ShapeDtypeStruct((m, wide), BF16),
            jax.ShapeDtypeStruct((m // tm, vw, tm), BF16),
        ],
        compiler_params=_params("parallel"),
        name="qkv",
    )(lat, lat, kr, cos_t, sin_t, wqt, wk, wvt, gq, gk)


def _flash_kernel(q_ref, k_ref, vt_ref, o_ref, m_ref, acc_ref, s_ref, mx_ref, *, tq, heads):
    n = pl.program_id(2)
    ones = jnp.ones((SUM_ROWS, tq), BF16)
    m_ref[...] = jnp.full(m_ref.shape, -jnp.inf, F32)
    acc_ref[...] = jnp.zeros(acc_ref.shape, F32)

    half = tq // 2
    head_cols = lambda h: slice(h * HEAD_SLOT, (h + 1) * HEAD_SLOT)

    def v_rows(h, c):
        vt = vt_ref[c, h * V_HEAD_DIM:(h + 1) * V_HEAD_DIM, :]
        return jnp.concatenate([vt, ones], axis=0)

    def new_max(h, buf):
        m_prev = m_ref[h]
        m_new = jnp.maximum(m_prev, mx_ref[buf, h])
        m_ref[h] = m_new
        return m_new, jnp.exp2(m_prev - m_new)

    def scores(h, c, buf):
        start = pl.multiple_of(c * tq, tq)
        s = _dot(k_ref[pl.ds(start, tq), head_cols(h)], q_ref[head_cols(h), :])
        s_ref[buf, h] = s
        mx_ref[buf, h] = jnp.max(s, axis=0, keepdims=True)

    def accumulate(h, c, buf):
        m_new, alpha = new_max(h, buf)
        p = jnp.exp2(s_ref[buf, h] - m_new)
        acc_ref[h] = alpha * acc_ref[h] + _dot(v_rows(h, c), p.astype(BF16))

    def scores_diagonal(h, c, buf):
        start = pl.multiple_of(c * tq, tq)
        causal = (lax.broadcasted_iota(jnp.int32, (half, tq), 0)
                  <= lax.broadcasted_iota(jnp.int32, (half, tq), 1))
        q_t = q_ref[head_cols(h), :]
        s0 = _dot(k_ref[pl.ds(start, half), head_cols(h)], q_t)
        s1 = _dot(k_ref[pl.ds(start + half, half), head_cols(h)], q_t[:, half:])
        s0 = jnp.where(causal, s0, -jnp.inf)
        s1 = jnp.where(causal[:, :half], s1, -jnp.inf)
        s_ref[buf, h, :half, :] = s0
        s_ref[buf, h, half:, half:] = s1
        m0 = jnp.max(s0, axis=0, keepdims=True)
        m1 = jnp.max(s1, axis=0, keepdims=True)
        mx_ref[buf, h] = jnp.concatenate([m0[:, :half], jnp.maximum(m0[:, half:], m1)], axis=1)

    def accumulate_diagonal(h, c, buf):
        m_new, alpha = new_max(h, buf)
        p0 = jnp.exp2(s_ref[buf, h, :half, :] - m_new).astype(BF16)
        p1 = jnp.exp2(s_ref[buf, h, half:, half:] - m_new[:, half:]).astype(BF16)
        vt = v_rows(h, c)
        acc = alpha * acc_ref[h] + _dot(vt[:, :half], p0)
        acc_ref[h, :, :half] = acc[:, :half]
        acc_ref[h, :, half:] = acc[:, half:] + _dot(vt[:, half:], p1)

    def stage(c, buf, next_diagonal):
        for h in range(heads):
            (scores_diagonal if next_diagonal else scores)(h, c + 1, 1 - buf)
            accumulate(h, c, buf)

    def drain(c, buf):
        for h in range(heads):
            accumulate_diagonal(h, c, buf)

    @pl.when(n == 0)
    def _():
        for h in range(heads):
            scores_diagonal(h, 0, 0)
        drain(0, 0)

    @pl.when(n > 0)
    def _():
        for h in range(heads):
            scores(h, 0, 0)

    def two_stages(i, carry):
        stage(2 * i, 0, False)
        stage(2 * i + 1, 1, False)
        return carry

    lax.fori_loop(0, lax.div(n - 1, 2), two_stages, None)

    @pl.when(n % 2 == 1)
    def _():
        stage(n - 1, 0, True)
        drain(n, 1)

    @pl.when(jnp.logical_and(n >= 2, n % 2 == 0))
    def _():
        stage(n - 2, 0, False)
        stage(n - 1, 1, True)
        drain(n, 0)

    for h in range(heads):
        out = acc_ref[h, :V_HEAD_DIM, :] / acc_ref[h, V_HEAD_DIM:V_HEAD_DIM + 1, :]
        o_ref[:, h * V_HEAD_DIM:(h + 1) * V_HEAD_DIM] = out.T.astype(BF16)


def _flash_call(q, k, vt, tq, heads):
    b, s, _ = k.shape
    return pl.pallas_call(
        functools.partial(_flash_kernel, tq=tq, heads=heads),
        grid=(b, N_HEADS // heads, s // tq),
        in_specs=[
            pl.BlockSpec((None, None, heads * HEAD_SLOT, tq), lambda bi, h, i: (bi, i, h, 0)),
            pl.BlockSpec((None, s, heads * HEAD_SLOT), lambda bi, h, i: (bi, 0, h)),
            pl.BlockSpec((None, s // tq, heads * V_HEAD_DIM, tq), lambda bi, h, i: (bi, 0, h, 0)),
        ],
        out_specs=pl.BlockSpec((None, tq, heads * V_HEAD_DIM), lambda bi, h, i: (bi, i, h)),
        out_shape=jax.ShapeDtypeStruct((b, s, N_HEADS * V_HEAD_DIM), BF16),
        scratch_shapes=[
            pltpu.VMEM((heads, 1, tq), F32),
            pltpu.VMEM((heads, V_HEAD_DIM + SUM_ROWS, tq), F32),
            pltpu.VMEM((2, heads, tq, tq), F32),
            pltpu.VMEM((2, heads, 1, tq), F32),
        ],
        compiler_params=_params("parallel", "parallel", "arbitrary"),
        name="flash",
    )(q, k, vt)


def _mix_kernel(x_ref, u_ref, halo_ref, attn_ref, gp_ref, ga_ref, wg_ref, sc_ref,
                wpo_ref, wao_ref, wout_ref, g2_ref, o_ref, h2_ref, *, tm, seq):
    t0 = (pl.program_id(0) * tm) % seq
    pooled = _pool_mixer(u_ref[...], halo_ref[...], t0, wg_ref, sc_ref)
    y_pool = _dot(pooled, wpo_ref[...])
    y_attn = _dot(attn_ref[...], wao_ref[...])
    mixed = gp_ref[...] * y_pool + ga_ref[...] * y_attn
    x1 = x_ref[...] + _dot(mixed.astype(BF16), wout_ref[...])
    o_ref[...] = x1
    h2_ref[...] = _rms_bf16(x1, g2_ref[...])


def _mix_call(x, u_pool, attn, gates, w_grp, scale, w_po, w_ao, w_out, g_ffn, tm, seq):
    m, d = x.shape
    width = u_pool.shape[1]
    halo_blocks = tm // POOL_HALO
    row = lambda i: (i, 0)
    fixed = lambda i: (0, 0)
    return pl.pallas_call(
        functools.partial(_mix_kernel, tm=tm, seq=seq),
        grid=(m // tm,),
        in_specs=[
            pl.BlockSpec((tm, d), row),
            pl.BlockSpec((tm, width), row),
            pl.BlockSpec((POOL_HALO, width), lambda i: (jnp.maximum(i * halo_blocks - 1, 0), 0)),
            pl.BlockSpec((tm, attn.shape[1]), row),
            pl.BlockSpec((tm, d), lambda i: (i, 0)),
            pl.BlockSpec((tm, d), lambda i: (i, 1)),
            pl.BlockSpec(w_grp.shape, lambda i: (0, 0, 0)),
            pl.BlockSpec((1, width), fixed),
            pl.BlockSpec(w_po.shape, fixed),
            pl.BlockSpec(w_ao.shape, fixed),
            pl.BlockSpec(w_out.shape, fixed),
            pl.BlockSpec((1, d), fixed),
        ],
        out_specs=[pl.BlockSpec((tm, d), row), pl.BlockSpec((tm, d), row)],
        out_shape=[jax.ShapeDtypeStruct((m, d), F32), jax.ShapeDtypeStruct((m, d), BF16)],
        compiler_params=_params("parallel"),
        name="mix",
    )(x, u_pool, u_pool, attn, gates, gates, w_grp, scale, w_po, w_ao, w_out, g_ffn)


def _ffn_kernel(h_ref, x_ref, wg_ref, wu_ref, wd_ref, o_ref, *, res_steps):
    j = pl.program_id(1)

    def step(first):
        h = h_ref[...]
        a = _dot(h, wg_ref[...].astype(BF16))
        u = _dot(h, wu_ref[...].astype(BF16))
        act = _silu(a) * u
        down = _dot(act.astype(BF16), wd_ref[...].astype(BF16))
        if first:
            o_ref[...] = down
        else:
            o_ref[...] += down

    pl.when(j == 0)(lambda: step(True))
    pl.when(j > 0)(lambda: step(False))

    slab = x_ref.shape[1]
    for c in range(res_steps):
        @pl.when(j == c)
        def _():
            o_ref[:, c * slab:(c + 1) * slab] += x_ref[...]


def _ffn_call(h, x, w_gate, w_up, w_down, tm, tf):
    m, d = x.shape
    f = w_gate.shape[1]
    res_steps = FFN_RESIDUAL_STEPS
    assert f // tf >= res_steps
    return pl.pallas_call(
        functools.partial(_ffn_kernel, res_steps=res_steps),
        grid=(m // tm, f // tf),
        in_specs=[
            pl.BlockSpec((tm, d), lambda i, j: (i, 0)),
            pl.BlockSpec((tm, d // res_steps), lambda i, j: (i, jnp.minimum(j, res_steps - 1))),
            pl.BlockSpec((d, tf), lambda i, j: (0, j)),
            pl.BlockSpec((d, tf), lambda i, j: (0, j)),
            pl.BlockSpec((tf, d), lambda i, j: (j, 0)),
        ],
        out_specs=pl.BlockSpec((tm, d), lambda i, j: (i, 0)),
        out_shape=jax.ShapeDtypeStruct((m, d), F32),
        compiler_params=pltpu.CompilerParams(dimension_semantics=("parallel", "arbitrary"),
                                             vmem_limit_bytes=FFN_VMEM_LIMIT),
        name="ffn",
    )(h, x, w_gate, w_up, w_down)


def _rope_slot(t):
    z = jnp.zeros(t.shape[:-1] + (ROPE_HALF,), t.dtype)
    return jnp.concatenate([t[..., :ROPE_HALF], z, t[..., ROPE_HALF:], z], axis=-1)


def _head_slot(t):
    return jnp.concatenate([t[..., :QK_NOPE_DIM], _rope_slot(t[..., QK_NOPE_DIM:])], axis=-1)


def kernel(x, positions, attn_norm_g, w_in, b_gate, q_a_norm_g, w_q_b, kv_a_norm_g, w_kv_b,
           q_norm_g, k_norm_g, w_attn_o, w_pool_grp, pool_scale, w_pool_o, w_out,
           ffn_norm_g, w_ffn_gate, w_ffn_up, w_ffn_down):
    b, s, d = x.shape
    depth = w_in.shape[0]
    m = b * s
    pool_width = w_pool_o.shape[1]

    inv_freq = ROPE_THETA ** (-jnp.arange(ROPE_HALF, dtype=F32) / ROPE_HALF)
    ang_t = positions.astype(F32).reshape(1, m) * inv_freq.reshape(ROPE_HALF, 1)
    cos_t = jnp.cos(ang_t)
    sin_t = jnp.sin(ang_t)

    xf = x.reshape(m, d)
    for l in range(depth):
        g_lat = jnp.concatenate([q_a_norm_g[l], kv_a_norm_g[l]]).reshape(1, -1)
        wqt = w_q_b[l].T.astype(BF16)
        wkv = w_kv_b[l].reshape(KV_LORA_RANK, N_HEADS, QK_NOPE_DIM + V_HEAD_DIM)
        wk = wkv[..., :QK_NOPE_DIM].reshape(KV_LORA_RANK, -1).astype(BF16)
        wvt = wkv[..., QK_NOPE_DIM:].reshape(KV_LORA_RANK, -1).T.astype(BF16)
        tq = ATTN_TILE
        gq = jnp.broadcast_to((q_norm_g[l] * (QK_HEAD_DIM ** -0.5 * LOG2_E)).reshape(-1, 1),
                              (QK_HEAD_DIM, tq))
        gk = _head_slot(k_norm_g[l]).reshape(1, HEAD_SLOT)
        g_attn_norm = attn_norm_g[l].reshape(1, d)

        lat, u_pool, gates, kr = _inproj_call(xf, g_attn_norm, w_in[l].T, g_lat,
                                              b_gate[l].reshape(1, -1), pool_width, tm=INPROJ_ROWS)
        qt, k, vt = _qkv_call(lat, kr, cos_t, sin_t, wqt, wk, wvt, gq, gk, tm=tq)
        attn = _flash_call(qt.reshape(b, s // tq, -1, tq), k.reshape(b, s, -1),
                           vt.reshape(b, s // tq, -1, tq), tq=tq, heads=ATTN_HEADS_PER_STEP)
        xf, h2 = _mix_call(xf, u_pool, attn.reshape(m, -1), gates, w_pool_grp[l].astype(BF16),
                           pool_scale[l].reshape(1, -1), w_pool_o[l].astype(BF16),
                           w_attn_o[l].astype(BF16), w_out[l].astype(BF16),
                           ffn_norm_g[l].reshape(1, d), tm=MIX_ROWS, seq=s)
        xf = _ffn_call(h2, xf, w_ffn_gate[l], w_ffn_up[l], w_ffn_down[l], tm=FFN_ROWS, tf=FFN_COLS)
    return xf.reshape(b, s, d)
```

```python
import functools

import jax
import jax.numpy as jnp
from jax import lax
from jax.experimental import pallas as pl
from jax.experimental.pallas import tpu as pltpu

F32 = jnp.float32
BF16 = jnp.bfloat16

N_HEADS = 16
QK_NOPE_DIM = 128
QK_ROPE_DIM = 64
QK_HEAD_DIM = QK_NOPE_DIM + QK_ROPE_DIM
V_HEAD_DIM = 128
Q_LORA_RANK = 512
KV_LORA_RANK = 512
ROPE_THETA = 10000.0
POOL_WINDOWS = (2, 4, 8, 16)
EPS = 1e-6

LANES = 128
SUBLANES = 8
HEAD_SLOT = 2 * LANES
ROPE_HALF = QK_ROPE_DIM // 2
POOL_HALO = 16
VMEM_LIMIT = 56 * 1024 * 1024
FFN_VMEM_LIMIT = 62 * 1024 * 1024
FFN_RESIDUAL_STEPS = 8

INPROJ_ROWS = 1024
ATTN_TILE = 512
ATTN_HEADS_PER_STEP = 4
MIX_ROWS = 256
FFN_ROWS, FFN_COLS = 1024, 512
SUM_ROWS = 16
LOG2_E = 1.4426950408889634
_CONTRACT_LAST = (((1,), (1,)), ((), ()))


def _params(*sem):
    return pltpu.CompilerParams(dimension_semantics=sem, vmem_limit_bytes=VMEM_LIMIT)


def _dot(a, b):
    return jnp.dot(a, b, preferred_element_type=F32)


def _sigmoid(x):
    return 0.5 * jnp.tanh(0.5 * x) + 0.5


def _silu(x):
    t = 0.5 * x
    return t + t * jnp.tanh(t)


def _rms_bf16(x, g):
    ms = jnp.mean(x * x, axis=-1, keepdims=True)
    return (x * lax.rsqrt(ms + EPS) * g).astype(BF16)


def _rope(t, cos, sin):
    return t * cos + pltpu.roll(t, 2 * ROPE_HALF, 1) * sin


def _dot_t(a, w):
    return lax.dot_general(a, w.astype(BF16), _CONTRACT_LAST, preferred_element_type=F32)


def _inproj_kernel(x_ref, g_ref, w_ref, gl_ref, b_ref, wkr_ref,
                   lat_ref, up_ref, gate_ref, kr_ref, h_ref, *, n_lat, n_up):
    j = pl.program_id(1)

    @pl.when(j == 0)
    def _():
        h_ref[...] = _rms_bf16(x_ref[...], g_ref[...])
        wkr = wkr_ref[...]
        z = jnp.zeros((ROPE_HALF, wkr.shape[1]), F32)
        slot = jnp.concatenate([wkr[:ROPE_HALF], z, wkr[ROPE_HALF:], z], axis=0)
        kr_ref[...] = _dot_t(h_ref[...], slot)

    @pl.when(j < n_lat)
    def _():
        lat_ref[...] = _rms_bf16(_dot_t(h_ref[...], w_ref[...]), gl_ref[...])

    @pl.when(jnp.logical_and(j >= n_lat, j < n_lat + n_up))
    def _():
        up_ref[...] = _dot_t(h_ref[...], w_ref[...])

    @pl.when(j >= n_lat + n_up)
    def _():
        gate_ref[...] = _sigmoid(_dot_t(h_ref[...], w_ref[...]) + b_ref[...])


def _inproj_call(x, g, w_t, g_lat, b_gate, pool_width, tm):
    m, d = x.shape
    tn = Q_LORA_RANK
    n_lat = g_lat.shape[1] // tn
    n_up = pool_width // tn
    n_gate = b_gate.shape[1] // tn
    o_kr = n_lat * tn
    o_up = o_kr + QK_ROPE_DIM
    clamp = lambda j, lo, n: jnp.clip(j - lo, 0, n - 1)
    sub = SUBLANES
    w_row = lambda j: sub * jnp.where(j < n_lat, j * (tn // sub),
                                      o_up // sub + (j - n_lat) * (tn // sub))
    return pl.pallas_call(
        functools.partial(_inproj_kernel, n_lat=n_lat, n_up=n_up),
        grid=(m // tm, n_lat + n_up + n_gate),
        in_specs=[
            pl.BlockSpec((tm, d), lambda i, j: (i, 0)),
            pl.BlockSpec((1, d), lambda i, j: (0, 0)),
            pl.BlockSpec((pl.Element(tn), pl.Element(d)), lambda i, j: (w_row(j), 0)),
            pl.BlockSpec((1, tn), lambda i, j: (0, clamp(j, 0, n_lat))),
            pl.BlockSpec((1, tn), lambda i, j: (0, clamp(j, n_lat + n_up, n_gate))),
            pl.BlockSpec((pl.Element(QK_ROPE_DIM), pl.Element(d)), lambda i, j: (o_kr, 0)),
        ],
        out_specs=[
            pl.BlockSpec((tm, tn), lambda i, j: (i, clamp(j, 0, n_lat))),
            pl.BlockSpec((tm, tn), lambda i, j: (i, clamp(j, n_lat, n_up))),
            pl.BlockSpec((tm, tn), lambda i, j: (i, clamp(j, n_lat + n_up, n_gate))),
            pl.BlockSpec((tm, LANES), lambda i, j: (i, 0)),
        ],
        out_shape=[
            jax.ShapeDtypeStruct((m, n_lat * tn), BF16),
            jax.ShapeDtypeStruct((m, n_up * tn), F32),
            jax.ShapeDtypeStruct((m, n_gate * tn), F32),
            jax.ShapeDtypeStruct((m, LANES), F32),
        ],
        scratch_shapes=[pltpu.VMEM((tm, d), BF16)],
        compiler_params=_params("parallel", "arbitrary"),
        name="inproj",
    )(x, g, w_t, g_lat, b_gate, w_t)


def _pool_mixer(u, halo, t0, wg_ref, sc_ref):
    tm = u.shape[0]
    halo = jnp.where(t0 > 0, halo, 0.0)
    ext = jnp.concatenate([halo, u], axis=0)
    pos = lax.broadcasted_iota(jnp.int32, (tm, 1), 0) + t0
    gd = wg_ref.shape[1]
    out = []
    for g, w in enumerate(POOL_WINDOWS):
        cols = slice(g * gd, (g + 1) * gd)
        a = ext[:, cols]
        shift = 1
        while shift < w:
            a = a + pltpu.roll(a, shift, 0)
            shift *= 2
        count = jnp.minimum(pos + 1, w).astype(F32)
        pooled = a[POOL_HALO:, :] / count - u[:, cols]
        y = _dot(pooled.astype(BF16), wg_ref[g]) * sc_ref[:, cols]
        out.append(y.astype(BF16))
    return jnp.concatenate(out, axis=1)


def _qkv_kernel(cq_ref, ckv_ref, kr_ref, cos_ref, sin_ref, wqt_ref, wk_ref, wvt_ref,
                gq_ref, gk_ref, qt_ref, k_ref, vt_ref):
    inv_dim = 1.0 / QK_HEAD_DIM
    half = ROPE_HALF
    cq = cq_ref[...]
    ckv = ckv_ref[...]
    cos_t = cos_ref[...]
    sin_t = sin_ref[...]
    qt = lax.dot_general(wqt_ref[...], cq, _CONTRACT_LAST, preferred_element_type=F32)
    vt_ref[...] = lax.dot_general(wvt_ref[...], ckv, _CONTRACT_LAST,
                                  preferred_element_type=F32).astype(BF16)
    gq = gq_ref[...]
    zeros = jnp.zeros((half, qt.shape[1]), BF16)
    for h in range(N_HEADS):
        src = h * QK_HEAD_DIM
        dst = h * HEAD_SLOT
        qh = qt[src:src + QK_HEAD_DIM]
        inv = lax.rsqrt(jnp.sum(qh * qh, axis=0, keepdims=True) * inv_dim + EPS)
        qh = qh * inv * gq
        t1 = qh[QK_NOPE_DIM:QK_NOPE_DIM + half]
        t2 = qh[QK_NOPE_DIM + half:]
        qt_ref[dst:dst + QK_NOPE_DIM] = qh[:QK_NOPE_DIM].astype(BF16)
        qt_ref[dst + LANES:dst + LANES + half] = (t1 * cos_t - t2 * sin_t).astype(BF16)
        qt_ref[dst + LANES + half:dst + LANES + 2 * half] = zeros
        qt_ref[dst + LANES + 2 * half:dst + LANES + 3 * half] = (t2 * cos_t + t1 * sin_t).astype(BF16)
        qt_ref[dst + LANES + 3 * half:dst + HEAD_SLOT] = zeros

    cos_r = cos_t.T
    sin_r = sin_t.T
    cos = jnp.concatenate([cos_r] * 4, axis=1)
    sin = jnp.concatenate([-sin_r, -sin_r, sin_r, sin_r], axis=1)
    kv = _dot(ckv, wk_ref[...])
    gk_n, gk_r = gk_ref[:, :LANES], gk_ref[:, LANES:]
    kr = kr_ref[...]
    ss_kr = jnp.sum(kr * kr, axis=-1, keepdims=True)
    kr_rot = _rope(kr * gk_r, cos, sin)
    for h in range(N_HEADS):
        lo = h * HEAD_SLOT
        kn = kv[:, h * LANES:(h + 1) * LANES]
        ssk = jnp.sum(kn * kn, axis=-1, keepdims=True) + ss_kr
        invk = lax.rsqrt(ssk * inv_dim + EPS)
        k_ref[:, lo:lo + LANES] = (kn * invk * gk_n).astype(BF16)
        k_ref[:, lo + LANES:lo + HEAD_SLOT] = (kr_rot * invk).astype(BF16)


def _qkv_call(lat, kr, cos_t, sin_t, wqt, wk, wvt, gq, gk, tm):
    m = lat.shape[0]
    rank = Q_LORA_RANK
    wide = N_HEADS * HEAD_SLOT
    vw = N_HEADS * V_HEAD_DIM
    row = lambda i: (i, 0)
    col = lambda i: (0, i)
    fixed = lambda i: (0, 0)
    return pl.pallas_call(
        _qkv_kernel,
        grid=(m // tm,),
        in_specs=[
            pl.BlockSpec((tm, rank), lambda i: (i, 0)),
            pl.BlockSpec((tm, rank), lambda i: (i, 1)),
            pl.BlockSpec((tm, LANES), row),
            pl.BlockSpec((ROPE_HALF, tm), col),
            pl.BlockSpec((ROPE_HALF, tm), col),
            pl.BlockSpec(wqt.shape, fixed),
            pl.BlockSpec(wk.shape, fixed),
            pl.BlockSpec(wvt.shape, fixed),
            pl.BlockSpec(gq.shape, fixed),
            pl.BlockSpec((1, HEAD_SLOT), fixed),
        ],
        out_specs=[
            pl.BlockSpec((None, wide, tm), lambda i: (i, 0, 0)),
            pl.BlockSpec((tm, wide), row),
            pl.BlockSpec((None, vw, tm), lambda i: (i, 0, 0)),
        ],
        out_shape=[
            jax.ShapeDtypeStruct((m // tm, wide, tm), BF16),
            jax.ShapeDtypeStruct((m, wide), BF16),
            jax.ShapeDtypeStruct((m // tm, vw, tm), BF16),
        ],
        compiler_params=_params("parallel"),
        name="qkv",
    )(lat, lat, kr, cos_t, sin_t, wqt, wk, wvt, gq, gk)


def _flash_kernel(q_ref, k_ref, vt_ref, *refs, tq, heads, n_cast):
    cast_src, (o_ref, *cast_dst), (m_ref, acc_ref, s_ref, mx_ref) = (
        refs[:n_cast], refs[n_cast:2 * n_cast + 1], refs[2 * n_cast + 1:])
    for src, dst in zip(cast_src, cast_dst):
        dst[...] = src[...].astype(BF16)
    n = pl.program_id(2)
    ones = jnp.ones((SUM_ROWS, tq), BF16)
    m_ref[...] = jnp.full(m_ref.shape, -jnp.inf, F32)
    acc_ref[...] = jnp.zeros(acc_ref.shape, F32)

    half = tq // 2
    head_cols = lambda h: slice(h * HEAD_SLOT, (h + 1) * HEAD_SLOT)

    def v_rows(h, c):
        vt = vt_ref[c, h * V_HEAD_DIM:(h + 1) * V_HEAD_DIM, :]
        return jnp.concatenate([vt, ones], axis=0)

    def new_max(h, buf):
        m_prev = m_ref[h]
        m_new = jnp.maximum(m_prev, mx_ref[buf, h])
        m_ref[h] = m_new
        return m_new, jnp.exp2(m_prev - m_new)

    def scores(h, c, buf):
        start = pl.multiple_of(c * tq, tq)
        s = _dot(k_ref[pl.ds(start, tq), head_cols(h)], q_ref[head_cols(h), :])
        s_ref[buf, h] = s
        mx_ref[buf, h] = jnp.max(s, axis=0, keepdims=True)

    def accumulate(h, c, buf):
        m_new, alpha = new_max(h, buf)
        p = jnp.exp2(s_ref[buf, h] - m_new)
        acc_ref[h] = alpha * acc_ref[h] + _dot(v_rows(h, c), p.astype(BF16))

    def scores_diagonal(h, c, buf):
        start = pl.multiple_of(c * tq, tq)
        causal = (lax.broadcasted_iota(jnp.int32, (half, tq), 0)
                  <= lax.broadcasted_iota(jnp.int32, (half, tq), 1))
        q_t = q_ref[head_cols(h), :]
        s0 = _dot(k_ref[pl.ds(start, half), head_cols(h)], q_t)
        s1 = _dot(k_ref[pl.ds(start + half, half), head_cols(h)], q_t[:, half:])
        s0 = jnp.where(causal, s0, -jnp.inf)
        s1 = jnp.where(causal[:, :half], s1, -jnp.inf)
        s_ref[buf, h, :half, :] = s0
        s_ref[buf, h, half:, half:] = s1
        m0 = jnp.max(s0, axis=0, keepdims=True)
        m1 = jnp.max(s1, axis=0, keepdims=True)
        mx_ref[buf, h] = jnp.concatenate([m0[:, :half], jnp.maximum(m0[:, half:], m1)], axis=1)

    def accumulate_diagonal(h, c, buf):
        m_new, alpha = new_max(h, buf)
        p0 = jnp.exp2(s_ref[buf, h, :half, :] - m_new).astype(BF16)
        p1 = jnp.exp2(s_ref[buf, h, half:, half:] - m_new[:, half:]).astype(BF16)
        vt = v_rows(h, c)
        acc = alpha * acc_ref[h] + _dot(vt[:, :half], p0)
        acc_ref[h, :, :half] = acc[:, :half]
        acc_ref[h, :, half:] = acc[:, half:] + _dot(vt[:, half:], p1)

    def stage(c, buf, next_diagonal):
        for h in range(heads):
            (scores_diagonal if next_diagonal else scores)(h, c + 1, 1 - buf)
            accumulate(h, c, buf)

    def drain(c, buf):
        for h in range(heads):
            accumulate_diagonal(h, c, buf)

    @pl.when(n == 0)
    def _():
        for h in range(heads):
            scores_diagonal(h, 0, 0)
        drain(0, 0)

    @pl.when(n > 0)
    def _():
        for h in range(heads):
            scores(h, 0, 0)

    def two_stages(i, carry):
        stage(2 * i, 0, False)
        stage(2 * i + 1, 1, False)
        return carry

    lax.fori_loop(0, lax.div(n - 1, 2), two_stages, None)

    @pl.when(n % 2 == 1)
    def _():
        stage(n - 1, 0, True)
        drain(n, 1)

    @pl.when(jnp.logical_and(n >= 2, n % 2 == 0))
    def _():
        stage(n - 2, 0, False)
        stage(n - 1, 1, True)
        drain(n, 0)

    for h in range(heads):
        out = acc_ref[h, :V_HEAD_DIM, :] / acc_ref[h, V_HEAD_DIM:V_HEAD_DIM + 1, :]
        o_ref[:, h * V_HEAD_DIM:(h + 1) * V_HEAD_DIM] = out.T.astype(BF16)


def _flash_call(q, k, vt, cast_weights, tq, heads):
    b, s, _ = k.shape
    grid = (b, N_HEADS // heads, s // tq)
    n_steps = grid[0] * grid[1] * grid[2]
    step = lambda bi, h, i: ((bi * grid[1] + h) * grid[2] + i, 0)
    cast_specs = [pl.BlockSpec((w.shape[0] // n_steps, w.shape[1]), step) for w in cast_weights]
    return pl.pallas_call(
        functools.partial(_flash_kernel, tq=tq, heads=heads, n_cast=len(cast_weights)),
        grid=grid,
        in_specs=[
            pl.BlockSpec((None, None, heads * HEAD_SLOT, tq), lambda bi, h, i: (bi, i, h, 0)),
            pl.BlockSpec((None, s, heads * HEAD_SLOT), lambda bi, h, i: (bi, 0, h)),
            pl.BlockSpec((None, s // tq, heads * V_HEAD_DIM, tq), lambda bi, h, i: (bi, 0, h, 0)),
        ] + cast_specs,
        out_specs=[pl.BlockSpec((None, tq, heads * V_HEAD_DIM), lambda bi, h, i: (bi, i, h))] + cast_specs,
        out_shape=[jax.ShapeDtypeStruct((b, s, N_HEADS * V_HEAD_DIM), BF16)]
        + [jax.ShapeDtypeStruct(w.shape, BF16) for w in cast_weights],
        scratch_shapes=[
            pltpu.VMEM((heads, 1, tq), F32),
            pltpu.VMEM((heads, V_HEAD_DIM + SUM_ROWS, tq), F32),
            pltpu.VMEM((2, heads, tq, tq), F32),
            pltpu.VMEM((2, heads, 1, tq), F32),
        ],
        compiler_params=_params("parallel", "parallel", "arbitrary"),
        name="flash",
    )(q, k, vt, *cast_weights)


def _mix_kernel(x_ref, u_ref, halo_ref, attn_ref, gp_ref, ga_ref, wg_ref, sc_ref,
                wpo_ref, wao_ref, wout_ref, g2_ref, o_ref, h2_ref, *, tm, seq):
    t0 = (pl.program_id(0) * tm) % seq
    pooled = _pool_mixer(u_ref[...], halo_ref[...], t0, wg_ref, sc_ref)
    y_pool = _dot(pooled, wpo_ref[...])
    y_attn = _dot(attn_ref[...], wao_ref[...])
    mixed = gp_ref[...] * y_pool + ga_ref[...] * y_attn
    x1 = x_ref[...] + _dot(mixed.astype(BF16), wout_ref[...])
    o_ref[...] = x1
    h2_ref[...] = _rms_bf16(x1, g2_ref[...])


def _mix_call(x, u_pool, attn, gates, w_grp, scale, w_po, w_ao, w_out, g_ffn, tm, seq):
    m, d = x.shape
    width = u_pool.shape[1]
    halo_blocks = tm // POOL_HALO
    row = lambda i: (i, 0)
    fixed = lambda i: (0, 0)
    return pl.pallas_call(
        functools.partial(_mix_kernel, tm=tm, seq=seq),
        grid=(m // tm,),
        in_specs=[
            pl.BlockSpec((tm, d), row),
            pl.BlockSpec((tm, width), row),
            pl.BlockSpec((POOL_HALO, width), lambda i: (jnp.maximum(i * halo_blocks - 1, 0), 0)),
            pl.BlockSpec((tm, attn.shape[1]), row),
            pl.BlockSpec((tm, d), lambda i: (i, 0)),
            pl.BlockSpec((tm, d), lambda i: (i, 1)),
            pl.BlockSpec(w_grp.shape, lambda i: (0, 0, 0)),
            pl.BlockSpec((1, width), fixed),
            pl.BlockSpec(w_po.shape, fixed),
            pl.BlockSpec(w_ao.shape, fixed),
            pl.BlockSpec(w_out.shape, fixed),
            pl.BlockSpec((1, d), fixed),
        ],
        out_specs=[pl.BlockSpec((tm, d), row), pl.BlockSpec((tm, d), row)],
        out_shape=[jax.ShapeDtypeStruct((m, d), F32), jax.ShapeDtypeStruct((m, d), BF16)],
        compiler_params=_params("parallel"),
        name="mix",
    )(x, u_pool, u_pool, attn, gates, gates, w_grp, scale, w_po, w_ao, w_out, g_ffn)


def _ffn_kernel(h_ref, x_ref, wg_ref, wu_ref, wd_ref, o_ref, *, res_steps):
    j = pl.program_id(1)

    def step(first):
        h = h_ref[...]
        a = _dot(h, wg_ref[...].astype(BF16))
        u = _dot(h, wu_ref[...].astype(BF16))
        act = _silu(a) * u
        down = _dot(act.astype(BF16), wd_ref[...].astype(BF16))
        if first:
            o_ref[...] = down
        else:
            o_ref[...] += down

    pl.when(j == 0)(lambda: step(True))
    pl.when(j > 0)(lambda: step(False))

    slab = x_ref.shape[1]
    for c in range(res_steps):
        @pl.when(j == c)
        def _():
            o_ref[:, c * slab:(c + 1) * slab] += x_ref[...]


def _ffn_call(h, x, w_gate, w_up, w_down, tm, tf):
    m, d = x.shape
    f = w_gate.shape[1]
    res_steps = FFN_RESIDUAL_STEPS
    assert f // tf >= res_steps
    return pl.pallas_call(
        functools.partial(_ffn_kernel, res_steps=res_steps),
        grid=(m // tm, f // tf),
        in_specs=[
            pl.BlockSpec((tm, d), lambda i, j: (i, 0)),
            pl.BlockSpec((tm, d // res_steps), lambda i, j: (i, jnp.minimum(j, res_steps - 1))),
            pl.BlockSpec((d, tf), lambda i, j: (0, j)),
            pl.BlockSpec((d, tf), lambda i, j: (0, j)),
            pl.BlockSpec((tf, d), lambda i, j: (j, 0)),
        ],
        out_specs=pl.BlockSpec((tm, d), lambda i, j: (i, 0)),
        out_shape=jax.ShapeDtypeStruct((m, d), F32),
        compiler_params=pltpu.CompilerParams(dimension_semantics=("parallel", "arbitrary"),
                                             vmem_limit_bytes=FFN_VMEM_LIMIT),
        name="ffn",
    )(h, x, w_gate, w_up, w_down)


def _rope_slot(t):
    z = jnp.zeros(t.shape[:-1] + (ROPE_HALF,), t.dtype)
    return jnp.concatenate([t[..., :ROPE_HALF], z, t[..., ROPE_HALF:], z], axis=-1)


def _head_slot(t):
    return jnp.concatenate([t[..., :QK_NOPE_DIM], _rope_slot(t[..., QK_NOPE_DIM:])], axis=-1)


def kernel(x, positions, attn_norm_g, w_in, b_gate, q_a_norm_g, w_q_b, kv_a_norm_g, w_kv_b,
           q_norm_g, k_norm_g, w_attn_o, w_pool_grp, pool_scale, w_pool_o, w_out,
           ffn_norm_g, w_ffn_gate, w_ffn_up, w_ffn_down):
    b, s, d = x.shape
    depth = w_in.shape[0]
    m = b * s
    pool_width = w_pool_o.shape[1]

    inv_freq = ROPE_THETA ** (-jnp.arange(ROPE_HALF, dtype=F32) / ROPE_HALF)
    ang_t = positions.astype(F32).reshape(1, m) * inv_freq.reshape(ROPE_HALF, 1)
    cos_t = jnp.cos(ang_t)
    sin_t = jnp.sin(ang_t)

    xf = x.reshape(m, d)
    for l in range(depth):
        g_lat = jnp.concatenate([q_a_norm_g[l], kv_a_norm_g[l]]).reshape(1, -1)
        wqt = w_q_b[l].T.astype(BF16)
        wkv = w_kv_b[l].reshape(KV_LORA_RANK, N_HEADS, QK_NOPE_DIM + V_HEAD_DIM)
        wk = wkv[..., :QK_NOPE_DIM].reshape(KV_LORA_RANK, -1).astype(BF16)
        wvt = wkv[..., QK_NOPE_DIM:].reshape(KV_LORA_RANK, -1).T.astype(BF16)
        tq = ATTN_TILE
        gq = jnp.broadcast_to((q_norm_g[l] * (QK_HEAD_DIM ** -0.5 * LOG2_E)).reshape(-1, 1),
                              (QK_HEAD_DIM, tq))
        gk = _head_slot(k_norm_g[l]).reshape(1, HEAD_SLOT)
        g_attn_norm = attn_norm_g[l].reshape(1, d)

        lat, u_pool, gates, kr = _inproj_call(xf, g_attn_norm, w_in[l].T, g_lat,
                                              b_gate[l].reshape(1, -1), pool_width, tm=INPROJ_ROWS)
        qt, k, vt = _qkv_call(lat, kr, cos_t, sin_t, wqt, wk, wvt, gq, gk, tm=tq)
        attn, w_po, w_ao, w_o = _flash_call(qt.reshape(b, s // tq, -1, tq), k.reshape(b, s, -1),
                                            vt.reshape(b, s // tq, -1, tq),
                                            [w_pool_o[l], w_attn_o[l], w_out[l]],
                                            tq=tq, heads=ATTN_HEADS_PER_STEP)
        xf, h2 = _mix_call(xf, u_pool, attn.reshape(m, -1), gates, w_pool_grp[l].astype(BF16),
                           pool_scale[l].reshape(1, -1), w_po, w_ao, w_o,
                           ffn_norm_g[l].reshape(1, d), tm=MIX_ROWS, seq=s)
        xf = _ffn_call(h2, xf, w_ffn_gate[l], w_ffn_up[l], w_ffn_down[l], tm=FFN_ROWS, tf=FFN_COLS)
    return xf.reshape(b, s, d)
```

```python
import functools

import jax
import jax.numpy as jnp
from jax import lax
from jax.experimental import pallas as pl
from jax.experimental.pallas import tpu as pltpu

F32 = jnp.float32
BF16 = jnp.bfloat16

N_HEADS = 16
QK_NOPE_DIM = 128
QK_ROPE_DIM = 64
QK_HEAD_DIM = QK_NOPE_DIM + QK_ROPE_DIM
V_HEAD_DIM = 128
Q_LORA_RANK = 512
KV_LORA_RANK = 512
ROPE_THETA = 10000.0
POOL_WINDOWS = (2, 4, 8, 16)
EPS = 1e-6

LANES = 128
SUBLANES = 8
HEAD_SLOT = 2 * LANES
ROPE_HALF = QK_ROPE_DIM // 2
POOL_HALO = 16
VMEM_LIMIT = 56 * 1024 * 1024
FFN_VMEM_LIMIT = 62 * 1024 * 1024
FFN_RESIDUAL_STEPS = 8

INPROJ_ROWS = 1024
ATTN_TILE = 512
ATTN_HEADS_PER_STEP = 4
MIX_ROWS = 256
FFN_ROWS, FFN_COLS = 1024, 512
WEIGHT_RING = 3
SUM_ROWS = 16
LOG2_E = 1.4426950408889634
_CONTRACT_LAST = (((1,), (1,)), ((), ()))


def _params(*sem):
    return pltpu.CompilerParams(dimension_semantics=sem, vmem_limit_bytes=VMEM_LIMIT)


def _dot(a, b):
    return jnp.dot(a, b, preferred_element_type=F32)


def _sigmoid(x):
    return 0.5 * jnp.tanh(0.5 * x) + 0.5


def _silu(x):
    t = 0.5 * x
    return t + t * jnp.tanh(t)


def _rms_bf16(x, g):
    ms = jnp.mean(x * x, axis=-1, keepdims=True)
    return (x * lax.rsqrt(ms + EPS) * g).astype(BF16)


def _rope(t, cos, sin):
    return t * cos + pltpu.roll(t, 2 * ROPE_HALF, 1) * sin


def _dot_t(a, w):
    return lax.dot_general(a, w.astype(BF16), _CONTRACT_LAST, preferred_element_type=F32)


def _inproj_kernel(x_ref, g_ref, w_hbm, gl_ref, b_ref, wkr_ref,
                   lat_ref, up_ref, gate_ref, kr_ref, h_ref, w_ring, w_sem, *, n_lat, n_up, w_row, tn):
    i = pl.program_id(0)
    j = pl.program_id(1)
    n_j = pl.num_programs(1)
    t = i * n_j + j
    last = pl.num_programs(0) * n_j - 1

    def w_copy(step):
        slot = lax.rem(step, WEIGHT_RING)
        rows = pl.ds(w_row(lax.rem(step, n_j)), tn)
        return pltpu.make_async_copy(w_hbm.at[rows, :], w_ring.at[slot], w_sem.at[slot])

    @pl.when(t == 0)
    def _():
        for step in range(WEIGHT_RING - 1):
            w_copy(step).start()

    @pl.when(t + WEIGHT_RING - 1 <= last)
    def _():
        w_copy(t + WEIGHT_RING - 1).start()

    w_copy(t).wait()
    w_ref = w_ring.at[lax.rem(t, WEIGHT_RING)]

    @pl.when(j == 0)
    def _():
        h_ref[...] = _rms_bf16(x_ref[...], g_ref[...])
        wkr = wkr_ref[...]
        z = jnp.zeros((ROPE_HALF, wkr.shape[1]), F32)
        slot = jnp.concatenate([wkr[:ROPE_HALF], z, wkr[ROPE_HALF:], z], axis=0)
        kr_ref[...] = _dot_t(h_ref[...], slot)

    @pl.when(j < n_lat)
    def _():
        lat_ref[...] = _rms_bf16(_dot_t(h_ref[...], w_ref[...]), gl_ref[...])

    @pl.when(jnp.logical_and(j >= n_lat, j < n_lat + n_up))
    def _():
        up_ref[...] = _dot_t(h_ref[...], w_ref[...])

    @pl.when(j >= n_lat + n_up)
    def _():
        gate_ref[...] = _sigmoid(_dot_t(h_ref[...], w_ref[...]) + b_ref[...])


def _inproj_call(x, g, w_t, g_lat, b_gate, pool_width, tm):
    m, d = x.shape
    tn = Q_LORA_RANK
    n_lat = g_lat.shape[1] // tn
    n_up = pool_width // tn
    n_gate = b_gate.shape[1] // tn
    o_kr = n_lat * tn
    o_up = o_kr + QK_ROPE_DIM
    clamp = lambda j, lo, n: jnp.clip(j - lo, 0, n - 1)
    sub = SUBLANES
    w_row = lambda j: sub * jnp.where(j < n_lat, j * (tn // sub),
                                      o_up // sub + (j - n_lat) * (tn // sub))
    return pl.pallas_call(
        functools.partial(_inproj_kernel, n_lat=n_lat, n_up=n_up, w_row=w_row, tn=tn),
        grid=(m // tm, n_lat + n_up + n_gate),
        in_specs=[
            pl.BlockSpec((tm, d), lambda i, j: (i, 0)),
            pl.BlockSpec((1, d), lambda i, j: (0, 0)),
            pl.BlockSpec(memory_space=pl.ANY),
            pl.BlockSpec((1, tn), lambda i, j: (0, clamp(j, 0, n_lat))),
            pl.BlockSpec((1, tn), lambda i, j: (0, clamp(j, n_lat + n_up, n_gate))),
            pl.BlockSpec((pl.Element(QK_ROPE_DIM), pl.Element(d)), lambda i, j: (o_kr, 0)),
        ],
        out_specs=[
            pl.BlockSpec((tm, tn), lambda i, j: (i, clamp(j, 0, n_lat))),
            pl.BlockSpec((tm, tn), lambda i, j: (i, clamp(j, n_lat, n_up))),
            pl.BlockSpec((tm, tn), lambda i, j: (i, clamp(j, n_lat + n_up, n_gate))),
            pl.BlockSpec((tm, LANES), lambda i, j: (i, 0)),
        ],
        out_shape=[
            jax.ShapeDtypeStruct((m, n_lat * tn), BF16),
            jax.ShapeDtypeStruct((m, n_up * tn), F32),
            jax.ShapeDtypeStruct((m, n_gate * tn), F32),
            jax.ShapeDtypeStruct((m, LANES), F32),
        ],
        scratch_shapes=[pltpu.VMEM((tm, d), BF16), pltpu.VMEM((WEIGHT_RING, tn, d), F32),
                        pltpu.SemaphoreType.DMA((WEIGHT_RING,))],
        compiler_params=_params("arbitrary", "arbitrary"),
        name="inproj",
    )(x, g, w_t, g_lat, b_gate, w_t)


def _pool_mixer(u, halo, t0, wg_ref, sc_ref):
    tm = u.shape[0]
    halo = jnp.where(t0 > 0, halo, 0.0)
    ext = jnp.concatenate([halo, u], axis=0)
    pos = lax.broadcasted_iota(jnp.int32, (tm, 1), 0) + t0
    gd = wg_ref.shape[1]
    out = []
    for g, w in enumerate(POOL_WINDOWS):
        cols = slice(g * gd, (g + 1) * gd)
        a = ext[:, cols]
        shift = 1
        while shift < w:
            a = a + pltpu.roll(a, shift, 0)
            shift *= 2
        count = jnp.minimum(pos + 1, w).astype(F32)
        pooled = a[POOL_HALO:, :] / count - u[:, cols]
        y = _dot(pooled.astype(BF16), wg_ref[g]) * sc_ref[:, cols]
        out.append(y.astype(BF16))
    return jnp.concatenate(out, axis=1)


def _qkv_kernel(cq_ref, ckv_ref, kr_ref, cos_ref, sin_ref, wqt_ref, wk_ref, wvt_ref,
                gq_ref, gk_ref, qt_ref, k_ref, vt_ref):
    inv_dim = 1.0 / QK_HEAD_DIM
    half = ROPE_HALF
    cq = cq_ref[...]
    ckv = ckv_ref[...]
    cos_t = cos_ref[...]
    sin_t = sin_ref[...]
    qt = lax.dot_general(wqt_ref[...], cq, _CONTRACT_LAST, preferred_element_type=F32)
    vt_ref[...] = lax.dot_general(wvt_ref[...], ckv, _CONTRACT_LAST,
                                  preferred_element_type=F32).astype(BF16)
    gq = gq_ref[...]
    zeros = jnp.zeros((half, qt.shape[1]), BF16)
    for h in range(N_HEADS):
        src = h * QK_HEAD_DIM
        dst = h * HEAD_SLOT
        qh = qt[src:src + QK_HEAD_DIM]
        inv = lax.rsqrt(jnp.sum(qh * qh, axis=0, keepdims=True) * inv_dim + EPS)
        qh = qh * inv * gq
        t1 = qh[QK_NOPE_DIM:QK_NOPE_DIM + half]
        t2 = qh[QK_NOPE_DIM + half:]
        qt_ref[dst:dst + QK_NOPE_DIM] = qh[:QK_NOPE_DIM].astype(BF16)
        qt_ref[dst + LANES:dst + LANES + half] = (t1 * cos_t - t2 * sin_t).astype(BF16)
        qt_ref[dst + LANES + half:dst + LANES + 2 * half] = zeros
        qt_ref[dst + LANES + 2 * half:dst + LANES + 3 * half] = (t2 * cos_t + t1 * sin_t).astype(BF16)
        qt_ref[dst + LANES + 3 * half:dst + HEAD_SLOT] = zeros

    cos_r = cos_t.T
    sin_r = sin_t.T
    cos = jnp.concatenate([cos_r] * 4, axis=1)
    sin = jnp.concatenate([-sin_r, -sin_r, sin_r, sin_r], axis=1)
    kv = _dot(ckv, wk_ref[...])
    gk_n, gk_r = gk_ref[:, :LANES], gk_ref[:, LANES:]
    kr = kr_ref[...]
    ss_kr = jnp.sum(kr * kr, axis=-1, keepdims=True)
    kr_rot = _rope(kr * gk_r, cos, sin)
    for h in range(N_HEADS):
        lo = h * HEAD_SLOT
        kn = kv[:, h * LANES:(h + 1) * LANES]
        ssk = jnp.sum(kn * kn, axis=-1, keepdims=True) + ss_kr
        invk = lax.rsqrt(ssk * inv_dim + EPS)
        k_ref[:, lo:lo + LANES] = (kn * invk * gk_n).astype(BF16)
        k_ref[:, lo + LANES:lo + HEAD_SLOT] = (kr_rot * invk).astype(BF16)


def _qkv_call(lat, kr, cos_t, sin_t, wqt, wk, wvt, gq, gk, tm):
    m = lat.shape[0]
    rank = Q_LORA_RANK
    wide = N_HEADS * HEAD_SLOT
    vw = N_HEADS * V_HEAD_DIM
    row = lambda i: (i, 0)
    col = lambda i: (0, i)
    fixed = lambda i: (0, 0)
    return pl.pallas_call(
        _qkv_kernel,
        grid=(m // tm,),
        in_specs=[
            pl.BlockSpec((tm, rank), lambda i: (i, 0)),
            pl.BlockSpec((tm, rank), lambda i: (i, 1)),
            pl.BlockSpec((tm, LANES), row),
            pl.BlockSpec((ROPE_HALF, tm), col),
            pl.BlockSpec((ROPE_HALF, tm), col),
            pl.BlockSpec(wqt.shape, fixed),
            pl.BlockSpec(wk.shape, fixed),
            pl.BlockSpec(wvt.shape, fixed),
            pl.BlockSpec(gq.shape, fixed),
            pl.BlockSpec((1, HEAD_SLOT), fixed),
        ],
        out_specs=[
            pl.BlockSpec((None, wide, tm), lambda i: (i, 0, 0)),
            pl.BlockSpec((tm, wide), row),
            pl.BlockSpec((None, vw, tm), lambda i: (i, 0, 0)),
        ],
        out_shape=[
            jax.ShapeDtypeStruct((m // tm, wide, tm), BF16),
            jax.ShapeDtypeStruct((m, wide), BF16),
            jax.ShapeDtypeStruct((m // tm, vw, tm), BF16),
        ],
        compiler_params=_params("parallel"),
        name="qkv",
    )(lat, lat, kr, cos_t, sin_t, wqt, wk, wvt, gq, gk)


def _flash_kernel(q_ref, k_ref, vt_ref, *refs, tq, heads, n_cast):
    cast_src, (o_ref, *cast_dst), (m_ref, acc_ref, s_ref, mx_ref) = (
        refs[:n_cast], refs[n_cast:2 * n_cast + 1], refs[2 * n_cast + 1:])
    for src, dst in zip(cast_src, cast_dst):
        dst[...] = src[...].astype(BF16)
    n = pl.program_id(2)
    ones = jnp.ones((SUM_ROWS, tq), BF16)
    m_ref[...] = jnp.full(m_ref.shape, -jnp.inf, F32)
    acc_ref[...] = jnp.zeros(acc_ref.shape, F32)

    half = tq // 2
    head_cols = lambda h: slice(h * HEAD_SLOT, (h + 1) * HEAD_SLOT)

    def v_rows(h, c):
        vt = vt_ref[c, h * V_HEAD_DIM:(h + 1) * V_HEAD_DIM, :]
        return jnp.concatenate([vt, ones], axis=0)

    def new_max(h, buf):
        m_prev = m_ref[h]
        m_new = jnp.maximum(m_prev, mx_ref[buf, h])
        m_ref[h] = m_new
        return m_new, jnp.exp2(m_prev - m_new)

    def scores(h, c, buf):
        start = pl.multiple_of(c * tq, tq)
        s = _dot(k_ref[pl.ds(start, tq), head_cols(h)], q_ref[head_cols(h), :])
        s_ref[buf, h] = s
        mx_ref[buf, h] = jnp.max(s, axis=0, keepdims=True)

    def accumulate(h, c, buf):
        m_new, alpha = new_max(h, buf)
        p = jnp.exp2(s_ref[buf, h] - m_new)
        acc_ref[h] = alpha * acc_ref[h] + _dot(v_rows(h, c), p.astype(BF16))

    def scores_diagonal(h, c, buf):
        start = pl.multiple_of(c * tq, tq)
        causal = (lax.broadcasted_iota(jnp.int32, (half, tq), 0)
                  <= lax.broadcasted_iota(jnp.int32, (half, tq), 1))
        q_t = q_ref[head_cols(h), :]
        s0 = _dot(k_ref[pl.ds(start, half), head_cols(h)], q_t)
        s1 = _dot(k_ref[pl.ds(start + half, half), head_cols(h)], q_t[:, half:])
        s0 = jnp.where(causal, s0, -jnp.inf)
        s1 = jnp.where(causal[:, :half], s1, -jnp.inf)
        s_ref[buf, h, :half, :] = s0
        s_ref[buf, h, half:, half:] = s1
        m0 = jnp.max(s0, axis=0, keepdims=True)
        m1 = jnp.max(s1, axis=0, keepdims=True)
        mx_ref[buf, h] = jnp.concatenate([m0[:, :half], jnp.maximum(m0[:, half:], m1)], axis=1)

    def accumulate_diagonal(h, c, buf):
        m_new, alpha = new_max(h, buf)
        p0 = jnp.exp2(s_ref[buf, h, :half, :] - m_new).astype(BF16)
        p1 = jnp.exp2(s_ref[buf, h, half:, half:] - m_new[:, half:]).astype(BF16)
        vt = v_rows(h, c)
        acc = alpha * acc_ref[h] + _dot(vt[:, :half], p0)
        acc_ref[h, :, :half] = acc[:, :half]
        acc_ref[h, :, half:] = acc[:, half:] + _dot(vt[:, half:], p1)

    def stage(c, buf, next_diagonal):
        for h in range(heads):
            (scores_diagonal if next_diagonal else scores)(h, c + 1, 1 - buf)
            accumulate(h, c, buf)

    def drain(c, buf):
        for h in range(heads):
            accumulate_diagonal(h, c, buf)

    @pl.when(n == 0)
    def _():
        for h in range(heads):
            scores_diagonal(h, 0, 0)
        drain(0, 0)

    @pl.when(n > 0)
    def _():
        for h in range(heads):
            scores(h, 0, 0)

    def two_stages(i, carry):
        stage(2 * i, 0, False)
        stage(2 * i + 1, 1, False)
        return carry

    lax.fori_loop(0, lax.div(n - 1, 2), two_stages, None)

    @pl.when(n % 2 == 1)
    def _():
        stage(n - 1, 0, True)
        drain(n, 1)

    @pl.when(jnp.logical_and(n >= 2, n % 2 == 0))
    def _():
        stage(n - 2, 0, False)
        stage(n - 1, 1, True)
        drain(n, 0)

    for h in range(heads):
        out = acc_ref[h, :V_HEAD_DIM, :] / acc_ref[h, V_HEAD_DIM:V_HEAD_DIM + 1, :]
        o_ref[:, h * V_HEAD_DIM:(h + 1) * V_HEAD_DIM] = out.T.astype(BF16)


def _flash_call(q, k, vt, cast_weights, tq, heads):
    b, s, _ = k.shape
    grid = (b, N_HEADS // heads, s // tq)
    n_steps = grid[0] * grid[1] * grid[2]
    step = lambda bi, h, i: ((bi * grid[1] + h) * grid[2] + i, 0)
    cast_specs = [pl.BlockSpec((w.shape[0] // n_steps, w.shape[1]), step) for w in cast_weights]
    return pl.pallas_call(
        functools.partial(_flash_kernel, tq=tq, heads=heads, n_cast=len(cast_weights)),
        grid=grid,
        in_specs=[
            pl.BlockSpec((None, None, heads * HEAD_SLOT, tq), lambda bi, h, i: (bi, i, h, 0)),
            pl.BlockSpec((None, s, heads * HEAD_SLOT), lambda bi, h, i: (bi, 0, h)),
            pl.BlockSpec((None, s // tq, heads * V_HEAD_DIM, tq), lambda bi, h, i: (bi, 0, h, 0)),
        ] + cast_specs,
        out_specs=[pl.BlockSpec((None, tq, heads * V_HEAD_DIM), lambda bi, h, i: (bi, i, h))] + cast_specs,
        out_shape=[jax.ShapeDtypeStruct((b, s, N_HEADS * V_HEAD_DIM), BF16)]
        + [jax.ShapeDtypeStruct(w.shape, BF16) for w in cast_weights],
        scratch_shapes=[
            pltpu.VMEM((heads, 1, tq), F32),
            pltpu.VMEM((heads, V_HEAD_DIM + SUM_ROWS, tq), F32),
            pltpu.VMEM((2, heads, tq, tq), F32),
            pltpu.VMEM((2, heads, 1, tq), F32),
        ],
        compiler_params=_params("parallel", "parallel", "arbitrary"),
        name="flash",
    )(q, k, vt, *cast_weights)


def _mix_kernel(x_ref, u_ref, halo_ref, attn_ref, gp_ref, ga_ref, wg_ref, sc_ref,
                wpo_ref, wao_ref, wout_ref, g2_ref, o_ref, h2_ref, *, tm, seq):
    t0 = (pl.program_id(0) * tm) % seq
    pooled = _pool_mixer(u_ref[...], halo_ref[...], t0, wg_ref, sc_ref)
    y_pool = _dot(pooled, wpo_ref[...])
    y_attn = _dot(attn_ref[...], wao_ref[...])
    mixed = gp_ref[...] * y_pool + ga_ref[...] * y_attn
    x1 = x_ref[...] + _dot(mixed.astype(BF16), wout_ref[...])
    o_ref[...] = x1
    h2_ref[...] = _rms_bf16(x1, g2_ref[...])


def _mix_call(x, u_pool, attn, gates, w_grp, scale, w_po, w_ao, w_out, g_ffn, tm, seq):
    m, d = x.shape
    width = u_pool.shape[1]
    halo_blocks = tm // POOL_HALO
    row = lambda i: (i, 0)
    fixed = lambda i: (0, 0)
    return pl.pallas_call(
        functools.partial(_mix_kernel, tm=tm, seq=seq),
        grid=(m // tm,),
        in_specs=[
            pl.BlockSpec((tm, d), row),
            pl.BlockSpec((tm, width), row),
            pl.BlockSpec((POOL_HALO, width), lambda i: (jnp.maximum(i * halo_blocks - 1, 0), 0)),
            pl.BlockSpec((tm, attn.shape[1]), row),
            pl.BlockSpec((tm, d), lambda i: (i, 0)),
            pl.BlockSpec((tm, d), lambda i: (i, 1)),
            pl.BlockSpec(w_grp.shape, lambda i: (0, 0, 0)),
            pl.BlockSpec((1, width), fixed),
            pl.BlockSpec(w_po.shape, fixed),
            pl.BlockSpec(w_ao.shape, fixed),
            pl.BlockSpec(w_out.shape, fixed),
            pl.BlockSpec((1, d), fixed),
        ],
        out_specs=[pl.BlockSpec((tm, d), row), pl.BlockSpec((tm, d), row)],
        out_shape=[jax.ShapeDtypeStruct((m, d), F32), jax.ShapeDtypeStruct((m, d), BF16)],
        compiler_params=_params("parallel"),
        name="mix",
    )(x, u_pool, u_pool, attn, gates, gates, w_grp, scale, w_po, w_ao, w_out, g_ffn)


def _ffn_kernel(h_ref, x_ref, wg_ref, wu_ref, wd_ref, o_ref, *, res_steps):
    j = pl.program_id(1)

    def step(first):
        h = h_ref[...]
        a = _dot(h, wg_ref[...].astype(BF16))
        u = _dot(h, wu_ref[...].astype(BF16))
        act = _silu(a) * u
        down = _dot(act.astype(BF16), wd_ref[...].astype(BF16))
        if first:
            o_ref[...] = down
        else:
            o_ref[...] += down

    pl.when(j == 0)(lambda: step(True))
    pl.when(j > 0)(lambda: step(False))

    slab = x_ref.shape[1]
    for c in range(res_steps):
        @pl.when(j == c)
        def _():
            o_ref[:, c * slab:(c + 1) * slab] += x_ref[...]


def _ffn_call(h, x, w_gate, w_up, w_down, tm, tf):
    m, d = x.shape
    f = w_gate.shape[1]
    res_steps = FFN_RESIDUAL_STEPS
    assert f // tf >= res_steps
    return pl.pallas_call(
        functools.partial(_ffn_kernel, res_steps=res_steps),
        grid=(m // tm, f // tf),
        in_specs=[
            pl.BlockSpec((tm, d), lambda i, j: (i, 0)),
            pl.BlockSpec((tm, d // res_steps), lambda i, j: (i, jnp.minimum(j, res_steps - 1))),
            pl.BlockSpec((d, tf), lambda i, j: (0, j)),
            pl.BlockSpec((d, tf), lambda i, j: (0, j)),
            pl.BlockSpec((tf, d), lambda i, j: (j, 0)),
        ],
        out_specs=pl.BlockSpec((tm, d), lambda i, j: (i, 0)),
        out_shape=jax.ShapeDtypeStruct((m, d), F32),
        compiler_params=pltpu.CompilerParams(dimension_semantics=("parallel", "arbitrary"),
                                             vmem_limit_bytes=FFN_VMEM_LIMIT),
        name="ffn",
    )(h, x, w_gate, w_up, w_down)


def _rope_slot(t):
    z = jnp.zeros(t.shape[:-1] + (ROPE_HALF,), t.dtype)
    return jnp.concatenate([t[..., :ROPE_HALF], z, t[..., ROPE_HALF:], z], axis=-1)


def _head_slot(t):
    return jnp.concatenate([t[..., :QK_NOPE_DIM], _rope_slot(t[..., QK_NOPE_DIM:])], axis=-1)


def kernel(x, positions, attn_norm_g, w_in, b_gate, q_a_norm_g, w_q_b, kv_a_norm_g, w_kv_b,
           q_norm_g, k_norm_g, w_attn_o, w_pool_grp, pool_scale, w_pool_o, w_out,
           ffn_norm_g, w_ffn_gate, w_ffn_up, w_ffn_down):
    b, s, d = x.shape
    depth = w_in.shape[0]
    m = b * s
    pool_width = w_pool_o.shape[1]

    inv_freq = ROPE_THETA ** (-jnp.arange(ROPE_HALF, dtype=F32) / ROPE_HALF)
    ang_t = positions.astype(F32).reshape(1, m) * inv_freq.reshape(ROPE_HALF, 1)
    cos_t = jnp.cos(ang_t)
    sin_t = jnp.sin(ang_t)

    xf = x.reshape(m, d)
    for l in range(depth):
        g_lat = jnp.concatenate([q_a_norm_g[l], kv_a_norm_g[l]]).reshape(1, -1)
        wqt = w_q_b[l].T.astype(BF16)
        wkv = w_kv_b[l].reshape(KV_LORA_RANK, N_HEADS, QK_NOPE_DIM + V_HEAD_DIM)
        wk = wkv[..., :QK_NOPE_DIM].reshape(KV_LORA_RANK, -1).astype(BF16)
        wvt = wkv[..., QK_NOPE_DIM:].reshape(KV_LORA_RANK, -1).T.astype(BF16)
        tq = ATTN_TILE
        gq = jnp.broadcast_to((q_norm_g[l] * (QK_HEAD_DIM ** -0.5 * LOG2_E)).reshape(-1, 1),
                              (QK_HEAD_DIM, tq))
        gk = _head_slot(k_norm_g[l]).reshape(1, HEAD_SLOT)
        g_attn_norm = attn_norm_g[l].reshape(1, d)

        lat, u_pool, gates, kr = _inproj_call(xf, g_attn_norm, w_in[l].T, g_lat,
                                              b_gate[l].reshape(1, -1), pool_width, tm=INPROJ_ROWS)
        qt, k, vt = _qkv_call(lat, kr, cos_t, sin_t, wqt, wk, wvt, gq, gk, tm=tq)
        attn, w_po, w_ao, w_o = _flash_call(qt.reshape(b, s // tq, -1, tq), k.reshape(b, s, -1),
                                            vt.reshape(b, s // tq, -1, tq),
                                            [w_pool_o[l], w_attn_o[l], w_out[l]],
                                            tq=tq, heads=ATTN_HEADS_PER_STEP)
        xf, h2 = _mix_call(xf, u_pool, attn.reshape(m, -1), gates, w_pool_grp[l].astype(BF16),
                           pool_scale[l].reshape(1, -1), w_po, w_ao, w_o,
                           ffn_norm_g[l].reshape(1, d), tm=MIX_ROWS, seq=s)
        xf = _ffn_call(h2, xf, w_ffn_gate[l], w_ffn_up[l], w_ffn_down[l], tm=FFN_ROWS, tf=FFN_COLS)
    return xf.reshape(b, s, d)
```

```python
import functools

import jax
import jax.numpy as jnp
from jax import lax
from jax.experimental import pallas as pl
from jax.experimental.pallas import tpu as pltpu

F32 = jnp.float32
BF16 = jnp.bfloat16

N_HEADS = 16
QK_NOPE_DIM = 128
QK_ROPE_DIM = 64
QK_HEAD_DIM = QK_NOPE_DIM + QK_ROPE_DIM
V_HEAD_DIM = 128
Q_LORA_RANK = 512
KV_LORA_RANK = 512
ROPE_THETA = 10000.0
POOL_WINDOWS = (2, 4, 8, 16)
EPS = 1e-6

LANES = 128
SUBLANES = 8
HEAD_SLOT = 2 * LANES
ROPE_HALF = QK_ROPE_DIM // 2
POOL_HALO = 16
VMEM_LIMIT = 56 * 1024 * 1024
FFN_VMEM_LIMIT = 62 * 1024 * 1024
FFN_RESIDUAL_STEPS = 8

INPROJ_ROWS = 1024
ATTN_TILE = 512
ATTN_HEADS_PER_STEP = 4
MIX_ROWS = 256
FFN_ROWS, FFN_COLS = 1024, 512
WEIGHT_RING = 3
SUM_ROWS = 16
LOG2_E = 1.4426950408889634
_CONTRACT_LAST = (((1,), (1,)), ((), ()))


def _params(*sem):
    return pltpu.CompilerParams(dimension_semantics=sem, vmem_limit_bytes=VMEM_LIMIT)


def _dot(a, b):
    return jnp.dot(a, b, preferred_element_type=F32)


def _sigmoid(x):
    return 0.5 * jnp.tanh(0.5 * x) + 0.5


def _silu(x):
    t = 0.5 * x
    return t + t * jnp.tanh(t)


def _rms_bf16(x, g):
    ms = jnp.mean(x * x, axis=-1, keepdims=True)
    return (x * lax.rsqrt(ms + EPS) * g).astype(BF16)


def _rope(t, cos, sin):
    return t * cos + pltpu.roll(t, 2 * ROPE_HALF, 1) * sin


def _dot_t(a, w):
    return lax.dot_general(a, w.astype(BF16), _CONTRACT_LAST, preferred_element_type=F32)


def _inproj_kernel(x_hbm, g_ref, w_hbm, gl_ref, b_ref, wkr_ref,
                   lat_ref, up_ref, gate_ref, kr_ref, h_ref, w_ring, w_sem, x_buf, x_sem,
                   *, n_lat, n_up, w_row, tn, tm):
    i = pl.program_id(0)
    j = pl.program_id(1)
    n_i = pl.num_programs(0)
    n_j = pl.num_programs(1)
    t = i * n_j + j
    last = n_i * n_j - 1

    def w_copy(step):
        slot = lax.rem(step, WEIGHT_RING)
        rows = pl.ds(w_row(lax.rem(step, n_j)), tn)
        return pltpu.make_async_copy(w_hbm.at[rows, :], w_ring.at[slot], w_sem.at[slot])

    def x_copy(tile):
        slot = lax.rem(tile, 2)
        rows = pl.ds(pl.multiple_of(tile * tm, tm), tm)
        return pltpu.make_async_copy(x_hbm.at[rows, :], x_buf.at[slot], x_sem.at[slot])

    @pl.when(t == 0)
    def _():
        x_copy(0).start()
        for step in range(WEIGHT_RING - 1):
            w_copy(step).start()

    @pl.when(t + WEIGHT_RING - 1 <= last)
    def _():
        w_copy(t + WEIGHT_RING - 1).start()

    @pl.when(jnp.logical_and(j == 1, i + 1 < n_i))
    def _():
        x_copy(i + 1).start()

    w_copy(t).wait()
    w_ref = w_ring.at[lax.rem(t, WEIGHT_RING)]

    @pl.when(j == 0)
    def _():
        x_copy(i).wait()
        h_ref[...] = _rms_bf16(x_buf[lax.rem(i, 2)], g_ref[...])
        wkr = wkr_ref[...]
        z = jnp.zeros((ROPE_HALF, wkr.shape[1]), F32)
        slot = jnp.concatenate([wkr[:ROPE_HALF], z, wkr[ROPE_HALF:], z], axis=0)
        kr_ref[...] = _dot_t(h_ref[...], slot)

    @pl.when(j < n_lat)
    def _():
        lat_ref[...] = _rms_bf16(_dot_t(h_ref[...], w_ref[...]), gl_ref[...])

    @pl.when(jnp.logical_and(j >= n_lat, j < n_lat + n_up))
    def _():
        up_ref[...] = _dot_t(h_ref[...], w_ref[...])

    @pl.when(j >= n_lat + n_up)
    def _():
        gate_ref[...] = _sigmoid(_dot_t(h_ref[...], w_ref[...]) + b_ref[...])


def _inproj_call(x, g, w_t, g_lat, b_gate, pool_width, tm):
    m, d = x.shape
    tn = Q_LORA_RANK
    n_lat = g_lat.shape[1] // tn
    n_up = pool_width // tn
    n_gate = b_gate.shape[1] // tn
    o_kr = n_lat * tn
    o_up = o_kr + QK_ROPE_DIM
    clamp = lambda j, lo, n: jnp.clip(j - lo, 0, n - 1)
    sub = SUBLANES
    w_row = lambda j: sub * jnp.where(j < n_lat, j * (tn // sub),
                                      o_up // sub + (j - n_lat) * (tn // sub))
    return pl.pallas_call(
        functools.partial(_inproj_kernel, n_lat=n_lat, n_up=n_up, w_row=w_row, tn=tn, tm=tm),
        grid=(m // tm, n_lat + n_up + n_gate),
        in_specs=[
            pl.BlockSpec(memory_space=pl.ANY),
            pl.BlockSpec((1, d), lambda i, j: (0, 0)),
            pl.BlockSpec(memory_space=pl.ANY),
            pl.BlockSpec((1, tn), lambda i, j: (0, clamp(j, 0, n_lat))),
            pl.BlockSpec((1, tn), lambda i, j: (0, clamp(j, n_lat + n_up, n_gate))),
            pl.BlockSpec((pl.Element(QK_ROPE_DIM), pl.Element(d)), lambda i, j: (o_kr, 0)),
        ],
        out_specs=[
            pl.BlockSpec((tm, tn), lambda i, j: (i, clamp(j, 0, n_lat))),
            pl.BlockSpec((tm, tn), lambda i, j: (i, clamp(j, n_lat, n_up))),
            pl.BlockSpec((tm, tn), lambda i, j: (i, clamp(j, n_lat + n_up, n_gate))),
            pl.BlockSpec((tm, LANES), lambda i, j: (i, 0)),
        ],
        out_shape=[
            jax.ShapeDtypeStruct((m, n_lat * tn), BF16),
            jax.ShapeDtypeStruct((m, n_up * tn), F32),
            jax.ShapeDtypeStruct((m, n_gate * tn), F32),
            jax.ShapeDtypeStruct((m, LANES), F32),
        ],
        scratch_shapes=[pltpu.VMEM((tm, d), BF16), pltpu.VMEM((WEIGHT_RING, tn, d), F32),
                        pltpu.SemaphoreType.DMA((WEIGHT_RING,)),
                        pltpu.VMEM((2, tm, d), F32), pltpu.SemaphoreType.DMA((2,))],
        compiler_params=_params("arbitrary", "arbitrary"),
        name="inproj",
    )(x, g, w_t, g_lat, b_gate, w_t)


def _pool_mixer(u, halo, t0, wg_ref, sc_ref):
    tm = u.shape[0]
    halo = jnp.where(t0 > 0, halo, 0.0)
    ext = jnp.concatenate([halo, u], axis=0)
    pos = lax.broadcasted_iota(jnp.int32, (tm, 1), 0) + t0
    gd = wg_ref.shape[1]
    out = []
    for g, w in enumerate(POOL_WINDOWS):
        cols = slice(g * gd, (g + 1) * gd)
        a = ext[:, cols]
        shift = 1
        while shift < w:
            a = a + pltpu.roll(a, shift, 0)
            shift *= 2
        count = jnp.minimum(pos + 1, w).astype(F32)
        pooled = a[POOL_HALO:, :] / count - u[:, cols]
        y = _dot(pooled.astype(BF16), wg_ref[g]) * sc_ref[:, cols]
        out.append(y.astype(BF16))
    return jnp.concatenate(out, axis=1)


def _qkv_kernel(cq_ref, ckv_ref, kr_ref, cos_ref, sin_ref, wqt_ref, wk_ref, wvt_ref,
                gq_ref, gk_ref, qt_ref, k_ref, vt_ref):
    inv_dim = 1.0 / QK_HEAD_DIM
    half = ROPE_HALF
    cq = cq_ref[...]
    ckv = ckv_ref[...]
    cos_t = cos_ref[...]
    sin_t = sin_ref[...]
    qt = lax.dot_general(wqt_ref[...], cq, _CONTRACT_LAST, preferred_element_type=F32)
    vt_ref[...] = lax.dot_general(wvt_ref[...], ckv, _CONTRACT_LAST,
                                  preferred_element_type=F32).astype(BF16)
    gq = gq_ref[...]
    zeros = jnp.zeros((half, qt.shape[1]), BF16)
    for h in range(N_HEADS):
        src = h * QK_HEAD_DIM
        dst = h * HEAD_SLOT
        qh = qt[src:src + QK_HEAD_DIM]
        inv = lax.rsqrt(jnp.sum(qh * qh, axis=0, keepdims=True) * inv_dim + EPS)
        qh = qh * inv * gq
        t1 = qh[QK_NOPE_DIM:QK_NOPE_DIM + half]
        t2 = qh[QK_NOPE_DIM + half:]
        qt_ref[dst:dst + QK_NOPE_DIM] = qh[:QK_NOPE_DIM].astype(BF16)
        qt_ref[dst + LANES:dst + LANES + half] = (t1 * cos_t - t2 * sin_t).astype(BF16)
        qt_ref[dst + LANES + half:dst + LANES + 2 * half] = zeros
        qt_ref[dst + LANES + 2 * half:dst + LANES + 3 * half] = (t2 * cos_t + t1 * sin_t).astype(BF16)
        qt_ref[dst + LANES + 3 * half:dst + HEAD_SLOT] = zeros

    cos_r = cos_t.T
    sin_r = sin_t.T
    cos = jnp.concatenate([cos_r] * 4, axis=1)
    sin = jnp.concatenate([-sin_r, -sin_r, sin_r, sin_r], axis=1)
    kv = _dot(ckv, wk_ref[...])
    gk_n, gk_r = gk_ref[:, :LANES], gk_ref[:, LANES:]
    kr = kr_ref[...]
    ss_kr = jnp.sum(kr * kr, axis=-1, keepdims=True)
    kr_rot = _rope(kr * gk_r, cos, sin)
    for h in range(N_HEADS):
        lo = h * HEAD_SLOT
        kn = kv[:, h * LANES:(h + 1) * LANES]
        ssk = jnp.sum(kn * kn, axis=-1, keepdims=True) + ss_kr
        invk = lax.rsqrt(ssk * inv_dim + EPS)
        k_ref[:, lo:lo + LANES] = (kn * invk * gk_n).astype(BF16)
        k_ref[:, lo + LANES:lo + HEAD_SLOT] = (kr_rot * invk).astype(BF16)


def _qkv_call(lat, kr, cos_t, sin_t, wqt, wk, wvt, gq, gk, tm):
    m = lat.shape[0]
    rank = Q_LORA_RANK
    wide = N_HEADS * HEAD_SLOT
    vw = N_HEADS * V_HEAD_DIM
    row = lambda i: (i, 0)
    col = lambda i: (0, i)
    fixed = lambda i: (0, 0)
    return pl.pallas_call(
        _qkv_kernel,
        grid=(m // tm,),
        in_specs=[
            pl.BlockSpec((tm, rank), lambda i: (i, 0)),
            pl.BlockSpec((tm, rank), lambda i: (i, 1)),
            pl.BlockSpec((tm, LANES), row),
            pl.BlockSpec((ROPE_HALF, tm), col),
            pl.BlockSpec((ROPE_HALF, tm), col),
            pl.BlockSpec(wqt.shape, fixed),
            pl.BlockSpec(wk.shape, fixed),
            pl.BlockSpec(wvt.shape, fixed),
            pl.BlockSpec(gq.shape, fixed),
            pl.BlockSpec((1, HEAD_SLOT), fixed),
        ],
        out_specs=[
            pl.BlockSpec((None, wide, tm), lambda i: (i, 0, 0)),
            pl.BlockSpec((tm, wide), row),
            pl.BlockSpec((None, vw, tm), lambda i: (i, 0, 0)),
        ],
        out_shape=[
            jax.ShapeDtypeStruct((m // tm, wide, tm), BF16),
            jax.ShapeDtypeStruct((m, wide), BF16),
            jax.ShapeDtypeStruct((m // tm, vw, tm), BF16),
        ],
        compiler_params=_params("parallel"),
        name="qkv",
    )(lat, lat, kr, cos_t, sin_t, wqt, wk, wvt, gq, gk)


def _flash_kernel(q_ref, k_ref, vt_ref, *refs, tq, heads, n_cast):
    cast_src, (o_ref, *cast_dst), (m_ref, acc_ref, s_ref, mx_ref) = (
        refs[:n_cast], refs[n_cast:2 * n_cast + 1], refs[2 * n_cast + 1:])
    for src, dst in zip(cast_src, cast_dst):
        dst[...] = src[...].astype(BF16)
    n = pl.program_id(2)
    ones = jnp.ones((SUM_ROWS, tq), BF16)
    m_ref[...] = jnp.full(m_ref.shape, -jnp.inf, F32)
    acc_ref[...] = jnp.zeros(acc_ref.shape, F32)

    half = tq // 2
    head_cols = lambda h: slice(h * HEAD_SLOT, (h + 1) * HEAD_SLOT)

    def v_rows(h, c):
        vt = vt_ref[c, h * V_HEAD_DIM:(h + 1) * V_HEAD_DIM, :]
        return jnp.concatenate([vt, ones], axis=0)

    def new_max(h, buf):
        m_prev = m_ref[h]
        m_new = jnp.maximum(m_prev, mx_ref[buf, h])
        m_ref[h] = m_new
        return m_new, jnp.exp2(m_prev - m_new)

    def scores(h, c, buf):
        start = pl.multiple_of(c * tq, tq)
        s = _dot(k_ref[pl.ds(start, tq), head_cols(h)], q_ref[head_cols(h), :])
        s_ref[buf, h] = s
        mx_ref[buf, h] = jnp.max(s, axis=0, keepdims=True)

    def accumulate(h, c, buf):
        m_new, alpha = new_max(h, buf)
        p = jnp.exp2(s_ref[buf, h] - m_new)
        acc_ref[h] = alpha * acc_ref[h] + _dot(v_rows(h, c), p.astype(BF16))

    def scores_diagonal(h, c, buf):
        start = pl.multiple_of(c * tq, tq)
        causal = (lax.broadcasted_iota(jnp.int32, (half, tq), 0)
                  <= lax.broadcasted_iota(jnp.int32, (half, tq), 1))
        q_t = q_ref[head_cols(h), :]
        s0 = _dot(k_ref[pl.ds(start, half), head_cols(h)], q_t)
        s1 = _dot(k_ref[pl.ds(start + half, half), head_cols(h)], q_t[:, half:])
        s0 = jnp.where(causal, s0, -jnp.inf)
        s1 = jnp.where(causal[:, :half], s1, -jnp.inf)
        s_ref[buf, h, :half, :] = s0
        s_ref[buf, h, half:, half:] = s1
        m0 = jnp.max(s0, axis=0, keepdims=True)
        m1 = jnp.max(s1, axis=0, keepdims=True)
        mx_ref[buf, h] = jnp.concatenate([m0[:, :half], jnp.maximum(m0[:, half:], m1)], axis=1)

    def accumulate_diagonal(h, c, buf):
        m_new, alpha = new_max(h, buf)
        p0 = jnp.exp2(s_ref[buf, h, :half, :] - m_new).astype(BF16)
        p1 = jnp.exp2(s_ref[buf, h, half:, half:] - m_new[:, half:]).astype(BF16)
        vt = v_rows(h, c)
        acc = alpha * acc_ref[h] + _dot(vt[:, :half], p0)
        acc_ref[h, :, :half] = acc[:, :half]
        acc_ref[h, :, half:] = acc[:, half:] + _dot(vt[:, half:], p1)

    def stage(c, buf, next_diagonal):
        for h in range(heads):
            (scores_diagonal if next_diagonal else scores)(h, c + 1, 1 - buf)
            accumulate(h, c, buf)

    def drain(c, buf):
        for h in range(heads):
            accumulate_diagonal(h, c, buf)

    @pl.when(n == 0)
    def _():
        for h in range(heads):
            scores_diagonal(h, 0, 0)
        drain(0, 0)

    @pl.when(n > 0)
    def _():
        for h in range(heads):
            scores(h, 0, 0)

    def two_stages(i, carry):
        stage(2 * i, 0, False)
        stage(2 * i + 1, 1, False)
        return carry

    lax.fori_loop(0, lax.div(n - 1, 2), two_stages, None)

    @pl.when(n % 2 == 1)
    def _():
        stage(n - 1, 0, True)
        drain(n, 1)

    @pl.when(jnp.logical_and(n >= 2, n % 2 == 0))
    def _():
        stage(n - 2, 0, False)
        stage(n - 1, 1, True)
        drain(n, 0)

    for h in range(heads):
        out = acc_ref[h, :V_HEAD_DIM, :] / acc_ref[h, V_HEAD_DIM:V_HEAD_DIM + 1, :]
        o_ref[:, h * V_HEAD_DIM:(h + 1) * V_HEAD_DIM] = out.T.astype(BF16)


def _flash_call(q, k, vt, cast_weights, tq, heads):
    b, s, _ = k.shape
    grid = (b, N_HEADS // heads, s // tq)
    n_steps = grid[0] * grid[1] * grid[2]
    step = lambda bi, h, i: ((bi * grid[1] + h) * grid[2] + i, 0)
    cast_specs = [pl.BlockSpec((w.shape[0] // n_steps, w.shape[1]), step) for w in cast_weights]
    return pl.pallas_call(
        functools.partial(_flash_kernel, tq=tq, heads=heads, n_cast=len(cast_weights)),
        grid=grid,
        in_specs=[
            pl.BlockSpec((None, None, heads * HEAD_SLOT, tq), lambda bi, h, i: (bi, i, h, 0)),
            pl.BlockSpec((None, s, heads * HEAD_SLOT), lambda bi, h, i: (bi, 0, h)),
            pl.BlockSpec((None, s // tq, heads * V_HEAD_DIM, tq), lambda bi, h, i: (bi, 0, h, 0)),
        ] + cast_specs,
        out_specs=[pl.BlockSpec((None, tq, heads * V_HEAD_DIM), lambda bi, h, i: (bi, i, h))] + cast_specs,
        out_shape=[jax.ShapeDtypeStruct((b, s, N_HEADS * V_HEAD_DIM), BF16)]
        + [jax.ShapeDtypeStruct(w.shape, BF16) for w in cast_weights],
        scratch_shapes=[
            pltpu.VMEM((heads, 1, tq), F32),
            pltpu.VMEM((heads, V_HEAD_DIM + SUM_ROWS, tq), F32),
            pltpu.VMEM((2, heads, tq, tq), F32),
            pltpu.VMEM((2, heads, 1, tq), F32),
        ],
        compiler_params=_params("parallel", "parallel", "arbitrary"),
        name="flash",
    )(q, k, vt, *cast_weights)


def _mix_kernel(x_ref, u_ref, halo_ref, attn_ref, gp_ref, ga_ref, wg_ref, sc_ref,
                wpo_ref, wao_ref, wout_ref, g2_ref, o_ref, h2_ref, *, tm, seq):
    t0 = (pl.program_id(0) * tm) % seq
    pooled = _pool_mixer(u_ref[...], halo_ref[...], t0, wg_ref, sc_ref)
    y_pool = _dot(pooled, wpo_ref[...])
    y_attn = _dot(attn_ref[...], wao_ref[...])
    mixed = gp_ref[...] * y_pool + ga_ref[...] * y_attn
    x1 = x_ref[...] + _dot(mixed.astype(BF16), wout_ref[...])
    o_ref[...] = x1
    h2_ref[...] = _rms_bf16(x1, g2_ref[...])


def _mix_call(x, u_pool, attn, gates, w_grp, scale, w_po, w_ao, w_out, g_ffn, tm, seq):
    m, d = x.shape
    width = u_pool.shape[1]
    halo_blocks = tm // POOL_HALO
    row = lambda i: (i, 0)
    fixed = lambda i: (0, 0)
    return pl.pallas_call(
        functools.partial(_mix_kernel, tm=tm, seq=seq),
        grid=(m // tm,),
        in_specs=[
            pl.BlockSpec((tm, d), row),
            pl.BlockSpec((tm, width), row),
            pl.BlockSpec((POOL_HALO, width), lambda i: (jnp.maximum(i * halo_blocks - 1, 0), 0)),
            pl.BlockSpec((tm, attn.shape[1]), row),
            pl.BlockSpec((tm, d), lambda i: (i, 0)),
            pl.BlockSpec((tm, d), lambda i: (i, 1)),
            pl.BlockSpec(w_grp.shape, lambda i: (0, 0, 0)),
            pl.BlockSpec((1, width), fixed),
            pl.BlockSpec(w_po.shape, fixed),
            pl.BlockSpec(w_ao.shape, fixed),
            pl.BlockSpec(w_out.shape, fixed),
            pl.BlockSpec((1, d), fixed),
        ],
        out_specs=[pl.BlockSpec((tm, d), row), pl.BlockSpec((tm, d), row)],
        out_shape=[jax.ShapeDtypeStruct((m, d), F32), jax.ShapeDtypeStruct((m, d), BF16)],
        compiler_params=_params("parallel"),
        name="mix",
    )(x, u_pool, u_pool, attn, gates, gates, w_grp, scale, w_po, w_ao, w_out, g_ffn)


def _ffn_kernel(h_ref, x_ref, wg_ref, wu_ref, wd_ref, o_ref, *, res_steps):
    j = pl.program_id(1)

    def step(first):
        h = h_ref[...]
        a = _dot(h, wg_ref[...].astype(BF16))
        u = _dot(h, wu_ref[...].astype(BF16))
        act = _silu(a) * u
        down = _dot(act.astype(BF16), wd_ref[...].astype(BF16))
        if first:
            o_ref[...] = down
        else:
            o_ref[...] += down

    pl.when(j == 0)(lambda: step(True))
    pl.when(j > 0)(lambda: step(False))

    slab = x_ref.shape[1]
    for c in range(res_steps):
        @pl.when(j == c)
        def _():
            o_ref[:, c * slab:(c + 1) * slab] += x_ref[...]


def _ffn_call(h, x, w_gate, w_up, w_down, tm, tf):
    m, d = x.shape
    f = w_gate.shape[1]
    res_steps = FFN_RESIDUAL_STEPS
    assert f // tf >= res_steps
    return pl.pallas_call(
        functools.partial(_ffn_kernel, res_steps=res_steps),
        grid=(m // tm, f // tf),
        in_specs=[
            pl.BlockSpec((tm, d), lambda i, j: (i, 0)),
            pl.BlockSpec((tm, d // res_steps), lambda i, j: (i, jnp.minimum(j, res_steps - 1))),
            pl.BlockSpec((d, tf), lambda i, j: (0, j)),
            pl.BlockSpec((d, tf), lambda i, j: (0, j)),
            pl.BlockSpec((tf, d), lambda i, j: (j, 0)),
        ],
        out_specs=pl.BlockSpec((tm, d), lambda i, j: (i, 0)),
        out_shape=jax.ShapeDtypeStruct((m, d), F32),
        compiler_params=pltpu.CompilerParams(dimension_semantics=("parallel", "arbitrary"),
                                             vmem_limit_bytes=FFN_VMEM_LIMIT),
        name="ffn",
    )(h, x, w_gate, w_up, w_down)


def _rope_slot(t):
    z = jnp.zeros(t.shape[:-1] + (ROPE_HALF,), t.dtype)
    return jnp.concatenate([t[..., :ROPE_HALF], z, t[..., ROPE_HALF:], z], axis=-1)


def _head_slot(t):
    return jnp.concatenate([t[..., :QK_NOPE_DIM], _rope_slot(t[..., QK_NOPE_DIM:])], axis=-1)


def kernel(x, positions, attn_norm_g, w_in, b_gate, q_a_norm_g, w_q_b, kv_a_norm_g, w_kv_b,
           q_norm_g, k_norm_g, w_attn_o, w_pool_grp, pool_scale, w_pool_o, w_out,
           ffn_norm_g, w_ffn_gate, w_ffn_up, w_ffn_down):
    b, s, d = x.shape
    depth = w_in.shape[0]
    m = b * s
    pool_width = w_pool_o.shape[1]

    inv_freq = ROPE_THETA ** (-jnp.arange(ROPE_HALF, dtype=F32) / ROPE_HALF)
    ang_t = positions.astype(F32).reshape(1, m) * inv_freq.reshape(ROPE_HALF, 1)
    cos_t = jnp.cos(ang_t)
    sin_t = jnp.sin(ang_t)

    xf = x.reshape(m, d)
    for l in range(depth):
        g_lat = jnp.concatenate([q_a_norm_g[l], kv_a_norm_g[l]]).reshape(1, -1)
        wqt = w_q_b[l].T.astype(BF16)
        wkv = w_kv_b[l].reshape(KV_LORA_RANK, N_HEADS, QK_NOPE_DIM + V_HEAD_DIM)
        wk = wkv[..., :QK_NOPE_DIM].reshape(KV_LORA_RANK, -1).astype(BF16)
        wvt = wkv[..., QK_NOPE_DIM:].reshape(KV_LORA_RANK, -1).T.astype(BF16)
        tq = ATTN_TILE
        gq = jnp.broadcast_to((q_norm_g[l] * (QK_HEAD_DIM ** -0.5 * LOG2_E)).reshape(-1, 1),
                              (QK_HEAD_DIM, tq))
        gk = _head_slot(k_norm_g[l]).reshape(1, HEAD_SLOT)
        g_attn_norm = attn_norm_g[l].reshape(1, d)

        lat, u_pool, gates, kr = _inproj_call(xf, g_attn_norm, w_in[l].T, g_lat,
                                              b_gate[l].reshape(1, -1), pool_width, tm=INPROJ_ROWS)
        qt, k, vt = _qkv_call(lat, kr, cos_t, sin_t, wqt, wk, wvt, gq, gk, tm=tq)
        attn, w_po, w_ao, w_o = _flash_call(qt.reshape(b, s // tq, -1, tq), k.reshape(b, s, -1),
                                            vt.reshape(b, s // tq, -1, tq),
                                            [w_pool_o[l], w_attn_o[l], w_out[l]],
                                            tq=tq, heads=ATTN_HEADS_PER_STEP)
        xf, h2 = _mix_call(xf, u_pool, attn.reshape(m, -1), gates, w_pool_grp[l].astype(BF16),
                           pool_scale[l].reshape(1, -1), w_po, w_ao, w_o,
                           ffn_norm_g[l].reshape(1, d), tm=MIX_ROWS, seq=s)
        xf = _ffn_call(h2, xf, w_ffn_gate[l], w_ffn_up[l], w_ffn_down[l], tm=FFN_ROWS, tf=FFN_COLS)
    return xf.reshape(b, s, d)
```

```python
import functools

import jax
import jax.numpy as jnp
from jax import lax
from jax.experimental import pallas as pl
from jax.experimental.pallas import tpu as pltpu

F32 = jnp.float32
BF16 = jnp.bfloat16

N_HEADS = 16
QK_NOPE_DIM = 128
QK_ROPE_DIM = 64
QK_HEAD_DIM = QK_NOPE_DIM + QK_ROPE_DIM
V_HEAD_DIM = 128
Q_LORA_RANK = 512
KV_LORA_RANK = 512
ROPE_THETA = 10000.0
POOL_WINDOWS = (2, 4, 8, 16)
EPS = 1e-6

LANES = 128
SUBLANES = 8
HEAD_SLOT = 2 * LANES
ROPE_HALF = QK_ROPE_DIM // 2
POOL_HALO = 16
VMEM_LIMIT = 56 * 1024 * 1024
FFN_VMEM_LIMIT = 62 * 1024 * 1024
FFN_RESIDUAL_STEPS = 8

INPROJ_ROWS = 1024
ATTN_TILE = 512
ATTN_HEADS_PER_STEP = 4
MIX_ROWS = 256
FFN_ROWS, FFN_COLS = 1024, 512
WEIGHT_RING = 4
SUM_ROWS = 16
LOG2_E = 1.4426950408889634
_CONTRACT_LAST = (((1,), (1,)), ((), ()))


def _params(*sem):
    return pltpu.CompilerParams(dimension_semantics=sem, vmem_limit_bytes=VMEM_LIMIT)


def _dot(a, b):
    return jnp.dot(a, b, preferred_element_type=F32)


def _sigmoid(x):
    return 0.5 * jnp.tanh(0.5 * x) + 0.5


def _silu(x):
    t = 0.5 * x
    return t + t * jnp.tanh(t)


def _rms_bf16(x, g):
    ms = jnp.mean(x * x, axis=-1, keepdims=True)
    return (x * lax.rsqrt(ms + EPS) * g).astype(BF16)


def _rope(t, cos, sin):
    return t * cos + pltpu.roll(t, 2 * ROPE_HALF, 1) * sin


def _dot_t(a, w):
    return lax.dot_general(a, w.astype(BF16), _CONTRACT_LAST, preferred_element_type=F32)


def _inproj_kernel(x_hbm, g_ref, w_hbm, gl_ref, b_ref, wkr_ref,
                   lat_ref, up_ref, gate_ref, kr_ref, h_ref, w_ring, w_sem, x_buf, x_sem,
                   *, n_lat, n_up, w_row, tn, tm):
    i = pl.program_id(0)
    j = pl.program_id(1)
    n_i = pl.num_programs(0)
    n_j = pl.num_programs(1)
    t = i * n_j + j
    last = n_i * n_j - 1

    def w_copy(step):
        slot = lax.rem(step, WEIGHT_RING)
        rows = pl.ds(w_row(lax.rem(step, n_j)), tn)
        return pltpu.make_async_copy(w_hbm.at[rows, :], w_ring.at[slot], w_sem.at[slot])

    def x_copy(tile):
        slot = lax.rem(tile, 2)
        rows = pl.ds(pl.multiple_of(tile * tm, tm), tm)
        return pltpu.make_async_copy(x_hbm.at[rows, :], x_buf.at[slot], x_sem.at[slot])

    @pl.when(t == 0)
    def _():
        x_copy(0).start()
        for step in range(WEIGHT_RING - 1):
            w_copy(step).start()

    @pl.when(t + WEIGHT_RING - 1 <= last)
    def _():
        w_copy(t + WEIGHT_RING - 1).start()

    @pl.when(jnp.logical_and(j == 1, i + 1 < n_i))
    def _():
        x_copy(i + 1).start()

    w_copy(t).wait()
    w_ref = w_ring.at[lax.rem(t, WEIGHT_RING)]

    @pl.when(j == 0)
    def _():
        x_copy(i).wait()
        h_ref[...] = _rms_bf16(x_buf[lax.rem(i, 2)], g_ref[...])
        wkr = wkr_ref[...]
        z = jnp.zeros((ROPE_HALF, wkr.shape[1]), F32)
        slot = jnp.concatenate([wkr[:ROPE_HALF], z, wkr[ROPE_HALF:], z], axis=0)
        kr_ref[...] = _dot_t(h_ref[...], slot)

    @pl.when(j < n_lat)
    def _():
        lat_ref[...] = _rms_bf16(_dot_t(h_ref[...], w_ref[...]), gl_ref[...])

    @pl.when(jnp.logical_and(j >= n_lat, j < n_lat + n_up))
    def _():
        up_ref[...] = _dot_t(h_ref[...], w_ref[...])

    @pl.when(j >= n_lat + n_up)
    def _():
        gate_ref[...] = _sigmoid(_dot_t(h_ref[...], w_ref[...]) + b_ref[...])


def _inproj_call(x, g, w_t, g_lat, b_gate, pool_width, tm):
    m, d = x.shape
    tn = Q_LORA_RANK
    n_lat = g_lat.shape[1] // tn
    n_up = pool_width // tn
    n_gate = b_gate.shape[1] // tn
    o_kr = n_lat * tn
    o_up = o_kr + QK_ROPE_DIM
    clamp = lambda j, lo, n: jnp.clip(j - lo, 0, n - 1)
    sub = SUBLANES
    w_row = lambda j: sub * jnp.where(j < n_lat, j * (tn // sub),
                                      o_up // sub + (j - n_lat) * (tn // sub))
    return pl.pallas_call(
        functools.partial(_inproj_kernel, n_lat=n_lat, n_up=n_up, w_row=w_row, tn=tn, tm=tm),
        grid=(m // tm, n_lat + n_up + n_gate),
        in_specs=[
            pl.BlockSpec(memory_space=pl.ANY),
            pl.BlockSpec((1, d), lambda i, j: (0, 0)),
            pl.BlockSpec(memory_space=pl.ANY),
            pl.BlockSpec((1, tn), lambda i, j: (0, clamp(j, 0, n_lat))),
            pl.BlockSpec((1, tn), lambda i, j: (0, clamp(j, n_lat + n_up, n_gate))),
            pl.BlockSpec((pl.Element(QK_ROPE_DIM), pl.Element(d)), lambda i, j: (o_kr, 0)),
        ],
        out_specs=[
            pl.BlockSpec((tm, tn), lambda i, j: (i, clamp(j, 0, n_lat))),
            pl.BlockSpec((tm, tn), lambda i, j: (i, clamp(j, n_lat, n_up))),
            pl.BlockSpec((tm, tn), lambda i, j: (i, clamp(j, n_lat + n_up, n_gate))),
            pl.BlockSpec((tm, LANES), lambda i, j: (i, 0)),
        ],
        out_shape=[
            jax.ShapeDtypeStruct((m, n_lat * tn), BF16),
            jax.ShapeDtypeStruct((m, n_up * tn), F32),
            jax.ShapeDtypeStruct((m, n_gate * tn), F32),
            jax.ShapeDtypeStruct((m, LANES), F32),
        ],
        scratch_shapes=[pltpu.VMEM((tm, d), BF16), pltpu.VMEM((WEIGHT_RING, tn, d), F32),
                        pltpu.SemaphoreType.DMA((WEIGHT_RING,)),
                        pltpu.VMEM((2, tm, d), F32), pltpu.SemaphoreType.DMA((2,))],
        compiler_params=_params("arbitrary", "arbitrary"),
        name="inproj",
    )(x, g, w_t, g_lat, b_gate, w_t)


def _pool_mixer(u, halo, t0, wg_ref, sc_ref):
    tm = u.shape[0]
    halo = jnp.where(t0 > 0, halo, 0.0)
    ext = jnp.concatenate([halo, u], axis=0)
    pos = lax.broadcasted_iota(jnp.int32, (tm, 1), 0) + t0
    gd = wg_ref.shape[1]
    out = []
    for g, w in enumerate(POOL_WINDOWS):
        cols = slice(g * gd, (g + 1) * gd)
        a = ext[:, cols]
        shift = 1
        while shift < w:
            a = a + pltpu.roll(a, shift, 0)
            shift *= 2
        count = jnp.minimum(pos + 1, w).astype(F32)
        pooled = a[POOL_HALO:, :] / count - u[:, cols]
        y = _dot(pooled.astype(BF16), wg_ref[g]) * sc_ref[:, cols]
        out.append(y.astype(BF16))
    return jnp.concatenate(out, axis=1)


def _qkv_kernel(cq_ref, ckv_ref, kr_ref, cos_ref, sin_ref, wqt_ref, wk_ref, wvt_ref,
                gq_ref, gk_ref, qt_ref, k_ref, vt_ref):
    inv_dim = 1.0 / QK_HEAD_DIM
    half = ROPE_HALF
    cq = cq_ref[...]
    ckv = ckv_ref[...]
    cos_t = cos_ref[...]
    sin_t = sin_ref[...]
    qt = lax.dot_general(wqt_ref[...], cq, _CONTRACT_LAST, preferred_element_type=F32)
    vt_ref[...] = lax.dot_general(wvt_ref[...], ckv, _CONTRACT_LAST,
                                  preferred_element_type=F32).astype(BF16)
    gq = gq_ref[...]
    zeros = jnp.zeros((half, qt.shape[1]), BF16)
    for h in range(N_HEADS):
        src = h * QK_HEAD_DIM
        dst = h * HEAD_SLOT
        qh = qt[src:src + QK_HEAD_DIM]
        inv = lax.rsqrt(jnp.sum(qh * qh, axis=0, keepdims=True) * inv_dim + EPS)
        qh = qh * inv * gq
        t1 = qh[QK_NOPE_DIM:QK_NOPE_DIM + half]
        t2 = qh[QK_NOPE_DIM + half:]
        qt_ref[dst:dst + QK_NOPE_DIM] = qh[:QK_NOPE_DIM].astype(BF16)
        qt_ref[dst + LANES:dst + LANES + half] = (t1 * cos_t - t2 * sin_t).astype(BF16)
        qt_ref[dst + LANES + half:dst + LANES + 2 * half] = zeros
        qt_ref[dst + LANES + 2 * half:dst + LANES + 3 * half] = (t2 * cos_t + t1 * sin_t).astype(BF16)
        qt_ref[dst + LANES + 3 * half:dst + HEAD_SLOT] = zeros

    cos_r = cos_t.T
    sin_r = sin_t.T
    cos = jnp.concatenate([cos_r] * 4, axis=1)
    sin = jnp.concatenate([-sin_r, -sin_r, sin_r, sin_r], axis=1)
    kv = _dot(ckv, wk_ref[...])
    gk_n, gk_r = gk_ref[:, :LANES], gk_ref[:, LANES:]
    kr = kr_ref[...]
    ss_kr = jnp.sum(kr * kr, axis=-1, keepdims=True)
    kr_rot = _rope(kr * gk_r, cos, sin)
    for h in range(N_HEADS):
        lo = h * HEAD_SLOT
        kn = kv[:, h * LANES:(h + 1) * LANES]
        ssk = jnp.sum(kn * kn, axis=-1, keepdims=True) + ss_kr
        invk = lax.rsqrt(ssk * inv_dim + EPS)
        k_ref[:, lo:lo + LANES] = (kn * invk * gk_n).astype(BF16)
        k_ref[:, lo + LANES:lo + HEAD_SLOT] = (kr_rot * invk).astype(BF16)


def _qkv_call(lat, kr, cos_t, sin_t, wqt, wk, wvt, gq, gk, tm):
    m = lat.shape[0]
    rank = Q_LORA_RANK
    wide = N_HEADS * HEAD_SLOT
    vw = N_HEADS * V_HEAD_DIM
    row = lambda i: (i, 0)
    col = lambda i: (0, i)
    fixed = lambda i: (0, 0)
    return pl.pallas_call(
        _qkv_kernel,
        grid=(m // tm,),
        in_specs=[
            pl.BlockSpec((tm, rank), lambda i: (i, 0)),
            pl.BlockSpec((tm, rank), lambda i: (i, 1)),
            pl.BlockSpec((tm, LANES), row),
            pl.BlockSpec((ROPE_HALF, tm), col),
            pl.BlockSpec((ROPE_HALF, tm), col),
            pl.BlockSpec(wqt.shape, fixed),
            pl.BlockSpec(wk.shape, fixed),
            pl.BlockSpec(wvt.shape, fixed),
            pl.BlockSpec(gq.shape, fixed),
            pl.BlockSpec((1, HEAD_SLOT), fixed),
        ],
        out_specs=[
            pl.BlockSpec((None, wide, tm), lambda i: (i, 0, 0)),
            pl.BlockSpec((tm, wide), row),
            pl.BlockSpec((None, vw, tm), lambda i: (i, 0, 0)),
        ],
        out_shape=[
            jax.ShapeDtypeStruct((m // tm, wide, tm), BF16),
            jax.ShapeDtypeStruct((m, wide), BF16),
            jax.ShapeDtypeStruct((m // tm, vw, tm), BF16),
        ],
        compiler_params=_params("parallel"),
        name="qkv",
    )(lat, lat, kr, cos_t, sin_t, wqt, wk, wvt, gq, gk)


def _flash_kernel(q_ref, k_ref, vt_ref, *refs, tq, heads, n_cast):
    cast_src, (o_ref, *cast_dst), (m_ref, acc_ref, s_ref, mx_ref) = (
        refs[:n_cast], refs[n_cast:2 * n_cast + 1], refs[2 * n_cast + 1:])
    for src, dst in zip(cast_src, cast_dst):
        dst[...] = src[...].astype(BF16)
    n = pl.program_id(2)
    ones = jnp.ones((SUM_ROWS, tq), BF16)
    m_ref[...] = jnp.full(m_ref.shape, -jnp.inf, F32)
    acc_ref[...] = jnp.zeros(acc_ref.shape, F32)

    half = tq // 2
    head_cols = lambda h: slice(h * HEAD_SLOT, (h + 1) * HEAD_SLOT)

    def v_rows(h, c):
        vt = vt_ref[c, h * V_HEAD_DIM:(h + 1) * V_HEAD_DIM, :]
        return jnp.concatenate([vt, ones], axis=0)

    def new_max(h, buf):
        m_prev = m_ref[h]
        m_new = jnp.maximum(m_prev, mx_ref[buf, h])
        m_ref[h] = m_new
        return m_new, jnp.exp2(m_prev - m_new)

    def scores(h, c, buf):
        start = pl.multiple_of(c * tq, tq)
        s = _dot(k_ref[pl.ds(start, tq), head_cols(h)], q_ref[head_cols(h), :])
        s_ref[buf, h] = s
        mx_ref[buf, h] = jnp.max(s, axis=0, keepdims=True)

    def accumulate(h, c, buf):
        m_new, alpha = new_max(h, buf)
        p = jnp.exp2(s_ref[buf, h] - m_new)
        acc_ref[h] = alpha * acc_ref[h] + _dot(v_rows(h, c), p.astype(BF16))

    def scores_diagonal(h, c, buf):
        start = pl.multiple_of(c * tq, tq)
        causal = (lax.broadcasted_iota(jnp.int32, (half, tq), 0)
                  <= lax.broadcasted_iota(jnp.int32, (half, tq), 1))
        q_t = q_ref[head_cols(h), :]
        s0 = _dot(k_ref[pl.ds(start, half), head_cols(h)], q_t)
        s1 = _dot(k_ref[pl.ds(start + half, half), head_cols(h)], q_t[:, half:])
        s0 = jnp.where(causal, s0, -jnp.inf)
        s1 = jnp.where(causal[:, :half], s1, -jnp.inf)
        s_ref[buf, h, :half, :] = s0
        s_ref[buf, h, half:, half:] = s1
        m0 = jnp.max(s0, axis=0, keepdims=True)
        m1 = jnp.max(s1, axis=0, keepdims=True)
        mx_ref[buf, h] = jnp.concatenate([m0[:, :half], jnp.maximum(m0[:, half:], m1)], axis=1)

    def accumulate_diagonal(h, c, buf):
        m_new, alpha = new_max(h, buf)
        p0 = jnp.exp2(s_ref[buf, h, :half, :] - m_new).astype(BF16)
        p1 = jnp.exp2(s_ref[buf, h, half:, half:] - m_new[:, half:]).astype(BF16)
        vt = v_rows(h, c)
        acc = alpha * acc_ref[h] + _dot(vt[:, :half], p0)
        acc_ref[h, :, :half] = acc[:, :half]
        acc_ref[h, :, half:] = acc[:, half:] + _dot(vt[:, half:], p1)

    def stage(c, buf, next_diagonal):
        for h in range(heads):
            (scores_diagonal if next_diagonal else scores)(h, c + 1, 1 - buf)
            accumulate(h, c, buf)

    def drain(c, buf):
        for h in range(heads):
            accumulate_diagonal(h, c, buf)

    @pl.when(n == 0)
    def _():
        for h in range(heads):
            scores_diagonal(h, 0, 0)
        drain(0, 0)

    @pl.when(n > 0)
    def _():
        for h in range(heads):
            scores(h, 0, 0)

    def two_stages(i, carry):
        stage(2 * i, 0, False)
        stage(2 * i + 1, 1, False)
        return carry

    lax.fori_loop(0, lax.div(n - 1, 2), two_stages, None)

    @pl.when(n % 2 == 1)
    def _():
        stage(n - 1, 0, True)
        drain(n, 1)

    @pl.when(jnp.logical_and(n >= 2, n % 2 == 0))
    def _():
        stage(n - 2, 0, False)
        stage(n - 1, 1, True)
        drain(n, 0)

    for h in range(heads):
        out = acc_ref[h, :V_HEAD_DIM, :] / acc_ref[h, V_HEAD_DIM:V_HEAD_DIM + 1, :]
        o_ref[:, h * V_HEAD_DIM:(h + 1) * V_HEAD_DIM] = out.T.astype(BF16)


def _flash_call(q, k, vt, cast_weights, tq, heads):
    b, s, _ = k.shape
    grid = (b, N_HEADS // heads, s // tq)
    n_steps = grid[0] * grid[1] * grid[2]
    step = lambda bi, h, i: ((bi * grid[1] + h) * grid[2] + i, 0)
    cast_specs = [pl.BlockSpec((w.shape[0] // n_steps, w.shape[1]), step) for w in cast_weights]
    return pl.pallas_call(
        functools.partial(_flash_kernel, tq=tq, heads=heads, n_cast=len(cast_weights)),
        grid=grid,
        in_specs=[
            pl.BlockSpec((None, None, heads * HEAD_SLOT, tq), lambda bi, h, i: (bi, i, h, 0)),
            pl.BlockSpec((None, s, heads * HEAD_SLOT), lambda bi, h, i: (bi, 0, h)),
            pl.BlockSpec((None, s // tq, heads * V_HEAD_DIM, tq), lambda bi, h, i: (bi, 0, h, 0)),
        ] + cast_specs,
        out_specs=[pl.BlockSpec((None, tq, heads * V_HEAD_DIM), lambda bi, h, i: (bi, i, h))] + cast_specs,
        out_shape=[jax.ShapeDtypeStruct((b, s, N_HEADS * V_HEAD_DIM), BF16)]
        + [jax.ShapeDtypeStruct(w.shape, BF16) for w in cast_weights],
        scratch_shapes=[
            pltpu.VMEM((heads, 1, tq), F32),
            pltpu.VMEM((heads, V_HEAD_DIM + SUM_ROWS, tq), F32),
            pltpu.VMEM((2, heads, tq, tq), F32),
            pltpu.VMEM((2, heads, 1, tq), F32),
        ],
        compiler_params=_params("parallel", "parallel", "arbitrary"),
        name="flash",
    )(q, k, vt, *cast_weights)


def _mix_kernel(x_ref, u_ref, halo_ref, attn_ref, gp_ref, ga_ref, wg_ref, sc_ref,
                wpo_ref, wao_ref, wout_ref, g2_ref, o_ref, h2_ref, *, tm, seq):
    t0 = (pl.program_id(0) * tm) % seq
    pooled = _pool_mixer(u_ref[...], halo_ref[...], t0, wg_ref, sc_ref)
    y_pool = _dot(pooled, wpo_ref[...])
    y_attn = _dot(attn_ref[...], wao_ref[...])
    mixed = gp_ref[...] * y_pool + ga_ref[...] * y_attn
    x1 = x_ref[...] + _dot(mixed.astype(BF16), wout_ref[...])
    o_ref[...] = x1
    h2_ref[...] = _rms_bf16(x1, g2_ref[...])


def _mix_call(x, u_pool, attn, gates, w_grp, scale, w_po, w_ao, w_out, g_ffn, tm, seq):
    m, d = x.shape
    width = u_pool.shape[1]
    halo_blocks = tm // POOL_HALO
    row = lambda i: (i, 0)
    fixed = lambda i: (0, 0)
    return pl.pallas_call(
        functools.partial(_mix_kernel, tm=tm, seq=seq),
        grid=(m // tm,),
        in_specs=[
            pl.BlockSpec((tm, d), row),
            pl.BlockSpec((tm, width), row),
            pl.BlockSpec((POOL_HALO, width), lambda i: (jnp.maximum(i * halo_blocks - 1, 0), 0)),
            pl.BlockSpec((tm, attn.shape[1]), row),
            pl.BlockSpec((tm, d), lambda i: (i, 0)),
            pl.BlockSpec((tm, d), lambda i: (i, 1)),
            pl.BlockSpec(w_grp.shape, lambda i: (0, 0, 0)),
            pl.BlockSpec((1, width), fixed),
            pl.BlockSpec(w_po.shape, fixed),
            pl.BlockSpec(w_ao.shape, fixed),
            pl.BlockSpec(w_out.shape, fixed),
            pl.BlockSpec((1, d), fixed),
        ],
        out_specs=[pl.BlockSpec((tm, d), row), pl.BlockSpec((tm, d), row)],
        out_shape=[jax.ShapeDtypeStruct((m, d), F32), jax.ShapeDtypeStruct((m, d), BF16)],
        compiler_params=_params("parallel"),
        name="mix",
    )(x, u_pool, u_pool, attn, gates, gates, w_grp, scale, w_po, w_ao, w_out, g_ffn)


def _ffn_kernel(h_ref, x_ref, wg_ref, wu_ref, wd_ref, o_ref, *, res_steps):
    j = pl.program_id(1)

    def step(first):
        h = h_ref[...]
        a = _dot(h, wg_ref[...].astype(BF16))
        u = _dot(h, wu_ref[...].astype(BF16))
        act = _silu(a) * u
        down = _dot(act.astype(BF16), wd_ref[...].astype(BF16))
        if first:
            o_ref[...] = down
        else:
            o_ref[...] += down

    pl.when(j == 0)(lambda: step(True))
    pl.when(j > 0)(lambda: step(False))

    slab = x_ref.shape[1]
    for c in range(res_steps):
        @pl.when(j == c)
        def _():
            o_ref[:, c * slab:(c + 1) * slab] += x_ref[...]


def _ffn_call(h, x, w_gate, w_up, w_down, tm, tf):
    m, d = x.shape
    f = w_gate.shape[1]
    res_steps = FFN_RESIDUAL_STEPS
    assert f // tf >= res_steps
    return pl.pallas_call(
        functools.partial(_ffn_kernel, res_steps=res_steps),
        grid=(m // tm, f // tf),
        in_specs=[
            pl.BlockSpec((tm, d), lambda i, j: (i, 0)),
            pl.BlockSpec((tm, d // res_steps), lambda i, j: (i, jnp.minimum(j, res_steps - 1))),
            pl.BlockSpec((d, tf), lambda i, j: (0, j)),
            pl.BlockSpec((d, tf), lambda i, j: (0, j)),
            pl.BlockSpec((tf, d), lambda i, j: (j, 0)),
        ],
        out_specs=pl.BlockSpec((tm, d), lambda i, j: (i, 0)),
        out_shape=jax.ShapeDtypeStruct((m, d), F32),
        compiler_params=pltpu.CompilerParams(dimension_semantics=("parallel", "arbitrary"),
                                             vmem_limit_bytes=FFN_VMEM_LIMIT),
        name="ffn",
    )(h, x, w_gate, w_up, w_down)


def _rope_slot(t):
    z = jnp.zeros(t.shape[:-1] + (ROPE_HALF,), t.dtype)
    return jnp.concatenate([t[..., :ROPE_HALF], z, t[..., ROPE_HALF:], z], axis=-1)


def _head_slot(t):
    return jnp.concatenate([t[..., :QK_NOPE_DIM], _rope_slot(t[..., QK_NOPE_DIM:])], axis=-1)


def kernel(x, positions, attn_norm_g, w_in, b_gate, q_a_norm_g, w_q_b, kv_a_norm_g, w_kv_b,
           q_norm_g, k_norm_g, w_attn_o, w_pool_grp, pool_scale, w_pool_o, w_out,
           ffn_norm_g, w_ffn_gate, w_ffn_up, w_ffn_down):
    b, s, d = x.shape
    depth = w_in.shape[0]
    m = b * s
    pool_width = w_pool_o.shape[1]

    inv_freq = ROPE_THETA ** (-jnp.arange(ROPE_HALF, dtype=F32) / ROPE_HALF)
    ang_t = positions.astype(F32).reshape(1, m) * inv_freq.reshape(ROPE_HALF, 1)
    cos_t = jnp.cos(ang_t)
    sin_t = jnp.sin(ang_t)

    xf = x.reshape(m, d)
    for l in range(depth):
        g_lat = jnp.concatenate([q_a_norm_g[l], kv_a_norm_g[l]]).reshape(1, -1)
        wqt = w_q_b[l].T.astype(BF16)
        wkv = w_kv_b[l].reshape(KV_LORA_RANK, N_HEADS, QK_NOPE_DIM + V_HEAD_DIM)
        wk = wkv[..., :QK_NOPE_DIM].reshape(KV_LORA_RANK, -1).astype(BF16)
        wvt = wkv[..., QK_NOPE_DIM:].reshape(KV_LORA_RANK, -1).T.astype(BF16)
        tq = ATTN_TILE
        gq = jnp.broadcast_to((q_norm_g[l] * (QK_HEAD_DIM ** -0.5 * LOG2_E)).reshape(-1, 1),
                              (QK_HEAD_DIM, tq))
        gk = _head_slot(k_norm_g[l]).reshape(1, HEAD_SLOT)
        g_attn_norm = attn_norm_g[l].reshape(1, d)

        lat, u_pool, gates, kr = _inproj_call(xf, g_attn_norm, w_in[l].T, g_lat,
                                              b_gate[l].reshape(1, -1), pool_width, tm=INPROJ_ROWS)
        qt, k, vt = _qkv_call(lat, kr, cos_t, sin_t, wqt, wk, wvt, gq, gk, tm=tq)
        attn, w_po, w_ao, w_o = _flash_call(qt.reshape(b, s // tq, -1, tq), k.reshape(b, s, -1),
                                            vt.reshape(b, s // tq, -1, tq),
                                            [w_pool_o[l], w_attn_o[l], w_out[l]],
                                            tq=tq, heads=ATTN_HEADS_PER_STEP)
        xf, h2 = _mix_call(xf, u_pool, attn.reshape(m, -1), gates, w_pool_grp[l].astype(BF16),
                           pool_scale[l].reshape(1, -1), w_po, w_ao, w_o,
                           ffn_norm_g[l].reshape(1, d), tm=MIX_ROWS, seq=s)
        xf = _ffn_call(h2, xf, w_ffn_gate[l], w_ffn_up[l], w_ffn_down[l], tm=FFN_ROWS, tf=FFN_COLS)
    return xf.reshape(b, s, d)
```

```python
import functools

import jax
import jax.numpy as jnp
from jax import lax
from jax.experimental import pallas as pl
from jax.experimental.pallas import tpu as pltpu

F32 = jnp.float32
BF16 = jnp.bfloat16

N_HEADS = 16
QK_NOPE_DIM = 128
QK_ROPE_DIM = 64
QK_HEAD_DIM = QK_NOPE_DIM + QK_ROPE_DIM
V_HEAD_DIM = 128
Q_LORA_RANK = 512
KV_LORA_RANK = 512
ROPE_THETA = 10000.0
POOL_WINDOWS = (2, 4, 8, 16)
EPS = 1e-6

LANES = 128
SUBLANES = 8
HEAD_SLOT = 2 * LANES
ROPE_HALF = QK_ROPE_DIM // 2
POOL_HALO = 16
VMEM_LIMIT = 56 * 1024 * 1024
FFN_VMEM_LIMIT = 62 * 1024 * 1024
FFN_RESIDUAL_STEPS = 8

INPROJ_ROWS = 1024
ATTN_TILE = 512
ATTN_HEADS_PER_STEP = 4
MIX_ROWS = 256
FFN_ROWS, FFN_COLS = 1024, 512
WEIGHT_RING = 3
SUM_ROWS = 16
LOG2_E = 1.4426950408889634
_CONTRACT_LAST = (((1,), (1,)), ((), ()))


def _params(*sem):
    return pltpu.CompilerParams(dimension_semantics=sem, vmem_limit_bytes=VMEM_LIMIT)


def _dot(a, b):
    return jnp.dot(a, b, preferred_element_type=F32)


def _sigmoid(x):
    return 0.5 * jnp.tanh(0.5 * x) + 0.5


def _silu(x):
    t = 0.5 * x
    return t + t * jnp.tanh(t)


def _rms_bf16(x, g):
    ms = jnp.mean(x * x, axis=-1, keepdims=True)
    return (x * lax.rsqrt(ms + EPS) * g).astype(BF16)


def _rope(t, cos, sin):
    return t * cos + pltpu.roll(t, 2 * ROPE_HALF, 1) * sin


def _dot_t(a, w):
    return lax.dot_general(a, w.astype(BF16), _CONTRACT_LAST, preferred_element_type=F32)


def _inproj_kernel(x_hbm, g_ref, w_hbm, gl_ref, b_ref, wkr_ref,
                   lat_ref, up_ref, gate_ref, kr_ref, h_ref, w_ring, w_sem, x_buf, x_sem,
                   *, n_lat, n_up, w_row, tn, tm):
    i = pl.program_id(0)
    j = pl.program_id(1)
    n_i = pl.num_programs(0)
    n_j = pl.num_programs(1)
    t = i * n_j + j
    last = n_i * n_j - 1

    def w_copy(step):
        slot = lax.rem(step, WEIGHT_RING)
        rows = pl.ds(w_row(lax.rem(step, n_j)), tn)
        return pltpu.make_async_copy(w_hbm.at[rows, :], w_ring.at[slot], w_sem.at[slot])

    def x_copy(tile):
        slot = lax.rem(tile, 2)
        rows = pl.ds(pl.multiple_of(tile * tm, tm), tm)
        return pltpu.make_async_copy(x_hbm.at[rows, :], x_buf.at[slot], x_sem.at[slot])

    @pl.when(t == 0)
    def _():
        x_copy(0).start()
        for step in range(WEIGHT_RING - 1):
            w_copy(step).start()

    @pl.when(t + WEIGHT_RING - 1 <= last)
    def _():
        w_copy(t + WEIGHT_RING - 1).start()

    @pl.when(jnp.logical_and(j == 1, i + 1 < n_i))
    def _():
        x_copy(i + 1).start()

    w_copy(t).wait()
    w_ref = w_ring.at[lax.rem(t, WEIGHT_RING)]

    @pl.when(j == 0)
    def _():
        x_copy(i).wait()
        h_ref[...] = _rms_bf16(x_buf[lax.rem(i, 2)], g_ref[...])
        wkr = wkr_ref[...]
        z = jnp.zeros((ROPE_HALF, wkr.shape[1]), F32)
        slot = jnp.concatenate([wkr[:ROPE_HALF], z, wkr[ROPE_HALF:], z], axis=0)
        kr_ref[...] = _dot_t(h_ref[...], slot)

    @pl.when(j < n_lat)
    def _():
        lat_ref[...] = _rms_bf16(_dot_t(h_ref[...], w_ref[...]), gl_ref[...])

    @pl.when(jnp.logical_and(j >= n_lat, j < n_lat + n_up))
    def _():
        up_ref[...] = _dot_t(h_ref[...], w_ref[...])

    @pl.when(j >= n_lat + n_up)
    def _():
        gate_ref[...] = _sigmoid(_dot_t(h_ref[...], w_ref[...]) + b_ref[...])


def _inproj_call(x, g, w_t, g_lat, b_gate, pool_width, tm):
    m, d = x.shape
    tn = Q_LORA_RANK
    n_lat = g_lat.shape[1] // tn
    n_up = pool_width // tn
    n_gate = b_gate.shape[1] // tn
    o_kr = n_lat * tn
    o_up = o_kr + QK_ROPE_DIM
    clamp = lambda j, lo, n: jnp.clip(j - lo, 0, n - 1)
    sub = SUBLANES
    w_row = lambda j: sub * jnp.where(j < n_lat, j * (tn // sub),
                                      o_up // sub + (j - n_lat) * (tn // sub))
    return pl.pallas_call(
        functools.partial(_inproj_kernel, n_lat=n_lat, n_up=n_up, w_row=w_row, tn=tn, tm=tm),
        grid=(m // tm, n_lat + n_up + n_gate),
        in_specs=[
            pl.BlockSpec(memory_space=pl.ANY),
            pl.BlockSpec((1, d), lambda i, j: (0, 0)),
            pl.BlockSpec(memory_space=pl.ANY),
            pl.BlockSpec((1, tn), lambda i, j: (0, clamp(j, 0, n_lat))),
            pl.BlockSpec((1, tn), lambda i, j: (0, clamp(j, n_lat + n_up, n_gate))),
            pl.BlockSpec((pl.Element(QK_ROPE_DIM), pl.Element(d)), lambda i, j: (o_kr, 0)),
        ],
        out_specs=[
            pl.BlockSpec((tm, tn), lambda i, j: (i, clamp(j, 0, n_lat))),
            pl.BlockSpec((tm, tn), lambda i, j: (i, clamp(j, n_lat, n_up))),
            pl.BlockSpec((tm, tn), lambda i, j: (i, clamp(j, n_lat + n_up, n_gate))),
            pl.BlockSpec((tm, LANES), lambda i, j: (i, 0)),
        ],
        out_shape=[
            jax.ShapeDtypeStruct((m, n_lat * tn), BF16),
            jax.ShapeDtypeStruct((m, n_up * tn), F32),
            jax.ShapeDtypeStruct((m, n_gate * tn), F32),
            jax.ShapeDtypeStruct((m, LANES), F32),
        ],
        scratch_shapes=[pltpu.VMEM((tm, d), BF16), pltpu.VMEM((WEIGHT_RING, tn, d), F32),
                        pltpu.SemaphoreType.DMA((WEIGHT_RING,)),
                        pltpu.VMEM((2, tm, d), F32), pltpu.SemaphoreType.DMA((2,))],
        compiler_params=_params("arbitrary", "arbitrary"),
        name="inproj",
    )(x, g, w_t, g_lat, b_gate, w_t)


def _pool_mixer(u, halo, t0, wg_ref, sc_ref):
    tm = u.shape[0]
    halo = jnp.where(t0 > 0, halo, 0.0)
    ext = jnp.concatenate([halo, u], axis=0)
    pos = lax.broadcasted_iota(jnp.int32, (tm, 1), 0) + t0
    gd = wg_ref.shape[1]
    out = []
    for g, w in enumerate(POOL_WINDOWS):
        cols = slice(g * gd, (g + 1) * gd)
        a = ext[:, cols]
        shift = 1
        while shift < w:
            a = a + pltpu.roll(a, shift, 0)
            shift *= 2
        count = jnp.minimum(pos + 1, w).astype(F32)
        pooled = a[POOL_HALO:, :] / count - u[:, cols]
        y = _dot(pooled.astype(BF16), wg_ref[g]) * sc_ref[:, cols]
        out.append(y.astype(BF16))
    return jnp.concatenate(out, axis=1)


def _qkv_kernel(cq_ref, ckv_ref, kr_ref, cos_ref, sin_ref, wq_ref, wk_ref, wvt_ref,
                gq_ref, gk_ref, qt_ref, k_ref, vt_ref):
    inv_dim = 1.0 / QK_HEAD_DIM
    half = ROPE_HALF
    cq = cq_ref[...]
    ckv = ckv_ref[...]
    cos_t = cos_ref[...]
    sin_t = sin_ref[...]
    qt = lax.dot_general(wq_ref[...].astype(BF16), cq, (((0,), (1,)), ((), ())),
                         preferred_element_type=F32)
    vt_ref[...] = lax.dot_general(wvt_ref[...], ckv, _CONTRACT_LAST,
                                  preferred_element_type=F32).astype(BF16)
    gq = gq_ref[...]
    zeros = jnp.zeros((half, qt.shape[1]), BF16)
    for h in range(N_HEADS):
        src = h * QK_HEAD_DIM
        dst = h * HEAD_SLOT
        qh = qt[src:src + QK_HEAD_DIM]
        inv = lax.rsqrt(jnp.sum(qh * qh, axis=0, keepdims=True) * inv_dim + EPS)
        qh = qh * inv * gq
        t1 = qh[QK_NOPE_DIM:QK_NOPE_DIM + half]
        t2 = qh[QK_NOPE_DIM + half:]
        qt_ref[dst:dst + QK_NOPE_DIM] = qh[:QK_NOPE_DIM].astype(BF16)
        qt_ref[dst + LANES:dst + LANES + half] = (t1 * cos_t - t2 * sin_t).astype(BF16)
        qt_ref[dst + LANES + half:dst + LANES + 2 * half] = zeros
        qt_ref[dst + LANES + 2 * half:dst + LANES + 3 * half] = (t2 * cos_t + t1 * sin_t).astype(BF16)
        qt_ref[dst + LANES + 3 * half:dst + HEAD_SLOT] = zeros

    cos_r = cos_t.T
    sin_r = sin_t.T
    cos = jnp.concatenate([cos_r] * 4, axis=1)
    sin = jnp.concatenate([-sin_r, -sin_r, sin_r, sin_r], axis=1)
    kv = _dot(ckv, wk_ref[...])
    gk_n, gk_r = gk_ref[:, :LANES], gk_ref[:, LANES:]
    kr = kr_ref[...]
    ss_kr = jnp.sum(kr * kr, axis=-1, keepdims=True)
    kr_rot = _rope(kr * gk_r, cos, sin)
    for h in range(N_HEADS):
        lo = h * HEAD_SLOT
        kn = kv[:, h * LANES:(h + 1) * LANES]
        ssk = jnp.sum(kn * kn, axis=-1, keepdims=True) + ss_kr
        invk = lax.rsqrt(ssk * inv_dim + EPS)
        k_ref[:, lo:lo + LANES] = (kn * invk * gk_n).astype(BF16)
        k_ref[:, lo + LANES:lo + HEAD_SLOT] = (kr_rot * invk).astype(BF16)


def _qkv_call(lat, kr, cos_t, sin_t, wq, wk, wvt, gq, gk, tm):
    m = lat.shape[0]
    rank = Q_LORA_RANK
    wide = N_HEADS * HEAD_SLOT
    vw = N_HEADS * V_HEAD_DIM
    row = lambda i: (i, 0)
    col = lambda i: (0, i)
    fixed = lambda i: (0, 0)
    return pl.pallas_call(
        _qkv_kernel,
        grid=(m // tm,),
        in_specs=[
            pl.BlockSpec((tm, rank), lambda i: (i, 0)),
            pl.BlockSpec((tm, rank), lambda i: (i, 1)),
            pl.BlockSpec((tm, LANES), row),
            pl.BlockSpec((ROPE_HALF, tm), col),
            pl.BlockSpec((ROPE_HALF, tm), col),
            pl.BlockSpec(wq.shape, fixed),
            pl.BlockSpec(wk.shape, fixed),
            pl.BlockSpec(wvt.shape, fixed),
            pl.BlockSpec(gq.shape, fixed),
            pl.BlockSpec((1, HEAD_SLOT), fixed),
        ],
        out_specs=[
            pl.BlockSpec((None, wide, tm), lambda i: (i, 0, 0)),
            pl.BlockSpec((tm, wide), row),
            pl.BlockSpec((None, vw, tm), lambda i: (i, 0, 0)),
        ],
        out_shape=[
            jax.ShapeDtypeStruct((m // tm, wide, tm), BF16),
            jax.ShapeDtypeStruct((m, wide), BF16),
            jax.ShapeDtypeStruct((m // tm, vw, tm), BF16),
        ],
        compiler_params=_params("parallel"),
        name="qkv",
    )(lat, lat, kr, cos_t, sin_t, wq, wk, wvt, gq, gk)


def _flash_kernel(q_ref, k_ref, vt_ref, *refs, tq, heads, n_cast):
    cast_src, (o_ref, *cast_dst), (m_ref, acc_ref, s_ref, mx_ref) = (
        refs[:n_cast], refs[n_cast:2 * n_cast + 1], refs[2 * n_cast + 1:])
    for src, dst in zip(cast_src, cast_dst):
        dst[...] = src[...].astype(BF16)
    n = pl.program_id(2)
    ones = jnp.ones((SUM_ROWS, tq), BF16)
    m_ref[...] = jnp.full(m_ref.shape, -jnp.inf, F32)
    acc_ref[...] = jnp.zeros(acc_ref.shape, F32)

    half = tq // 2
    head_cols = lambda h: slice(h * HEAD_SLOT, (h + 1) * HEAD_SLOT)

    def v_rows(h, c):
        vt = vt_ref[c, h * V_HEAD_DIM:(h + 1) * V_HEAD_DIM, :]
        return jnp.concatenate([vt, ones], axis=0)

    def new_max(h, buf):
        m_prev = m_ref[h]
        m_new = jnp.maximum(m_prev, mx_ref[buf, h])
        m_ref[h] = m_new
        return m_new, jnp.exp2(m_prev - m_new)

    def scores(h, c, buf):
        start = pl.multiple_of(c * tq, tq)
        s = _dot(k_ref[pl.ds(start, tq), head_cols(h)], q_ref[head_cols(h), :])
        s_ref[buf, h] = s
        mx_ref[buf, h] = jnp.max(s, axis=0, keepdims=True)

    def accumulate(h, c, buf):
        m_new, alpha = new_max(h, buf)
        p = jnp.exp2(s_ref[buf, h] - m_new)
        acc_ref[h] = alpha * acc_ref[h] + _dot(v_rows(h, c), p.astype(BF16))

    def scores_diagonal(h, c, buf):
        start = pl.multiple_of(c * tq, tq)
        causal = (lax.broadcasted_iota(jnp.int32, (half, tq), 0)
                  <= lax.broadcasted_iota(jnp.int32, (half, tq), 1))
        q_t = q_ref[head_cols(h), :]
        s0 = _dot(k_ref[pl.ds(start, half), head_cols(h)], q_t)
        s1 = _dot(k_ref[pl.ds(start + half, half), head_cols(h)], q_t[:, half:])
        s0 = jnp.where(causal, s0, -jnp.inf)
        s1 = jnp.where(causal[:, :half], s1, -jnp.inf)
        s_ref[buf, h, :half, :] = s0
        s_ref[buf, h, half:, half:] = s1
        m0 = jnp.max(s0, axis=0, keepdims=True)
        m1 = jnp.max(s1, axis=0, keepdims=True)
        mx_ref[buf, h] = jnp.concatenate([m0[:, :half], jnp.maximum(m0[:, half:], m1)], axis=1)

    def accumulate_diagonal(h, c, buf):
        m_new, alpha = new_max(h, buf)
        p0 = jnp.exp2(s_ref[buf, h, :half, :] - m_new).astype(BF16)
        p1 = jnp.exp2(s_ref[buf, h, half:, half:] - m_new[:, half:]).astype(BF16)
        vt = v_rows(h, c)
        acc = alpha * acc_ref[h] + _dot(vt[:, :half], p0)
        acc_ref[h, :, :half] = acc[:, :half]
        acc_ref[h, :, half:] = acc[:, half:] + _dot(vt[:, half:], p1)

    def stage(c, buf, next_diagonal):
        for h in range(heads):
            (scores_diagonal if next_diagonal else scores)(h, c + 1, 1 - buf)
            accumulate(h, c, buf)

    def drain(c, buf):
        for h in range(heads):
            accumulate_diagonal(h, c, buf)

    @pl.when(n == 0)
    def _():
        for h in range(heads):
            scores_diagonal(h, 0, 0)
        drain(0, 0)

    @pl.when(n > 0)
    def _():
        for h in range(heads):
            scores(h, 0, 0)

    def two_stages(i, carry):
        stage(2 * i, 0, False)
        stage(2 * i + 1, 1, False)
        return carry

    lax.fori_loop(0, lax.div(n - 1, 2), two_stages, None)

    @pl.when(n % 2 == 1)
    def _():
        stage(n - 1, 0, True)
        drain(n, 1)

    @pl.when(jnp.logical_and(n >= 2, n % 2 == 0))
    def _():
        stage(n - 2, 0, False)
        stage(n - 1, 1, True)
        drain(n, 0)

    for h in range(heads):
        out = acc_ref[h, :V_HEAD_DIM, :] / acc_ref[h, V_HEAD_DIM:V_HEAD_DIM + 1, :]
        o_ref[:, h * V_HEAD_DIM:(h + 1) * V_HEAD_DIM] = out.T.astype(BF16)


def _flash_call(q, k, vt, cast_weights, tq, heads):
    b, s, _ = k.shape
    grid = (b, N_HEADS // heads, s // tq)
    n_steps = grid[0] * grid[1] * grid[2]
    step = lambda bi, h, i: ((bi * grid[1] + h) * grid[2] + i, 0)
    cast_specs = [pl.BlockSpec((w.shape[0] // n_steps, w.shape[1]), step) for w in cast_weights]
    return pl.pallas_call(
        functools.partial(_flash_kernel, tq=tq, heads=heads, n_cast=len(cast_weights)),
        grid=grid,
        in_specs=[
            pl.BlockSpec((None, None, heads * HEAD_SLOT, tq), lambda bi, h, i: (bi, i, h, 0)),
            pl.BlockSpec((None, s, heads * HEAD_SLOT), lambda bi, h, i: (bi, 0, h)),
            pl.BlockSpec((None, s // tq, heads * V_HEAD_DIM, tq), lambda bi, h, i: (bi, 0, h, 0)),
        ] + cast_specs,
        out_specs=[pl.BlockSpec((None, tq, heads * V_HEAD_DIM), lambda bi, h, i: (bi, i, h))] + cast_specs,
        out_shape=[jax.ShapeDtypeStruct((b, s, N_HEADS * V_HEAD_DIM), BF16)]
        + [jax.ShapeDtypeStruct(w.shape, BF16) for w in cast_weights],
        scratch_shapes=[
            pltpu.VMEM((heads, 1, tq), F32),
            pltpu.VMEM((heads, V_HEAD_DIM + SUM_ROWS, tq), F32),
            pltpu.VMEM((2, heads, tq, tq), F32),
            pltpu.VMEM((2, heads, 1, tq), F32),
        ],
        compiler_params=_params("parallel", "parallel", "arbitrary"),
        name="flash",
    )(q, k, vt, *cast_weights)


def _mix_kernel(x_ref, u_ref, halo_ref, attn_ref, gp_ref, ga_ref, wg_ref, sc_ref,
                wpo_ref, wao_ref, wout_ref, g2_ref, o_ref, h2_ref, *, tm, seq):
    t0 = (pl.program_id(0) * tm) % seq
    pooled = _pool_mixer(u_ref[...], halo_ref[...], t0, wg_ref, sc_ref)
    y_pool = _dot(pooled, wpo_ref[...])
    y_attn = _dot(attn_ref[...], wao_ref[...])
    mixed = gp_ref[...] * y_pool + ga_ref[...] * y_attn
    x1 = x_ref[...] + _dot(mixed.astype(BF16), wout_ref[...])
    o_ref[...] = x1
    h2_ref[...] = _rms_bf16(x1, g2_ref[...])


def _mix_call(x, u_pool, attn, gates, w_grp, scale, w_po, w_ao, w_out, g_ffn, tm, seq):
    m, d = x.shape
    width = u_pool.shape[1]
    halo_blocks = tm // POOL_HALO
    row = lambda i: (i, 0)
    fixed = lambda i: (0, 0)
    return pl.pallas_call(
        functools.partial(_mix_kernel, tm=tm, seq=seq),
        grid=(m // tm,),
        in_specs=[
            pl.BlockSpec((tm, d), row),
            pl.BlockSpec((tm, width), row),
            pl.BlockSpec((POOL_HALO, width), lambda i: (jnp.maximum(i * halo_blocks - 1, 0), 0)),
            pl.BlockSpec((tm, attn.shape[1]), row),
            pl.BlockSpec((tm, d), lambda i: (i, 0)),
            pl.BlockSpec((tm, d), lambda i: (i, 1)),
            pl.BlockSpec(w_grp.shape, lambda i: (0, 0, 0)),
            pl.BlockSpec((1, width), fixed),
            pl.BlockSpec(w_po.shape, fixed),
            pl.BlockSpec(w_ao.shape, fixed),
            pl.BlockSpec(w_out.shape, fixed),
            pl.BlockSpec((1, d), fixed),
        ],
        out_specs=[pl.BlockSpec((tm, d), row), pl.BlockSpec((tm, d), row)],
        out_shape=[jax.ShapeDtypeStruct((m, d), F32), jax.ShapeDtypeStruct((m, d), BF16)],
        compiler_params=_params("parallel"),
        name="mix",
    )(x, u_pool, u_pool, attn, gates, gates, w_grp, scale, w_po, w_ao, w_out, g_ffn)


def _ffn_kernel(h_ref, x_ref, wg_ref, wu_ref, wd_ref, o_ref, *, res_steps):
    j = pl.program_id(1)

    def step(first):
        h = h_ref[...]
        a = _dot(h, wg_ref[...].astype(BF16))
        u = _dot(h, wu_ref[...].astype(BF16))
        act = _silu(a) * u
        down = _dot(act.astype(BF16), wd_ref[...].astype(BF16))
        if first:
            o_ref[...] = down
        else:
            o_ref[...] += down

    pl.when(j == 0)(lambda: step(True))
    pl.when(j > 0)(lambda: step(False))

    slab = x_ref.shape[1]
    for c in range(res_steps):
        @pl.when(j == c)
        def _():
            o_ref[:, c * slab:(c + 1) * slab] += x_ref[...]


def _ffn_call(h, x, w_gate, w_up, w_down, tm, tf):
    m, d = x.shape
    f = w_gate.shape[1]
    res_steps = FFN_RESIDUAL_STEPS
    assert f // tf >= res_steps
    return pl.pallas_call(
        functools.partial(_ffn_kernel, res_steps=res_steps),
        grid=(m // tm, f // tf),
        in_specs=[
            pl.BlockSpec((tm, d), lambda i, j: (i, 0)),
            pl.BlockSpec((tm, d // res_steps), lambda i, j: (i, jnp.minimum(j, res_steps - 1))),
            pl.BlockSpec((d, tf), lambda i, j: (0, j)),
            pl.BlockSpec((d, tf), lambda i, j: (0, j)),
            pl.BlockSpec((tf, d), lambda i, j: (j, 0)),
        ],
        out_specs=pl.BlockSpec((tm, d), lambda i, j: (i, 0)),
        out_shape=jax.ShapeDtypeStruct((m, d), F32),
        compiler_params=pltpu.CompilerParams(dimension_semantics=("parallel", "arbitrary"),
                                             vmem_limit_bytes=FFN_VMEM_LIMIT),
        name="ffn",
    )(h, x, w_gate, w_up, w_down)


def _rope_slot(t):
    z = jnp.zeros(t.shape[:-1] + (ROPE_HALF,), t.dtype)
    return jnp.concatenate([t[..., :ROPE_HALF], z, t[..., ROPE_HALF:], z], axis=-1)


def _head_slot(t):
    return jnp.concatenate([t[..., :QK_NOPE_DIM], _rope_slot(t[..., QK_NOPE_DIM:])], axis=-1)


def kernel(x, positions, attn_norm_g, w_in, b_gate, q_a_norm_g, w_q_b, kv_a_norm_g, w_kv_b,
           q_norm_g, k_norm_g, w_attn_o, w_pool_grp, pool_scale, w_pool_o, w_out,
           ffn_norm_g, w_ffn_gate, w_ffn_up, w_ffn_down):
    b, s, d = x.shape
    depth = w_in.shape[0]
    m = b * s
    pool_width = w_pool_o.shape[1]

    inv_freq = ROPE_THETA ** (-jnp.arange(ROPE_HALF, dtype=F32) / ROPE_HALF)
    ang_t = positions.astype(F32).reshape(1, m) * inv_freq.reshape(ROPE_HALF, 1)
    cos_t = jnp.cos(ang_t)
    sin_t = jnp.sin(ang_t)

    xf = x.reshape(m, d)
    for l in range(depth):
        g_lat = jnp.concatenate([q_a_norm_g[l], kv_a_norm_g[l]]).reshape(1, -1)
        wkv = w_kv_b[l].reshape(KV_LORA_RANK, N_HEADS, QK_NOPE_DIM + V_HEAD_DIM)
        wk = wkv[..., :QK_NOPE_DIM].reshape(KV_LORA_RANK, -1).astype(BF16)
        wvt = wkv[..., QK_NOPE_DIM:].reshape(KV_LORA_RANK, -1).T.astype(BF16)
        tq = ATTN_TILE
        gq = jnp.broadcast_to((q_norm_g[l] * (QK_HEAD_DIM ** -0.5 * LOG2_E)).reshape(-1, 1),
                              (QK_HEAD_DIM, tq))
        gk = _head_slot(k_norm_g[l]).reshape(1, HEAD_SLOT)
        g_attn_norm = attn_norm_g[l].reshape(1, d)

        lat, u_pool, gates, kr = _inproj_call(xf, g_attn_norm, w_in[l].T, g_lat,
                                              b_gate[l].reshape(1, -1), pool_width, tm=INPROJ_ROWS)
        qt, k, vt = _qkv_call(lat, kr, cos_t, sin_t, w_q_b[l], wk, wvt, gq, gk, tm=tq)
        attn, w_po, w_ao, w_o = _flash_call(qt.reshape(b, s // tq, -1, tq), k.reshape(b, s, -1),
                                            vt.reshape(b, s // tq, -1, tq),
                                            [w_pool_o[l], w_attn_o[l], w_out[l]],
                                            tq=tq, heads=ATTN_HEADS_PER_STEP)
        xf, h2 = _mix_call(xf, u_pool, attn.reshape(m, -1), gates, w_pool_grp[l].astype(BF16),
                           pool_scale[l].reshape(1, -1), w_po, w_ao, w_o,
                           ffn_norm_g[l].reshape(1, d), tm=MIX_ROWS, seq=s)
        xf = _ffn_call(h2, xf, w_ffn_gate[l], w_ffn_up[l], w_ffn_down[l], tm=FFN_ROWS, tf=FFN_COLS)
    return xf.reshape(b, s, d)
```

```python
import functools

import jax
import jax.numpy as jnp
from jax import lax
from jax.experimental import pallas as pl
from jax.experimental.pallas import tpu as pltpu

F32 = jnp.float32
BF16 = jnp.bfloat16

N_HEADS = 16
QK_NOPE_DIM = 128
QK_ROPE_DIM = 64
QK_HEAD_DIM = QK_NOPE_DIM + QK_ROPE_DIM
V_HEAD_DIM = 128
Q_LORA_RANK = 512
KV_LORA_RANK = 512
ROPE_THETA = 10000.0
POOL_WINDOWS = (2, 4, 8, 16)
EPS = 1e-6

LANES = 128
SUBLANES = 8
HEAD_SLOT = 2 * LANES
ROPE_HALF = QK_ROPE_DIM // 2
POOL_HALO = 16
VMEM_LIMIT = 56 * 1024 * 1024
BIG_VMEM_LIMIT = 62 * 1024 * 1024
FFN_RESIDUAL_STEPS = 8

INPROJ_ROWS = 1024
ATTN_TILE = 512
ATTN_HEADS_PER_STEP = 4
MIX_ROWS = 256
FFN_ROWS, FFN_COLS = 1024, 512
WEIGHT_RING = 3
SUM_ROWS = 16
LOG2_E = 1.4426950408889634
_CONTRACT_LAST = (((1,), (1,)), ((), ()))


def _params(*sem):
    return pltpu.CompilerParams(dimension_semantics=sem, vmem_limit_bytes=VMEM_LIMIT)


def _dot(a, b):
    return jnp.dot(a, b, preferred_element_type=F32)


def _sigmoid(x):
    return 0.5 * jnp.tanh(0.5 * x) + 0.5


def _silu(x):
    t = 0.5 * x
    return t + t * jnp.tanh(t)


def _rms_bf16(x, g):
    ms = jnp.mean(x * x, axis=-1, keepdims=True)
    return (x * lax.rsqrt(ms + EPS) * g).astype(BF16)


def _rope(t, cos, sin):
    return t * cos + pltpu.roll(t, 2 * ROPE_HALF, 1) * sin


def _dot_t(a, w):
    return lax.dot_general(a, w.astype(BF16), _CONTRACT_LAST, preferred_element_type=F32)


def _inproj_kernel(x_hbm, g_ref, w_hbm, gl_ref, b_ref, wkr_ref,
                   lat_ref, up_ref, gate_ref, kr_ref, h_ref, w_ring, w_sem, x_buf, x_sem,
                   *, n_lat, n_up, w_row, tn, tm):
    i = pl.program_id(0)
    j = pl.program_id(1)
    n_i = pl.num_programs(0)
    n_j = pl.num_programs(1)
    t = i * n_j + j
    last = n_i * n_j - 1

    def w_copy(step):
        slot = lax.rem(step, WEIGHT_RING)
        rows = pl.ds(w_row(lax.rem(step, n_j)), tn)
        return pltpu.make_async_copy(w_hbm.at[rows, :], w_ring.at[slot], w_sem.at[slot])

    def x_copy(tile):
        slot = lax.rem(tile, 2)
        rows = pl.ds(pl.multiple_of(tile * tm, tm), tm)
        return pltpu.make_async_copy(x_hbm.at[rows, :], x_buf.at[slot], x_sem.at[slot])

    @pl.when(t == 0)
    def _():
        x_copy(0).start()
        for step in range(WEIGHT_RING - 1):
            w_copy(step).start()

    @pl.when(t + WEIGHT_RING - 1 <= last)
    def _():
        w_copy(t + WEIGHT_RING - 1).start()

    @pl.when(jnp.logical_and(j == 1, i + 1 < n_i))
    def _():
        x_copy(i + 1).start()

    w_copy(t).wait()
    w_ref = w_ring.at[lax.rem(t, WEIGHT_RING)]

    @pl.when(j == 0)
    def _():
        x_copy(i).wait()
        h_ref[...] = _rms_bf16(x_buf[lax.rem(i, 2)], g_ref[...])
        wkr = wkr_ref[...]
        z = jnp.zeros((ROPE_HALF, wkr.shape[1]), F32)
        slot = jnp.concatenate([wkr[:ROPE_HALF], z, wkr[ROPE_HALF:], z], axis=0)
        kr_ref[...] = _dot_t(h_ref[...], slot)

    @pl.when(j < n_lat)
    def _():
        lat_ref[...] = _rms_bf16(_dot_t(h_ref[...], w_ref[...]), gl_ref[...])

    @pl.when(jnp.logical_and(j >= n_lat, j < n_lat + n_up))
    def _():
        up_ref[...] = _dot_t(h_ref[...], w_ref[...])

    @pl.when(j >= n_lat + n_up)
    def _():
        gate_ref[...] = _sigmoid(_dot_t(h_ref[...], w_ref[...]) + b_ref[...])


def _inproj_call(x, g, w_t, g_lat, b_gate, pool_width, tm):
    m, d = x.shape
    tn = Q_LORA_RANK
    n_lat = g_lat.shape[1] // tn
    n_up = pool_width // tn
    n_gate = b_gate.shape[1] // tn
    o_kr = n_lat * tn
    o_up = o_kr + QK_ROPE_DIM
    clamp = lambda j, lo, n: jnp.clip(j - lo, 0, n - 1)
    sub = SUBLANES
    w_row = lambda j: sub * jnp.where(j < n_lat, j * (tn // sub),
                                      o_up // sub + (j - n_lat) * (tn // sub))
    return pl.pallas_call(
        functools.partial(_inproj_kernel, n_lat=n_lat, n_up=n_up, w_row=w_row, tn=tn, tm=tm),
        grid=(m // tm, n_lat + n_up + n_gate),
        in_specs=[
            pl.BlockSpec(memory_space=pl.ANY),
            pl.BlockSpec((1, d), lambda i, j: (0, 0)),
            pl.BlockSpec(memory_space=pl.ANY),
            pl.BlockSpec((1, tn), lambda i, j: (0, clamp(j, 0, n_lat))),
            pl.BlockSpec((1, tn), lambda i, j: (0, clamp(j, n_lat + n_up, n_gate))),
            pl.BlockSpec((pl.Element(QK_ROPE_DIM), pl.Element(d)), lambda i, j: (o_kr, 0)),
        ],
        out_specs=[
            pl.BlockSpec((tm, tn), lambda i, j: (i, clamp(j, 0, n_lat))),
            pl.BlockSpec((tm, tn), lambda i, j: (i, clamp(j, n_lat, n_up))),
            pl.BlockSpec((tm, tn), lambda i, j: (i, clamp(j, n_lat + n_up, n_gate))),
            pl.BlockSpec((tm, LANES), lambda i, j: (i, 0)),
        ],
        out_shape=[
            jax.ShapeDtypeStruct((m, n_lat * tn), BF16),
            jax.ShapeDtypeStruct((m, n_up * tn), F32),
            jax.ShapeDtypeStruct((m, n_gate * tn), F32),
            jax.ShapeDtypeStruct((m, LANES), F32),
        ],
        scratch_shapes=[pltpu.VMEM((tm, d), BF16), pltpu.VMEM((WEIGHT_RING, tn, d), F32),
                        pltpu.SemaphoreType.DMA((WEIGHT_RING,)),
                        pltpu.VMEM((2, tm, d), F32), pltpu.SemaphoreType.DMA((2,))],
        compiler_params=_params("arbitrary", "arbitrary"),
        name="inproj",
    )(x, g, w_t, g_lat, b_gate, w_t)


def _pool_mixer(u, halo, t0, wg_ref, sc_ref):
    tm = u.shape[0]
    halo = jnp.where(t0 > 0, halo, 0.0)
    ext = jnp.concatenate([halo, u], axis=0)
    pos = lax.broadcasted_iota(jnp.int32, (tm, 1), 0) + t0
    gd = wg_ref.shape[1]
    out = []
    for g, w in enumerate(POOL_WINDOWS):
        cols = slice(g * gd, (g + 1) * gd)
        a = ext[:, cols]
        shift = 1
        while shift < w:
            a = a + pltpu.roll(a, shift, 0)
            shift *= 2
        count = jnp.minimum(pos + 1, w).astype(F32)
        pooled = a[POOL_HALO:, :] / count - u[:, cols]
        y = _dot(pooled.astype(BF16), wg_ref[g]) * sc_ref[:, cols]
        out.append(y.astype(BF16))
    return jnp.concatenate(out, axis=1)


def _qkv_kernel(cq_ref, ckv_ref, kr_ref, cos_ref, sin_ref, wq_ref, wk_ref, wvt_ref,
                gq_ref, gk_ref, qt_ref, k_ref, vt_ref):
    inv_dim = 1.0 / QK_HEAD_DIM
    half = ROPE_HALF
    cq = cq_ref[...]
    ckv = ckv_ref[...]
    cos_t = cos_ref[...]
    sin_t = sin_ref[...]
    qt = lax.dot_general(wq_ref[...].astype(BF16), cq, (((0,), (1,)), ((), ())),
                         preferred_element_type=F32)
    vt_ref[...] = lax.dot_general(wvt_ref[...], ckv, _CONTRACT_LAST,
                                  preferred_element_type=F32).astype(BF16)
    gq = gq_ref[...]
    zeros = jnp.zeros((half, qt.shape[1]), BF16)
    for h in range(N_HEADS):
        src = h * QK_HEAD_DIM
        dst = h * HEAD_SLOT
        qh = qt[src:src + QK_HEAD_DIM]
        inv = lax.rsqrt(jnp.sum(qh * qh, axis=0, keepdims=True) * inv_dim + EPS)
        qh = qh * inv * gq
        t1 = qh[QK_NOPE_DIM:QK_NOPE_DIM + half]
        t2 = qh[QK_NOPE_DIM + half:]
        qt_ref[dst:dst + QK_NOPE_DIM] = qh[:QK_NOPE_DIM].astype(BF16)
        qt_ref[dst + LANES:dst + LANES + half] = (t1 * cos_t - t2 * sin_t).astype(BF16)
        qt_ref[dst + LANES + half:dst + LANES + 2 * half] = zeros
        qt_ref[dst + LANES + 2 * half:dst + LANES + 3 * half] = (t2 * cos_t + t1 * sin_t).astype(BF16)
        qt_ref[dst + LANES + 3 * half:dst + HEAD_SLOT] = zeros

    cos_r = cos_t.T
    sin_r = sin_t.T
    cos = jnp.concatenate([cos_r] * 4, axis=1)
    sin = jnp.concatenate([-sin_r, -sin_r, sin_r, sin_r], axis=1)
    kv = _dot(ckv, wk_ref[...])
    gk_n, gk_r = gk_ref[:, :LANES], gk_ref[:, LANES:]
    kr = kr_ref[...]
    ss_kr = jnp.sum(kr * kr, axis=-1, keepdims=True)
    kr_rot = _rope(kr * gk_r, cos, sin)
    for h in range(N_HEADS):
        lo = h * HEAD_SLOT
        kn = kv[:, h * LANES:(h + 1) * LANES]
        ssk = jnp.sum(kn * kn, axis=-1, keepdims=True) + ss_kr
        invk = lax.rsqrt(ssk * inv_dim + EPS)
        k_ref[:, lo:lo + LANES] = (kn * invk * gk_n).astype(BF16)
        k_ref[:, lo + LANES:lo + HEAD_SLOT] = (kr_rot * invk).astype(BF16)


def _qkv_call(lat, kr, cos_t, sin_t, wq, wk, wvt, gq, gk, tm):
    m = lat.shape[0]
    rank = Q_LORA_RANK
    wide = N_HEADS * HEAD_SLOT
    vw = N_HEADS * V_HEAD_DIM
    row = lambda i: (i, 0)
    col = lambda i: (0, i)
    fixed = lambda i: (0, 0)
    return pl.pallas_call(
        _qkv_kernel,
        grid=(m // tm,),
        in_specs=[
            pl.BlockSpec((tm, rank), lambda i: (i, 0)),
            pl.BlockSpec((tm, rank), lambda i: (i, 1)),
            pl.BlockSpec((tm, LANES), row),
            pl.BlockSpec((ROPE_HALF, tm), col),
            pl.BlockSpec((ROPE_HALF, tm), col),
            pl.BlockSpec(wq.shape, fixed),
            pl.BlockSpec(wk.shape, fixed),
            pl.BlockSpec(wvt.shape, fixed),
            pl.BlockSpec(gq.shape, fixed),
            pl.BlockSpec((1, HEAD_SLOT), fixed),
        ],
        out_specs=[
            pl.BlockSpec((None, wide, tm), lambda i: (i, 0, 0)),
            pl.BlockSpec((tm, wide), row),
            pl.BlockSpec((None, vw, tm), lambda i: (i, 0, 0)),
        ],
        out_shape=[
            jax.ShapeDtypeStruct((m // tm, wide, tm), BF16),
            jax.ShapeDtypeStruct((m, wide), BF16),
            jax.ShapeDtypeStruct((m // tm, vw, tm), BF16),
        ],
        compiler_params=_params("parallel"),
        name="qkv",
    )(lat, lat, kr, cos_t, sin_t, wq, wk, wvt, gq, gk)


def _flash_kernel(q_ref, k_ref, vt_ref, *refs, tq, heads, n_cast):
    cast_src, (o_ref, *cast_dst), (m_ref, acc_ref, s_ref, mx_ref) = (
        refs[:n_cast], refs[n_cast:2 * n_cast + 1], refs[2 * n_cast + 1:])
    for src, dst in zip(cast_src, cast_dst):
        dst[...] = src[...].astype(BF16)
    i = pl.program_id(2)
    ones = jnp.ones((SUM_ROWS, tq), BF16)
    m_ref[...] = jnp.full(m_ref.shape, -jnp.inf, F32)
    acc_ref[...] = jnp.zeros(acc_ref.shape, F32)

    half = tq // 2
    head_cols = lambda h: slice(h * HEAD_SLOT, (h + 1) * HEAD_SLOT)
    chains = [(t, h) for t in range(2) for h in range(heads)]
    cid = lambda t, h: t * heads + h

    def v_rows(h, c):
        vt = vt_ref[c, h * V_HEAD_DIM:(h + 1) * V_HEAD_DIM, :]
        return jnp.concatenate([vt, ones], axis=0)

    def new_max(ch, buf):
        m_prev = m_ref[ch]
        m_new = jnp.maximum(m_prev, mx_ref[buf, ch])
        m_ref[ch] = m_new
        return m_new, jnp.exp2(m_prev - m_new)

    def scores(t, h, c, buf):
        start = pl.multiple_of(c * tq, tq)
        s = _dot(k_ref[pl.ds(start, tq), head_cols(h)], q_ref[t, head_cols(h), :])
        s_ref[buf, cid(t, h)] = s
        mx_ref[buf, cid(t, h)] = jnp.max(s, axis=0, keepdims=True)

    def accumulate(t, h, c, buf):
        ch = cid(t, h)
        m_new, alpha = new_max(ch, buf)
        p = jnp.exp2(s_ref[buf, ch] - m_new)
        acc_ref[ch] = alpha * acc_ref[ch] + _dot(v_rows(h, c), p.astype(BF16))

    def scores_diagonal(t, h, c, buf):
        ch = cid(t, h)
        start = pl.multiple_of(c * tq, tq)
        causal = (lax.broadcasted_iota(jnp.int32, (half, tq), 0)
                  <= lax.broadcasted_iota(jnp.int32, (half, tq), 1))
        q_t = q_ref[t, head_cols(h), :]
        s0 = _dot(k_ref[pl.ds(start, half), head_cols(h)], q_t)
        s1 = _dot(k_ref[pl.ds(start + half, half), head_cols(h)], q_t[:, half:])
        s0 = jnp.where(causal, s0, -jnp.inf)
        s1 = jnp.where(causal[:, :half], s1, -jnp.inf)
        s_ref[buf, ch, :half, :] = s0
        s_ref[buf, ch, half:, half:] = s1
        m0 = jnp.max(s0, axis=0, keepdims=True)
        m1 = jnp.max(s1, axis=0, keepdims=True)
        mx_ref[buf, ch] = jnp.concatenate([m0[:, :half], jnp.maximum(m0[:, half:], m1)], axis=1)

    def accumulate_diagonal(t, h, c, buf):
        ch = cid(t, h)
        m_new, alpha = new_max(ch, buf)
        p0 = jnp.exp2(s_ref[buf, ch, :half, :] - m_new).astype(BF16)
        p1 = jnp.exp2(s_ref[buf, ch, half:, half:] - m_new[:, half:]).astype(BF16)
        vt = v_rows(h, c)
        acc = alpha * acc_ref[ch] + _dot(vt[:, :half], p0)
        acc_ref[ch, :, :half] = acc[:, :half]
        acc_ref[ch, :, half:] = acc[:, half:] + _dot(vt[:, half:], p1)

    def stage(c, buf, nxt, now):
        for t, h in chains:
            if nxt[t] == FULL:
                scores(t, h, c + 1, 1 - buf)
            elif nxt[t] == DIAG:
                scores_diagonal(t, h, c + 1, 1 - buf)
            if now[t] == FULL:
                accumulate(t, h, c, buf)
            elif now[t] == DIAG:
                accumulate_diagonal(t, h, c, buf)

    FULL, DIAG = "full", "diagonal"
    both_full, a_diag, b_diag, nothing = (FULL, FULL), (DIAG, FULL), (None, DIAG), (None, None)
    a_tile = 2 * i

    @pl.when(i == 0)
    def _():
        for t, h in chains:
            (scores_diagonal if t == 0 else scores)(t, h, 0, 0)
        stage(0, 0, b_diag, a_diag)
        stage(1, 1, nothing, b_diag)

    @pl.when(i > 0)
    def _():
        for t, h in chains:
            scores(t, h, 0, 0)

    def two_stages(j, carry):
        stage(2 * j, 0, both_full, both_full)
        stage(2 * j + 1, 1, both_full, both_full)
        return carry

    lax.fori_loop(0, i - 1, two_stages, None)

    @pl.when(i > 0)
    def _():
        stage(a_tile - 2, 0, both_full, both_full)
        stage(a_tile - 1, 1, a_diag, both_full)
        stage(a_tile, 0, b_diag, a_diag)
        stage(a_tile + 1, 1, nothing, b_diag)

    for t, h in chains:
        ch = cid(t, h)
        out = acc_ref[ch, :V_HEAD_DIM, :] / acc_ref[ch, V_HEAD_DIM:V_HEAD_DIM + 1, :]
        o_ref[t * tq:(t + 1) * tq, h * V_HEAD_DIM:(h + 1) * V_HEAD_DIM] = out.T.astype(BF16)


def _flash_call(q, k, vt, cast_weights, tq, heads):
    b, s, _ = k.shape
    grid = (b, N_HEADS // heads, s // (2 * tq))
    n_steps = grid[0] * grid[1] * grid[2]
    step = lambda bi, h, i: ((bi * grid[1] + h) * grid[2] + i, 0)
    cast_specs = [pl.BlockSpec((w.shape[0] // n_steps, w.shape[1]), step) for w in cast_weights]
    return pl.pallas_call(
        functools.partial(_flash_kernel, tq=tq, heads=heads, n_cast=len(cast_weights)),
        grid=grid,
        in_specs=[
            pl.BlockSpec((None, 2, heads * HEAD_SLOT, tq), lambda bi, h, i: (bi, i, h, 0)),
            pl.BlockSpec((None, s, heads * HEAD_SLOT), lambda bi, h, i: (bi, 0, h)),
            pl.BlockSpec((None, s // tq, heads * V_HEAD_DIM, tq), lambda bi, h, i: (bi, 0, h, 0)),
        ] + cast_specs,
        out_specs=[pl.BlockSpec((None, 2 * tq, heads * V_HEAD_DIM), lambda bi, h, i: (bi, i, h))] + cast_specs,
        out_shape=[jax.ShapeDtypeStruct((b, s, N_HEADS * V_HEAD_DIM), BF16)]
        + [jax.ShapeDtypeStruct(w.shape, BF16) for w in cast_weights],
        scratch_shapes=[
            pltpu.VMEM((2 * heads, 1, tq), F32),
            pltpu.VMEM((2 * heads, V_HEAD_DIM + SUM_ROWS, tq), F32),
            pltpu.VMEM((2, 2 * heads, tq, tq), F32),
            pltpu.VMEM((2, 2 * heads, 1, tq), F32),
        ],
        compiler_params=pltpu.CompilerParams(dimension_semantics=("parallel", "parallel", "arbitrary"),
                                             vmem_limit_bytes=BIG_VMEM_LIMIT),
        name="flash",
    )(q, k, vt, *cast_weights)


def _mix_kernel(x_ref, u_ref, halo_ref, attn_ref, gp_ref, ga_ref, wg_ref, sc_ref,
                wpo_ref, wao_ref, wout_ref, g2_ref, o_ref, h2_ref, *, tm, seq):
    t0 = (pl.program_id(0) * tm) % seq
    pooled = _pool_mixer(u_ref[...], halo_ref[...], t0, wg_ref, sc_ref)
    y_pool = _dot(pooled, wpo_ref[...])
    y_attn = _dot(attn_ref[...], wao_ref[...])
    mixed = gp_ref[...] * y_pool + ga_ref[...] * y_attn
    x1 = x_ref[...] + _dot(mixed.astype(BF16), wout_ref[...])
    o_ref[...] = x1
    h2_ref[...] = _rms_bf16(x1, g2_ref[...])


def _mix_call(x, u_pool, attn, gates, w_grp, scale, w_po, w_ao, w_out, g_ffn, tm, seq):
    m, d = x.shape
    width = u_pool.shape[1]
    halo_blocks = tm // POOL_HALO
    row = lambda i: (i, 0)
    fixed = lambda i: (0, 0)
    return pl.pallas_call(
        functools.partial(_mix_kernel, tm=tm, seq=seq),
        grid=(m // tm,),
        in_specs=[
            pl.BlockSpec((tm, d), row),
            pl.BlockSpec((tm, width), row),
            pl.BlockSpec((POOL_HALO, width), lambda i: (jnp.maximum(i * halo_blocks - 1, 0), 0)),
            pl.BlockSpec((tm, attn.shape[1]), row),
            pl.BlockSpec((tm, d), lambda i: (i, 0)),
            pl.BlockSpec((tm, d), lambda i: (i, 1)),
            pl.BlockSpec(w_grp.shape, lambda i: (0, 0, 0)),
            pl.BlockSpec((1, width), fixed),
            pl.BlockSpec(w_po.shape, fixed),
            pl.BlockSpec(w_ao.shape, fixed),
            pl.BlockSpec(w_out.shape, fixed),
            pl.BlockSpec((1, d), fixed),
        ],
        out_specs=[pl.BlockSpec((tm, d), row), pl.BlockSpec((tm, d), row)],
        out_shape=[jax.ShapeDtypeStruct((m, d), F32), jax.ShapeDtypeStruct((m, d), BF16)],
        compiler_params=_params("parallel"),
        name="mix",
    )(x, u_pool, u_pool, attn, gates, gates, w_grp, scale, w_po, w_ao, w_out, g_ffn)


def _ffn_kernel(h_ref, x_ref, wg_ref, wu_ref, wd_ref, o_ref, *, res_steps):
    j = pl.program_id(1)

    def step(first):
        h = h_ref[...]
        a = _dot(h, wg_ref[...].astype(BF16))
        u = _dot(h, wu_ref[...].astype(BF16))
        act = _silu(a) * u
        down = _dot(act.astype(BF16), wd_ref[...].astype(BF16))
        if first:
            o_ref[...] = down
        else:
            o_ref[...] += down

    pl.when(j == 0)(lambda: step(True))
    pl.when(j > 0)(lambda: step(False))

    slab = x_ref.shape[1]
    for c in range(res_steps):
        @pl.when(j == c)
        def _():
            o_ref[:, c * slab:(c + 1) * slab] += x_ref[...]


def _ffn_call(h, x, w_gate, w_up, w_down, tm, tf):
    m, d = x.shape
    f = w_gate.shape[1]
    res_steps = FFN_RESIDUAL_STEPS
    assert f // tf >= res_steps
    return pl.pallas_call(
        functools.partial(_ffn_kernel, res_steps=res_steps),
        grid=(m // tm, f // tf),
        in_specs=[
            pl.BlockSpec((tm, d), lambda i, j: (i, 0)),
            pl.BlockSpec((tm, d // res_steps), lambda i, j: (i, jnp.minimum(j, res_steps - 1))),
            pl.BlockSpec((d, tf), lambda i, j: (0, j)),
            pl.BlockSpec((d, tf), lambda i, j: (0, j)),
            pl.BlockSpec((tf, d), lambda i, j: (j, 0)),
        ],
        out_specs=pl.BlockSpec((tm, d), lambda i, j: (i, 0)),
        out_shape=jax.ShapeDtypeStruct((m, d), F32),
        compiler_params=pltpu.CompilerParams(dimension_semantics=("parallel", "arbitrary"),
                                             vmem_limit_bytes=BIG_VMEM_LIMIT),
        name="ffn",
    )(h, x, w_gate, w_up, w_down)


def _rope_slot(t):
    z = jnp.zeros(t.shape[:-1] + (ROPE_HALF,), t.dtype)
    return jnp.concatenate([t[..., :ROPE_HALF], z, t[..., ROPE_HALF:], z], axis=-1)


def _head_slot(t):
    return jnp.concatenate([t[..., :QK_NOPE_DIM], _rope_slot(t[..., QK_NOPE_DIM:])], axis=-1)


def kernel(x, positions, attn_norm_g, w_in, b_gate, q_a_norm_g, w_q_b, kv_a_norm_g, w_kv_b,
           q_norm_g, k_norm_g, w_attn_o, w_pool_grp, pool_scale, w_pool_o, w_out,
           ffn_norm_g, w_ffn_gate, w_ffn_up, w_ffn_down):
    b, s, d = x.shape
    depth = w_in.shape[0]
    m = b * s
    pool_width = w_pool_o.shape[1]

    inv_freq = ROPE_THETA ** (-jnp.arange(ROPE_HALF, dtype=F32) / ROPE_HALF)
    ang_t = positions.astype(F32).reshape(1, m) * inv_freq.reshape(ROPE_HALF, 1)
    cos_t = jnp.cos(ang_t)
    sin_t = jnp.sin(ang_t)

    xf = x.reshape(m, d)
    for l in range(depth):
        g_lat = jnp.concatenate([q_a_norm_g[l], kv_a_norm_g[l]]).reshape(1, -1)
        wkv = w_kv_b[l].reshape(KV_LORA_RANK, N_HEADS, QK_NOPE_DIM + V_HEAD_DIM)
        wk = wkv[..., :QK_NOPE_DIM].reshape(KV_LORA_RANK, -1).astype(BF16)
        wvt = wkv[..., QK_NOPE_DIM:].reshape(KV_LORA_RANK, -1).T.astype(BF16)
        tq = ATTN_TILE
        gq = jnp.broadcast_to((q_norm_g[l] * (QK_HEAD_DIM ** -0.5 * LOG2_E)).reshape(-1, 1),
                              (QK_HEAD_DIM, tq))
        gk = _head_slot(k_norm_g[l]).reshape(1, HEAD_SLOT)
        g_attn_norm = attn_norm_g[l].reshape(1, d)

        lat, u_pool, gates, kr = _inproj_call(xf, g_attn_norm, w_in[l].T, g_lat,
                                              b_gate[l].reshape(1, -1), pool_width, tm=INPROJ_ROWS)
        qt, k, vt = _qkv_call(lat, kr, cos_t, sin_t, w_q_b[l], wk, wvt, gq, gk, tm=tq)
        attn, w_po, w_ao, w_o = _flash_call(qt.reshape(b, s // tq, -1, tq), k.reshape(b, s, -1),
                                            vt.reshape(b, s // tq, -1, tq),
                                            [w_pool_o[l], w_attn_o[l], w_out[l]],
                                            tq=tq, heads=ATTN_HEADS_PER_STEP)
        xf, h2 = _mix_call(xf, u_pool, attn.reshape(m, -1), gates, w_pool_grp[l].astype(BF16),
                           pool_scale[l].reshape(1, -1), w_po, w_ao, w_o,
                           ffn_norm_g[l].reshape(1, d), tm=MIX_ROWS, seq=s)
        xf = _ffn_call(h2, xf, w_ffn_gate[l], w_ffn_up[l], w_ffn_down[l], tm=FFN_ROWS, tf=FFN_COLS)
    return xf.reshape(b, s, d)
```

```python
import functools

import jax
import jax.numpy as jnp
from jax import lax
from jax.experimental import pallas as pl
from jax.experimental.pallas import tpu as pltpu

F32 = jnp.float32
BF16 = jnp.bfloat16

N_HEADS = 16
QK_NOPE_DIM = 128
QK_ROPE_DIM = 64
QK_HEAD_DIM = QK_NOPE_DIM + QK_ROPE_DIM
V_HEAD_DIM = 128
Q_LORA_RANK = 512
KV_LORA_RANK = 512
ROPE_THETA = 10000.0
POOL_WINDOWS = (2, 4, 8, 16)
EPS = 1e-6

LANES = 128
SUBLANES = 8
HEAD_SLOT = 2 * LANES
ROPE_HALF = QK_ROPE_DIM // 2
POOL_HALO = 16
VMEM_LIMIT = 56 * 1024 * 1024
BIG_VMEM_LIMIT = 62 * 1024 * 1024
FFN_RESIDUAL_STEPS = 8

INPROJ_ROWS = 1024
ATTN_TILE = 512
ATTN_HEADS_PER_STEP = 4
MIX_ROWS = 256
FFN_ROWS, FFN_COLS = 1024, 512
INPROJ_TILES_PER_STEP = 2
SUM_ROWS = 16
LOG2_E = 1.4426950408889634
_CONTRACT_LAST = (((1,), (1,)), ((), ()))


def _params(*sem):
    return pltpu.CompilerParams(dimension_semantics=sem, vmem_limit_bytes=VMEM_LIMIT)


def _dot(a, b):
    return jnp.dot(a, b, preferred_element_type=F32)


def _sigmoid(x):
    return 0.5 * jnp.tanh(0.5 * x) + 0.5


def _silu(x):
    t = 0.5 * x
    return t + t * jnp.tanh(t)


def _rms_bf16(x, g):
    ms = jnp.mean(x * x, axis=-1, keepdims=True)
    return (x * lax.rsqrt(ms + EPS) * g).astype(BF16)


def _rope(t, cos, sin):
    return t * cos + pltpu.roll(t, 2 * ROPE_HALF, 1) * sin


def _dot_t(a, w):
    return lax.dot_general(a, w.astype(BF16), _CONTRACT_LAST, preferred_element_type=F32)


def _inproj_kernel(x_hbm, g_ref, w_hbm, gl_ref, b_ref, wkr_ref,
                   lat_ref, up_ref, gate_ref, kr_ref, h_ref, w_ring, w_sem, x_buf, x_sem,
                   *, lat_steps, up_steps, w_row, tn, tm):
    per_step = INPROJ_TILES_PER_STEP
    i = pl.program_id(0)
    j = pl.program_id(1)
    n_i = pl.num_programs(0)
    n_j = pl.num_programs(1)
    t = i * n_j + j
    last = n_i * n_j - 1

    def w_copy(tile):
        slot = lax.rem(tile, 2 * per_step)
        rows = pl.ds(w_row(lax.rem(tile, n_j * per_step)), tn)
        return pltpu.make_async_copy(w_hbm.at[rows, :], w_ring.at[slot], w_sem.at[slot])

    def x_copy(tile):
        rows = pl.ds(pl.multiple_of(tile * tm, tm), tm)
        return pltpu.make_async_copy(x_hbm.at[rows, :], x_buf, x_sem.at[0])

    @pl.when(t == 0)
    def _():
        x_copy(0).start()
        for k in range(per_step):
            w_copy(k).start()

    @pl.when(t < last)
    def _():
        for k in range(per_step):
            w_copy((t + 1) * per_step + k).start()

    @pl.when(jnp.logical_and(j == 1, i + 1 < n_i))
    def _():
        x_copy(i + 1).start()

    w_tiles = []
    for k in range(per_step):
        w_copy(t * per_step + k).wait()
        w_tiles.append(w_ring.at[lax.rem(t * per_step + k, 2 * per_step)])
    cols = [slice(k * tn, (k + 1) * tn) for k in range(per_step)]

    @pl.when(j == 0)
    def _():
        x_copy(i).wait()
        h_ref[...] = _rms_bf16(x_buf[...], g_ref[...])
        wkr = wkr_ref[...]
        z = jnp.zeros((ROPE_HALF, wkr.shape[1]), F32)
        slot = jnp.concatenate([wkr[:ROPE_HALF], z, wkr[ROPE_HALF:], z], axis=0)
        kr_ref[...] = _dot_t(h_ref[...], slot)

    @pl.when(j < lat_steps)
    def _():
        for w_ref, c in zip(w_tiles, cols):
            lat_ref[:, c] = _rms_bf16(_dot_t(h_ref[...], w_ref[...]), gl_ref[:, c])

    @pl.when(jnp.logical_and(j >= lat_steps, j < lat_steps + up_steps))
    def _():
        for w_ref, c in zip(w_tiles, cols):
            up_ref[:, c] = _dot_t(h_ref[...], w_ref[...])

    @pl.when(j >= lat_steps + up_steps)
    def _():
        for w_ref, c in zip(w_tiles, cols):
            gate_ref[:, c] = _sigmoid(_dot_t(h_ref[...], w_ref[...]) + b_ref[:, c])


def _inproj_call(x, g, w_t, g_lat, b_gate, pool_width, tm):
    m, d = x.shape
    tn = Q_LORA_RANK
    per_step = INPROJ_TILES_PER_STEP
    wide = per_step * tn
    lat_steps = g_lat.shape[1] // wide
    up_steps = pool_width // wide
    gate_steps = b_gate.shape[1] // wide
    assert (lat_steps * wide, up_steps * wide, gate_steps * wide) == (g_lat.shape[1], pool_width, b_gate.shape[1])
    n_lat = lat_steps * per_step
    o_kr = n_lat * tn
    o_up = o_kr + QK_ROPE_DIM
    clamp = lambda j, lo, n: jnp.clip(j - lo, 0, n - 1)
    sub = SUBLANES
    w_row = lambda tile: sub * jnp.where(tile < n_lat, tile * (tn // sub),
                                         o_up // sub + (tile - n_lat) * (tn // sub))
    return pl.pallas_call(
        functools.partial(_inproj_kernel, lat_steps=lat_steps, up_steps=up_steps, w_row=w_row, tn=tn, tm=tm),
        grid=(m // tm, lat_steps + up_steps + gate_steps),
        in_specs=[
            pl.BlockSpec(memory_space=pl.ANY),
            pl.BlockSpec((1, d), lambda i, j: (0, 0)),
            pl.BlockSpec(memory_space=pl.ANY),
            pl.BlockSpec((1, wide), lambda i, j: (0, clamp(j, 0, lat_steps))),
            pl.BlockSpec((1, wide), lambda i, j: (0, clamp(j, lat_steps + up_steps, gate_steps))),
            pl.BlockSpec((pl.Element(QK_ROPE_DIM), pl.Element(d)), lambda i, j: (o_kr, 0)),
        ],
        out_specs=[
            pl.BlockSpec((tm, wide), lambda i, j: (i, clamp(j, 0, lat_steps))),
            pl.BlockSpec((tm, wide), lambda i, j: (i, clamp(j, lat_steps, up_steps))),
            pl.BlockSpec((tm, wide), lambda i, j: (i, clamp(j, lat_steps + up_steps, gate_steps))),
            pl.BlockSpec((tm, LANES), lambda i, j: (i, 0)),
        ],
        out_shape=[
            jax.ShapeDtypeStruct((m, lat_steps * wide), BF16),
            jax.ShapeDtypeStruct((m, up_steps * wide), F32),
            jax.ShapeDtypeStruct((m, gate_steps * wide), F32),
            jax.ShapeDtypeStruct((m, LANES), F32),
        ],
        scratch_shapes=[pltpu.VMEM((tm, d), BF16), pltpu.VMEM((2 * per_step, tn, d), F32),
                        pltpu.SemaphoreType.DMA((2 * per_step,)),
                        pltpu.VMEM((tm, d), F32), pltpu.SemaphoreType.DMA((1,))],
        compiler_params=_params("arbitrary", "arbitrary"),
        name="inproj",
    )(x, g, w_t, g_lat, b_gate, w_t)


def _pool_mixer(u, halo, t0, wg_ref, sc_ref):
    tm = u.shape[0]
    halo = jnp.where(t0 > 0, halo, 0.0)
    ext = jnp.concatenate([halo, u], axis=0)
    pos = lax.broadcasted_iota(jnp.int32, (tm, 1), 0) + t0
    gd = wg_ref.shape[1]
    out = []
    for g, w in enumerate(POOL_WINDOWS):
        cols = slice(g * gd, (g + 1) * gd)
        a = ext[:, cols]
        shift = 1
        while shift < w:
            a = a + pltpu.roll(a, shift, 0)
            shift *= 2
        count = jnp.minimum(pos + 1, w).astype(F32)
        pooled = a[POOL_HALO:, :] / count - u[:, cols]
        y = _dot(pooled.astype(BF16), wg_ref[g]) * sc_ref[:, cols]
        out.append(y.astype(BF16))
    return jnp.concatenate(out, axis=1)


def _qkv_kernel(cq_ref, ckv_ref, kr_ref, cos_ref, sin_ref, wq_ref, wk_ref, wvt_ref,
                gq_ref, gk_ref, qt_ref, k_ref, vt_ref):
    inv_dim = 1.0 / QK_HEAD_DIM
    half = ROPE_HALF
    cq = cq_ref[...]
    ckv = ckv_ref[...]
    cos_t = cos_ref[...]
    sin_t = sin_ref[...]
    qt = lax.dot_general(wq_ref[...].astype(BF16), cq, (((0,), (1,)), ((), ())),
                         preferred_element_type=F32)
    vt_ref[...] = lax.dot_general(wvt_ref[...], ckv, _CONTRACT_LAST,
                                  preferred_element_type=F32).astype(BF16)
    gq = gq_ref[...]
    zeros = jnp.zeros((half, qt.shape[1]), BF16)
    for h in range(N_HEADS):
        src = h * QK_HEAD_DIM
        dst = h * HEAD_SLOT
        qh = qt[src:src + QK_HEAD_DIM]
        inv = lax.rsqrt(jnp.sum(qh * qh, axis=0, keepdims=True) * inv_dim + EPS)
        qh = qh * inv * gq
        t1 = qh[QK_NOPE_DIM:QK_NOPE_DIM + half]
        t2 = qh[QK_NOPE_DIM + half:]
        qt_ref[dst:dst + QK_NOPE_DIM] = qh[:QK_NOPE_DIM].astype(BF16)
        qt_ref[dst + LANES:dst + LANES + half] = (t1 * cos_t - t2 * sin_t).astype(BF16)
        qt_ref[dst + LANES + half:dst + LANES + 2 * half] = zeros
        qt_ref[dst + LANES + 2 * half:dst + LANES + 3 * half] = (t2 * cos_t + t1 * sin_t).astype(BF16)
        qt_ref[dst + LANES + 3 * half:dst + HEAD_SLOT] = zeros

    cos_r = cos_t.T
    sin_r = sin_t.T
    cos = jnp.concatenate([cos_r] * 4, axis=1)
    sin = jnp.concatenate([-sin_r, -sin_r, sin_r, sin_r], axis=1)
    kv = _dot(ckv, wk_ref[...])
    gk_n, gk_r = gk_ref[:, :LANES], gk_ref[:, LANES:]
    kr = kr_ref[...]
    ss_kr = jnp.sum(kr * kr, axis=-1, keepdims=True)
    kr_rot = _rope(kr * gk_r, cos, sin)
    for h in range(N_HEADS):
        lo = h * HEAD_SLOT
        kn = kv[:, h * LANES:(h + 1) * LANES]
        ssk = jnp.sum(kn * kn, axis=-1, keepdims=True) + ss_kr
        invk = lax.rsqrt(ssk * inv_dim + EPS)
        k_ref[:, lo:lo + LANES] = (kn * invk * gk_n).astype(BF16)
        k_ref[:, lo + LANES:lo + HEAD_SLOT] = (kr_rot * invk).astype(BF16)


def _qkv_call(lat, kr, cos_t, sin_t, wq, wk, wvt, gq, gk, tm):
    m = lat.shape[0]
    rank = Q_LORA_RANK
    wide = N_HEADS * HEAD_SLOT
    vw = N_HEADS * V_HEAD_DIM
    row = lambda i: (i, 0)
    col = lambda i: (0, i)
    fixed = lambda i: (0, 0)
    return pl.pallas_call(
        _qkv_kernel,
        grid=(m // tm,),
        in_specs=[
            pl.BlockSpec((tm, rank), lambda i: (i, 0)),
            pl.BlockSpec((tm, rank), lambda i: (i, 1)),
            pl.BlockSpec((tm, LANES), row),
            pl.BlockSpec((ROPE_HALF, tm), col),
            pl.BlockSpec((ROPE_HALF, tm), col),
            pl.BlockSpec(wq.shape, fixed),
            pl.BlockSpec(wk.shape, fixed),
            pl.BlockSpec(wvt.shape, fixed),
            pl.BlockSpec(gq.shape, fixed),
            pl.BlockSpec((1, HEAD_SLOT), fixed),
        ],
        out_specs=[
            pl.BlockSpec((None, wide, tm), lambda i: (i, 0, 0)),
            pl.BlockSpec((tm, wide), row),
            pl.BlockSpec((None, vw, tm), lambda i: (i, 0, 0)),
        ],
        out_shape=[
            jax.ShapeDtypeStruct((m // tm, wide, tm), BF16),
            jax.ShapeDtypeStruct((m, wide), BF16),
            jax.ShapeDtypeStruct((m // tm, vw, tm), BF16),
        ],
        compiler_params=_params("parallel"),
        name="qkv",
    )(lat, lat, kr, cos_t, sin_t, wq, wk, wvt, gq, gk)


def _flash_kernel(q_ref, k_ref, vt_ref, *refs, tq, heads, n_cast):
    cast_src, (o_ref, *cast_dst), (m_ref, acc_ref, s_ref, mx_ref) = (
        refs[:n_cast], refs[n_cast:2 * n_cast + 1], refs[2 * n_cast + 1:])
    for src, dst in zip(cast_src, cast_dst):
        dst[...] = src[...].astype(BF16)
    i = pl.program_id(2)
    ones = jnp.ones((SUM_ROWS, tq), BF16)
    m_ref[...] = jnp.full(m_ref.shape, -jnp.inf, F32)
    acc_ref[...] = jnp.zeros(acc_ref.shape, F32)

    half = tq // 2
    head_cols = lambda h: slice(h * HEAD_SLOT, (h + 1) * HEAD_SLOT)
    chains = [(t, h) for t in range(2) for h in range(heads)]
    cid = lambda t, h: t * heads + h

    def v_rows(h, c):
        vt = vt_ref[c, h * V_HEAD_DIM:(h + 1) * V_HEAD_DIM, :]
        return jnp.concatenate([vt, ones], axis=0)

    def new_max(ch, buf):
        m_prev = m_ref[ch]
        m_new = jnp.maximum(m_prev, mx_ref[buf, ch])
        m_ref[ch] = m_new
        return m_new, jnp.exp2(m_prev - m_new)

    def scores(t, h, c, buf):
        start = pl.multiple_of(c * tq, tq)
        s = _dot(k_ref[pl.ds(start, tq), head_cols(h)], q_ref[t, head_cols(h), :])
        s_ref[buf, cid(t, h)] = s
        mx_ref[buf, cid(t, h)] = jnp.max(s, axis=0, keepdims=True)

    def accumulate(t, h, c, buf):
        ch = cid(t, h)
        m_new, alpha = new_max(ch, buf)
        p = jnp.exp2(s_ref[buf, ch] - m_new)
        acc_ref[ch] = alpha * acc_ref[ch] + _dot(v_rows(h, c), p.astype(BF16))

    def scores_diagonal(t, h, c, buf):
        ch = cid(t, h)
        start = pl.multiple_of(c * tq, tq)
        causal = (lax.broadcasted_iota(jnp.int32, (half, tq), 0)
                  <= lax.broadcasted_iota(jnp.int32, (half, tq), 1))
        q_t = q_ref[t, head_cols(h), :]
        s0 = _dot(k_ref[pl.ds(start, half), head_cols(h)], q_t)
        s1 = _dot(k_ref[pl.ds(start + half, half), head_cols(h)], q_t[:, half:])
        s0 = jnp.where(causal, s0, -jnp.inf)
        s1 = jnp.where(causal[:, :half], s1, -jnp.inf)
        s_ref[buf, ch, :half, :] = s0
        s_ref[buf, ch, half:, half:] = s1
        m0 = jnp.max(s0, axis=0, keepdims=True)
        m1 = jnp.max(s1, axis=0, keepdims=True)
        mx_ref[buf, ch] = jnp.concatenate([m0[:, :half], jnp.maximum(m0[:, half:], m1)], axis=1)

    def accumulate_diagonal(t, h, c, buf):
        ch = cid(t, h)
        m_new, alpha = new_max(ch, buf)
        p0 = jnp.exp2(s_ref[buf, ch, :half, :] - m_new).astype(BF16)
        p1 = jnp.exp2(s_ref[buf, ch, half:, half:] - m_new[:, half:]).astype(BF16)
        vt = v_rows(h, c)
        acc = alpha * acc_ref[ch] + _dot(vt[:, :half], p0)
        acc_ref[ch, :, :half] = acc[:, :half]
        acc_ref[ch, :, half:] = acc[:, half:] + _dot(vt[:, half:], p1)

    def stage(c, buf, nxt, now):
        for t, h in chains:
            if nxt[t] == FULL:
                scores(t, h, c + 1, 1 - buf)
            elif nxt[t] == DIAG:
                scores_diagonal(t, h, c + 1, 1 - buf)
            if now[t] == FULL:
                accumulate(t, h, c, buf)
            elif now[t] == DIAG:
                accumulate_diagonal(t, h, c, buf)

    FULL, DIAG = "full", "diagonal"
    both_full, a_diag, b_diag, nothing = (FULL, FULL), (DIAG, FULL), (None, DIAG), (None, None)
    a_tile = 2 * i

    @pl.when(i == 0)
    def _():
        for t, h in chains:
            (scores_diagonal if t == 0 else scores)(t, h, 0, 0)
        stage(0, 0, b_diag, a_diag)
        stage(1, 1, nothing, b_diag)

    @pl.when(i > 0)
    def _():
        for t, h in chains:
            scores(t, h, 0, 0)

    def two_stages(j, carry):
        stage(2 * j, 0, both_full, both_full)
        stage(2 * j + 1, 1, both_full, both_full)
        return carry

    lax.fori_loop(0, i - 1, two_stages, None)

    @pl.when(i > 0)
    def _():
        stage(a_tile - 2, 0, both_full, both_full)
        stage(a_tile - 1, 1, a_diag, both_full)
        stage(a_tile, 0, b_diag, a_diag)
        stage(a_tile + 1, 1, nothing, b_diag)

    for t, h in chains:
        ch = cid(t, h)
        out = acc_ref[ch, :V_HEAD_DIM, :] / acc_ref[ch, V_HEAD_DIM:V_HEAD_DIM + 1, :]
        o_ref[t * tq:(t + 1) * tq, h * V_HEAD_DIM:(h + 1) * V_HEAD_DIM] = out.T.astype(BF16)


def _flash_call(q, k, vt, cast_weights, tq, heads):
    b, s, _ = k.shape
    grid = (b, N_HEADS // heads, s // (2 * tq))
    n_steps = grid[0] * grid[1] * grid[2]
    step = lambda bi, h, i: ((bi * grid[1] + h) * grid[2] + i, 0)
    cast_specs = [pl.BlockSpec((w.shape[0] // n_steps, w.shape[1]), step) for w in cast_weights]
    return pl.pallas_call(
        functools.partial(_flash_kernel, tq=tq, heads=heads, n_cast=len(cast_weights)),
        grid=grid,
        in_specs=[
            pl.BlockSpec((None, 2, heads * HEAD_SLOT, tq), lambda bi, h, i: (bi, i, h, 0)),
            pl.BlockSpec((None, s, heads * HEAD_SLOT), lambda bi, h, i: (bi, 0, h)),
            pl.BlockSpec((None, s // tq, heads * V_HEAD_DIM, tq), lambda bi, h, i: (bi, 0, h, 0)),
        ] + cast_specs,
        out_specs=[pl.BlockSpec((None, 2 * tq, heads * V_HEAD_DIM), lambda bi, h, i: (bi, i, h))] + cast_specs,
        out_shape=[jax.ShapeDtypeStruct((b, s, N_HEADS * V_HEAD_DIM), BF16)]
        + [jax.ShapeDtypeStruct(w.shape, BF16) for w in cast_weights],
        scratch_shapes=[
            pltpu.VMEM((2 * heads, 1, tq), F32),
            pltpu.VMEM((2 * heads, V_HEAD_DIM + SUM_ROWS, tq), F32),
            pltpu.VMEM((2, 2 * heads, tq, tq), F32),
            pltpu.VMEM((2, 2 * heads, 1, tq), F32),
        ],
        compiler_params=pltpu.CompilerParams(dimension_semantics=("parallel", "parallel", "arbitrary"),
                                             vmem_limit_bytes=BIG_VMEM_LIMIT),
        name="flash",
    )(q, k, vt, *cast_weights)


def _mix_kernel(x_ref, u_ref, halo_ref, attn_ref, gp_ref, ga_ref, wg_ref, sc_ref,
                wpo_ref, wao_ref, wout_ref, g2_ref, o_ref, h2_ref, *, tm, seq):
    t0 = (pl.program_id(0) * tm) % seq
    pooled = _pool_mixer(u_ref[...], halo_ref[...], t0, wg_ref, sc_ref)
    y_pool = _dot(pooled, wpo_ref[...])
    y_attn = _dot(attn_ref[...], wao_ref[...])
    mixed = gp_ref[...] * y_pool + ga_ref[...] * y_attn
    x1 = x_ref[...] + _dot(mixed.astype(BF16), wout_ref[...])
    o_ref[...] = x1
    h2_ref[...] = _rms_bf16(x1, g2_ref[...])


def _mix_call(x, u_pool, attn, gates, w_grp, scale, w_po, w_ao, w_out, g_ffn, tm, seq):
    m, d = x.shape
    width = u_pool.shape[1]
    halo_blocks = tm // POOL_HALO
    row = lambda i: (i, 0)
    fixed = lambda i: (0, 0)
    return pl.pallas_call(
        functools.partial(_mix_kernel, tm=tm, seq=seq),
        grid=(m // tm,),
        in_specs=[
            pl.BlockSpec((tm, d), row),
            pl.BlockSpec((tm, width), row),
            pl.BlockSpec((POOL_HALO, width), lambda i: (jnp.maximum(i * halo_blocks - 1, 0), 0)),
            pl.BlockSpec((tm, attn.shape[1]), row),
            pl.BlockSpec((tm, d), lambda i: (i, 0)),
            pl.BlockSpec((tm, d), lambda i: (i, 1)),
            pl.BlockSpec(w_grp.shape, lambda i: (0, 0, 0)),
            pl.BlockSpec((1, width), fixed),
            pl.BlockSpec(w_po.shape, fixed),
            pl.BlockSpec(w_ao.shape, fixed),
            pl.BlockSpec(w_out.shape, fixed),
            pl.BlockSpec((1, d), fixed),
        ],
        out_specs=[pl.BlockSpec((tm, d), row), pl.BlockSpec((tm, d), row)],
        out_shape=[jax.ShapeDtypeStruct((m, d), F32), jax.ShapeDtypeStruct((m, d), BF16)],
        compiler_params=_params("parallel"),
        name="mix",
    )(x, u_pool, u_pool, attn, gates, gates, w_grp, scale, w_po, w_ao, w_out, g_ffn)


def _ffn_kernel(h_ref, x_ref, wg_ref, wu_ref, wd_ref, o_ref, *, res_steps):
    j = pl.program_id(1)

    def step(first):
        h = h_ref[...]
        a = _dot(h, wg_ref[...].astype(BF16))
        u = _dot(h, wu_ref[...].astype(BF16))
        act = _silu(a) * u
        down = _dot(act.astype(BF16), wd_ref[...].astype(BF16))
        if first:
            o_ref[...] = down
        else:
            o_ref[...] += down

    pl.when(j == 0)(lambda: step(True))
    pl.when(j > 0)(lambda: step(False))

    slab = x_ref.shape[1]
    for c in range(res_steps):
        @pl.when(j == c)
        def _():
            o_ref[:, c * slab:(c + 1) * slab] += x_ref[...]


def _ffn_call(h, x, w_gate, w_up, w_down, tm, tf):
    m, d = x.shape
    f = w_gate.shape[1]
    res_steps = FFN_RESIDUAL_STEPS
    assert f // tf >= res_steps
    return pl.pallas_call(
        functools.partial(_ffn_kernel, res_steps=res_steps),
        grid=(m // tm, f // tf),
        in_specs=[
            pl.BlockSpec((tm, d), lambda i, j: (i, 0)),
            pl.BlockSpec((tm, d // res_steps), lambda i, j: (i, jnp.minimum(j, res_steps - 1))),
            pl.BlockSpec((d, tf), lambda i, j: (0, j)),
            pl.BlockSpec((d, tf), lambda i, j: (0, j)),
            pl.BlockSpec((tf, d), lambda i, j: (j, 0)),
        ],
        out_specs=pl.BlockSpec((tm, d), lambda i, j: (i, 0)),
        out_shape=jax.ShapeDtypeStruct((m, d), F32),
        compiler_params=pltpu.CompilerParams(dimension_semantics=("parallel", "arbitrary"),
                                             vmem_limit_bytes=BIG_VMEM_LIMIT),
        name="ffn",
    )(h, x, w_gate, w_up, w_down)


def _rope_slot(t):
    z = jnp.zeros(t.shape[:-1] + (ROPE_HALF,), t.dtype)
    return jnp.concatenate([t[..., :ROPE_HALF], z, t[..., ROPE_HALF:], z], axis=-1)


def _head_slot(t):
    return jnp.concatenate([t[..., :QK_NOPE_DIM], _rope_slot(t[..., QK_NOPE_DIM:])], axis=-1)


def kernel(x, positions, attn_norm_g, w_in, b_gate, q_a_norm_g, w_q_b, kv_a_norm_g, w_kv_b,
           q_norm_g, k_norm_g, w_attn_o, w_pool_grp, pool_scale, w_pool_o, w_out,
           ffn_norm_g, w_ffn_gate, w_ffn_up, w_ffn_down):
    b, s, d = x.shape
    depth = w_in.shape[0]
    m = b * s
    pool_width = w_pool_o.shape[1]

    inv_freq = ROPE_THETA ** (-jnp.arange(ROPE_HALF, dtype=F32) / ROPE_HALF)
    ang_t = positions.astype(F32).reshape(1, m) * inv_freq.reshape(ROPE_HALF, 1)
    cos_t = jnp.cos(ang_t)
    sin_t = jnp.sin(ang_t)

    xf = x.reshape(m, d)
    for l in range(depth):
        g_lat = jnp.concatenate([q_a_norm_g[l], kv_a_norm_g[l]]).reshape(1, -1)
        wkv = w_kv_b[l].reshape(KV_LORA_RANK, N_HEADS, QK_NOPE_DIM + V_HEAD_DIM)
        wk = wkv[..., :QK_NOPE_DIM].reshape(KV_LORA_RANK, -1).astype(BF16)
        wvt = wkv[..., QK_NOPE_DIM:].reshape(KV_LORA_RANK, -1).T.astype(BF16)
        tq = ATTN_TILE
        gq = jnp.broadcast_to((q_norm_g[l] * (QK_HEAD_DIM ** -0.5 * LOG2_E)).reshape(-1, 1),
                              (QK_HEAD_DIM, tq))
        gk = _head_slot(k_norm_g[l]).reshape(1, HEAD_SLOT)
        g_attn_norm = attn_norm_g[l].reshape(1, d)

        lat, u_pool, gates, kr = _inproj_call(xf, g_attn_norm, w_in[l].T, g_lat,
                                              b_gate[l].reshape(1, -1), pool_width, tm=INPROJ_ROWS)
        qt, k, vt = _qkv_call(lat, kr, cos_t, sin_t, w_q_b[l], wk, wvt, gq, gk, tm=tq)
        attn, w_po, w_ao, w_o = _flash_call(qt.reshape(b, s // tq, -1, tq), k.reshape(b, s, -1),
                                            vt.reshape(b, s // tq, -1, tq),
                                            [w_pool_o[l], w_attn_o[l], w_out[l]],
                                            tq=tq, heads=ATTN_HEADS_PER_STEP)
        xf, h2 = _mix_call(xf, u_pool, attn.reshape(m, -1), gates, w_pool_grp[l].astype(BF16),
                           pool_scale[l].reshape(1, -1), w_po, w_ao, w_o,
                           ffn_norm_g[l].reshape(1, d), tm=MIX_ROWS, seq=s)
        xf = _ffn_call(h2, xf, w_ffn_gate[l], w_ffn_up[l], w_ffn_down[l], tm=FFN_ROWS, tf=FFN_COLS)
    return xf.reshape(b, s, d)
```

```python
import functools

import jax
import jax.numpy as jnp
from jax import lax
from jax.experimental import pallas as pl
from jax.experimental.pallas import tpu as pltpu

F32 = jnp.float32
BF16 = jnp.bfloat16

N_HEADS = 16
QK_NOPE_DIM = 128
QK_ROPE_DIM = 64
QK_HEAD_DIM = QK_NOPE_DIM + QK_ROPE_DIM
V_HEAD_DIM = 128
Q_LORA_RANK = 512
KV_LORA_RANK = 512
ROPE_THETA = 10000.0
POOL_WINDOWS = (2, 4, 8, 16)
EPS = 1e-6

LANES = 128
SUBLANES = 8
HEAD_SLOT = 2 * LANES
ROPE_HALF = QK_ROPE_DIM // 2
POOL_HALO = 16
VMEM_LIMIT = 56 * 1024 * 1024
BIG_VMEM_LIMIT = 62 * 1024 * 1024
FFN_RESIDUAL_STEPS = 8

INPROJ_ROWS = 1024
ATTN_TILE = 512
ATTN_HEADS_PER_STEP = 4
MIX_ROWS = 256
FFN_ROWS, FFN_COLS = 1024, 512
INPROJ_TILES_PER_STEP = 2
SUM_ROWS = 16
LOG2_E = 1.4426950408889634
_CONTRACT_LAST = (((1,), (1,)), ((), ()))


def _params(*sem):
    return pltpu.CompilerParams(dimension_semantics=sem, vmem_limit_bytes=VMEM_LIMIT)


def _dot(a, b):
    return jnp.dot(a, b, preferred_element_type=F32)


def _sigmoid(x):
    return 0.5 * jnp.tanh(0.5 * x) + 0.5


def _silu(x):
    t = 0.5 * x
    return t + t * jnp.tanh(t)


def _rms_bf16(x, g):
    ms = jnp.mean(x * x, axis=-1, keepdims=True)
    return (x * lax.rsqrt(ms + EPS) * g).astype(BF16)


def _rope(t, cos, sin):
    return t * cos + pltpu.roll(t, 2 * ROPE_HALF, 1) * sin


def _dot_t(a, w):
    return lax.dot_general(a, w.astype(BF16), _CONTRACT_LAST, preferred_element_type=F32)


def _inproj_kernel(x_hbm, g_ref, w_hbm, gl_ref, b_ref, wkr_ref,
                   lat_ref, up_ref, gate_ref, kr_ref, h_ref, w_ring, w_sem, x_buf, x_sem,
                   *, lat_steps, up_steps, w_row, tn, tm):
    per_step = INPROJ_TILES_PER_STEP
    i = pl.program_id(0)
    j = pl.program_id(1)
    n_i = pl.num_programs(0)
    n_j = pl.num_programs(1)
    t = i * n_j + j
    last = n_i * n_j - 1

    def w_copy(tile):
        slot = lax.rem(tile, 2 * per_step)
        rows = pl.ds(w_row(lax.rem(tile, n_j * per_step)), tn)
        return pltpu.make_async_copy(w_hbm.at[rows, :], w_ring.at[slot], w_sem.at[slot])

    def x_copy(tile):
        rows = pl.ds(pl.multiple_of(tile * tm, tm), tm)
        return pltpu.make_async_copy(x_hbm.at[rows, :], x_buf, x_sem.at[0])

    @pl.when(t == 0)
    def _():
        x_copy(0).start()
        for k in range(per_step):
            w_copy(k).start()

    @pl.when(t < last)
    def _():
        for k in range(per_step):
            w_copy((t + 1) * per_step + k).start()

    @pl.when(jnp.logical_and(j == 1, i + 1 < n_i))
    def _():
        x_copy(i + 1).start()

    w_tiles = []
    for k in range(per_step):
        w_copy(t * per_step + k).wait()
        w_tiles.append(w_ring.at[lax.rem(t * per_step + k, 2 * per_step)])
    cols = [slice(k * tn, (k + 1) * tn) for k in range(per_step)]

    @pl.when(j == 0)
    def _():
        x_copy(i).wait()
        h_ref[...] = _rms_bf16(x_buf[...], g_ref[...])
        wkr = wkr_ref[...]
        z = jnp.zeros((ROPE_HALF, wkr.shape[1]), F32)
        slot = jnp.concatenate([wkr[:ROPE_HALF], z, wkr[ROPE_HALF:], z], axis=0)
        kr_ref[...] = _dot_t(h_ref[...], slot)

    @pl.when(j < lat_steps)
    def _():
        for w_ref, c in zip(w_tiles, cols):
            lat_ref[:, c] = _rms_bf16(_dot_t(h_ref[...], w_ref[...]), gl_ref[:, c])

    @pl.when(jnp.logical_and(j >= lat_steps, j < lat_steps + up_steps))
    def _():
        for w_ref, c in zip(w_tiles, cols):
            up_ref[:, c] = _dot_t(h_ref[...], w_ref[...])

    @pl.when(j >= lat_steps + up_steps)
    def _():
        for w_ref, c in zip(w_tiles, cols):
            gate_ref[:, c] = _sigmoid(_dot_t(h_ref[...], w_ref[...]) + b_ref[:, c])


def _inproj_call(x, g, w_t, g_lat, b_gate, pool_width, tm):
    m, d = x.shape
    tn = Q_LORA_RANK
    per_step = INPROJ_TILES_PER_STEP
    wide = per_step * tn
    lat_steps = g_lat.shape[1] // wide
    up_steps = pool_width // wide
    gate_steps = b_gate.shape[1] // wide
    assert (lat_steps * wide, up_steps * wide, gate_steps * wide) == (g_lat.shape[1], pool_width, b_gate.shape[1])
    n_lat = lat_steps * per_step
    o_kr = n_lat * tn
    o_up = o_kr + QK_ROPE_DIM
    clamp = lambda j, lo, n: jnp.clip(j - lo, 0, n - 1)
    sub = SUBLANES
    w_row = lambda tile: sub * jnp.where(tile < n_lat, tile * (tn // sub),
                                         o_up // sub + (tile - n_lat) * (tn // sub))
    return pl.pallas_call(
        functools.partial(_inproj_kernel, lat_steps=lat_steps, up_steps=up_steps, w_row=w_row, tn=tn, tm=tm),
        grid=(m // tm, lat_steps + up_steps + gate_steps),
        in_specs=[
            pl.BlockSpec(memory_space=pl.ANY),
            pl.BlockSpec((1, d), lambda i, j: (0, 0)),
            pl.BlockSpec(memory_space=pl.ANY),
            pl.BlockSpec((1, wide), lambda i, j: (0, clamp(j, 0, lat_steps))),
            pl.BlockSpec((1, wide), lambda i, j: (0, clamp(j, lat_steps + up_steps, gate_steps))),
            pl.BlockSpec((pl.Element(QK_ROPE_DIM), pl.Element(d)), lambda i, j: (o_kr, 0)),
        ],
        out_specs=[
            pl.BlockSpec((tm, wide), lambda i, j: (i, clamp(j, 0, lat_steps))),
            pl.BlockSpec((tm, wide), lambda i, j: (i, clamp(j, lat_steps, up_steps))),
            pl.BlockSpec((tm, wide), lambda i, j: (i, clamp(j, lat_steps + up_steps, gate_steps))),
            pl.BlockSpec((tm, LANES), lambda i, j: (i, 0)),
        ],
        out_shape=[
            jax.ShapeDtypeStruct((m, lat_steps * wide), BF16),
            jax.ShapeDtypeStruct((m, up_steps * wide), F32),
            jax.ShapeDtypeStruct((m, gate_steps * wide), F32),
            jax.ShapeDtypeStruct((m, LANES), F32),
        ],
        scratch_shapes=[pltpu.VMEM((tm, d), BF16), pltpu.VMEM((2 * per_step, tn, d), F32),
                        pltpu.SemaphoreType.DMA((2 * per_step,)),
                        pltpu.VMEM((tm, d), F32), pltpu.SemaphoreType.DMA((1,))],
        compiler_params=_params("arbitrary", "arbitrary"),
        name="inproj",
    )(x, g, w_t, g_lat, b_gate, w_t)


def _pool_mixer(u, halo, t0, wg_ref, sc_ref):
    tm = u.shape[0]
    halo = jnp.where(t0 > 0, halo, 0.0)
    ext = jnp.concatenate([halo, u], axis=0)
    pos = lax.broadcasted_iota(jnp.int32, (tm, 1), 0) + t0
    gd = wg_ref.shape[1]
    out = []
    for g, w in enumerate(POOL_WINDOWS):
        cols = slice(g * gd, (g + 1) * gd)
        a = ext[:, cols]
        shift = 1
        while shift < w:
            a = a + pltpu.roll(a, shift, 0)
            shift *= 2
        count = jnp.minimum(pos + 1, w).astype(F32)
        pooled = a[POOL_HALO:, :] / count - u[:, cols]
        y = _dot(pooled.astype(BF16), wg_ref[g]) * sc_ref[:, cols]
        out.append(y.astype(BF16))
    return jnp.concatenate(out, axis=1)


def _qkv_kernel(cq_ref, ckv_ref, kr_ref, cos_ref, sin_ref, wq_ref, wkv_ref,
                gq_ref, gk_ref, qt_ref, k_ref, vt_ref, wqt_s, wk_s, wvt_s):
    @pl.when(pl.program_id(0) == 0)
    def _():
        for c in range(wq_ref.shape[1] // LANES):
            wqt_s[c * LANES:(c + 1) * LANES, :] = wq_ref[:, c * LANES:(c + 1) * LANES].T.astype(BF16)
        for h in range(N_HEADS):
            lo = h * (QK_NOPE_DIM + V_HEAD_DIM)
            wk_s[:, h * LANES:(h + 1) * LANES] = wkv_ref[:, lo:lo + QK_NOPE_DIM].astype(BF16)
            wvt_s[h * LANES:(h + 1) * LANES, :] = (
                wkv_ref[:, lo + QK_NOPE_DIM:lo + QK_NOPE_DIM + V_HEAD_DIM].T.astype(BF16))

    inv_dim = 1.0 / QK_HEAD_DIM
    half = ROPE_HALF
    cq = cq_ref[...]
    ckv = ckv_ref[...]
    cos_t = cos_ref[...]
    sin_t = sin_ref[...]
    qt = lax.dot_general(wqt_s[...], cq, _CONTRACT_LAST,
                         preferred_element_type=F32)
    vt_ref[...] = lax.dot_general(wvt_s[...], ckv, _CONTRACT_LAST,
                                  preferred_element_type=F32).astype(BF16)
    gq = gq_ref[...]
    zeros = jnp.zeros((half, qt.shape[1]), BF16)
    for h in range(N_HEADS):
        src = h * QK_HEAD_DIM
        dst = h * HEAD_SLOT
        qh = qt[src:src + QK_HEAD_DIM]
        inv = lax.rsqrt(jnp.sum(qh * qh, axis=0, keepdims=True) * inv_dim + EPS)
        qh = qh * inv * gq
        t1 = qh[QK_NOPE_DIM:QK_NOPE_DIM + half]
        t2 = qh[QK_NOPE_DIM + half:]
        qt_ref[dst:dst + QK_NOPE_DIM] = qh[:QK_NOPE_DIM].astype(BF16)
        qt_ref[dst + LANES:dst + LANES + half] = (t1 * cos_t - t2 * sin_t).astype(BF16)
        qt_ref[dst + LANES + half:dst + LANES + 2 * half] = zeros
        qt_ref[dst + LANES + 2 * half:dst + LANES + 3 * half] = (t2 * cos_t + t1 * sin_t).astype(BF16)
        qt_ref[dst + LANES + 3 * half:dst + HEAD_SLOT] = zeros

    cos_r = cos_t.T
    sin_r = sin_t.T
    cos = jnp.concatenate([cos_r] * 4, axis=1)
    sin = jnp.concatenate([-sin_r, -sin_r, sin_r, sin_r], axis=1)
    kv = _dot(ckv, wk_s[...])
    gk_n, gk_r = gk_ref[:, :LANES], gk_ref[:, LANES:]
    kr = kr_ref[...]
    ss_kr = jnp.sum(kr * kr, axis=-1, keepdims=True)
    kr_rot = _rope(kr * gk_r, cos, sin)
    for h in range(N_HEADS):
        lo = h * HEAD_SLOT
        kn = kv[:, h * LANES:(h + 1) * LANES]
        ssk = jnp.sum(kn * kn, axis=-1, keepdims=True) + ss_kr
        invk = lax.rsqrt(ssk * inv_dim + EPS)
        k_ref[:, lo:lo + LANES] = (kn * invk * gk_n).astype(BF16)
        k_ref[:, lo + LANES:lo + HEAD_SLOT] = (kr_rot * invk).astype(BF16)


def _qkv_call(lat, kr, cos_t, sin_t, wq, wkv, gq, gk, tm):
    m = lat.shape[0]
    rank = Q_LORA_RANK
    wide = N_HEADS * HEAD_SLOT
    vw = N_HEADS * V_HEAD_DIM
    row = lambda i: (i, 0)
    col = lambda i: (0, i)
    fixed = lambda i: (0, 0)
    return pl.pallas_call(
        _qkv_kernel,
        grid=(m // tm,),
        in_specs=[
            pl.BlockSpec((tm, rank), lambda i: (i, 0)),
            pl.BlockSpec((tm, rank), lambda i: (i, 1)),
            pl.BlockSpec((tm, LANES), row),
            pl.BlockSpec((ROPE_HALF, tm), col),
            pl.BlockSpec((ROPE_HALF, tm), col),
            pl.BlockSpec(wq.shape, fixed, pipeline_mode=pl.Buffered(1)),
            pl.BlockSpec(wkv.shape, fixed, pipeline_mode=pl.Buffered(1)),
            pl.BlockSpec(gq.shape, fixed),
            pl.BlockSpec((1, HEAD_SLOT), fixed),
        ],
        out_specs=[
            pl.BlockSpec((None, wide, tm), lambda i: (i, 0, 0)),
            pl.BlockSpec((tm, wide), row),
            pl.BlockSpec((None, vw, tm), lambda i: (i, 0, 0)),
        ],
        out_shape=[
            jax.ShapeDtypeStruct((m // tm, wide, tm), BF16),
            jax.ShapeDtypeStruct((m, wide), BF16),
            jax.ShapeDtypeStruct((m // tm, vw, tm), BF16),
        ],
        scratch_shapes=[
            pltpu.VMEM((wq.shape[1], rank), BF16),
            pltpu.VMEM((rank, N_HEADS * QK_NOPE_DIM), BF16),
            pltpu.VMEM((vw, rank), BF16),
        ],
        compiler_params=pltpu.CompilerParams(dimension_semantics=("arbitrary",),
                                             vmem_limit_bytes=BIG_VMEM_LIMIT),
        name="qkv",
    )(lat, lat, kr, cos_t, sin_t, wq, wkv, gq, gk)


def _flash_kernel(q_ref, k_ref, vt_ref, *refs, tq, heads, n_cast):
    cast_src, (o_ref, *cast_dst), (m_ref, acc_ref, s_ref, mx_ref) = (
        refs[:n_cast], refs[n_cast:2 * n_cast + 1], refs[2 * n_cast + 1:])
    for src, dst in zip(cast_src, cast_dst):
        dst[...] = src[...].astype(BF16)
    i = pl.program_id(2)
    ones = jnp.ones((SUM_ROWS, tq), BF16)
    m_ref[...] = jnp.full(m_ref.shape, -jnp.inf, F32)
    acc_ref[...] = jnp.zeros(acc_ref.shape, F32)

    half = tq // 2
    head_cols = lambda h: slice(h * HEAD_SLOT, (h + 1) * HEAD_SLOT)
    chains = [(t, h) for t in range(2) for h in range(heads)]
    cid = lambda t, h: t * heads + h

    def v_rows(h, c):
        vt = vt_ref[c, h * V_HEAD_DIM:(h + 1) * V_HEAD_DIM, :]
        return jnp.concatenate([vt, ones], axis=0)

    def new_max(ch, buf):
        m_prev = m_ref[ch]
        m_new = jnp.maximum(m_prev, mx_ref[buf, ch])
        m_ref[ch] = m_new
        return m_new, jnp.exp2(m_prev - m_new)

    def scores(t, h, c, buf):
        start = pl.multiple_of(c * tq, tq)
        s = _dot(k_ref[pl.ds(start, tq), head_cols(h)], q_ref[t, head_cols(h), :])
        s_ref[buf, cid(t, h)] = s
        mx_ref[buf, cid(t, h)] = jnp.max(s, axis=0, keepdims=True)

    def accumulate(t, h, c, buf):
        ch = cid(t, h)
        m_new, alpha = new_max(ch, buf)
        p = jnp.exp2(s_ref[buf, ch] - m_new)
        acc_ref[ch] = alpha * acc_ref[ch] + _dot(v_rows(h, c), p.astype(BF16))

    def scores_diagonal(t, h, c, buf):
        ch = cid(t, h)
        start = pl.multiple_of(c * tq, tq)
        causal = (lax.broadcasted_iota(jnp.int32, (half, tq), 0)
                  <= lax.broadcasted_iota(jnp.int32, (half, tq), 1))
        q_t = q_ref[t, head_cols(h), :]
        s0 = _dot(k_ref[pl.ds(start, half), head_cols(h)], q_t)
        s1 = _dot(k_ref[pl.ds(start + half, half), head_cols(h)], q_t[:, half:])
        s0 = jnp.where(causal, s0, -jnp.inf)
        s1 = jnp.where(causal[:, :half], s1, -jnp.inf)
        s_ref[buf, ch, :half, :] = s0
        s_ref[buf, ch, half:, half:] = s1
        m0 = jnp.max(s0, axis=0, keepdims=True)
        m1 = jnp.max(s1, axis=0, keepdims=True)
        mx_ref[buf, ch] = jnp.concatenate([m0[:, :half], jnp.maximum(m0[:, half:], m1)], axis=1)

    def accumulate_diagonal(t, h, c, buf):
        ch = cid(t, h)
        m_new, alpha = new_max(ch, buf)
        p0 = jnp.exp2(s_ref[buf, ch, :half, :] - m_new).astype(BF16)
        p1 = jnp.exp2(s_ref[buf, ch, half:, half:] - m_new[:, half:]).astype(BF16)
        vt = v_rows(h, c)
        acc = alpha * acc_ref[ch] + _dot(vt[:, :half], p0)
        acc_ref[ch, :, :half] = acc[:, :half]
        acc_ref[ch, :, half:] = acc[:, half:] + _dot(vt[:, half:], p1)

    def stage(c, buf, nxt, now):
        for t, h in chains:
            if nxt[t] == FULL:
                scores(t, h, c + 1, 1 - buf)
            elif nxt[t] == DIAG:
                scores_diagonal(t, h, c + 1, 1 - buf)
            if now[t] == FULL:
                accumulate(t, h, c, buf)
            elif now[t] == DIAG:
                accumulate_diagonal(t, h, c, buf)

    FULL, DIAG = "full", "diagonal"
    both_full, a_diag, b_diag, nothing = (FULL, FULL), (DIAG, FULL), (None, DIAG), (None, None)
    a_tile = 2 * i

    @pl.when(i == 0)
    def _():
        for t, h in chains:
            (scores_diagonal if t == 0 else scores)(t, h, 0, 0)
        stage(0, 0, b_diag, a_diag)
        stage(1, 1, nothing, b_diag)

    @pl.when(i > 0)
    def _():
        for t, h in chains:
            scores(t, h, 0, 0)

    def two_stages(j, carry):
        stage(2 * j, 0, both_full, both_full)
        stage(2 * j + 1, 1, both_full, both_full)
        return carry

    lax.fori_loop(0, i - 1, two_stages, None)

    @pl.when(i > 0)
    def _():
        stage(a_tile - 2, 0, both_full, both_full)
        stage(a_tile - 1, 1, a_diag, both_full)
        stage(a_tile, 0, b_diag, a_diag)
        stage(a_tile + 1, 1, nothing, b_diag)

    for t, h in chains:
        ch = cid(t, h)
        out = acc_ref[ch, :V_HEAD_DIM, :] / acc_ref[ch, V_HEAD_DIM:V_HEAD_DIM + 1, :]
        o_ref[t * tq:(t + 1) * tq, h * V_HEAD_DIM:(h + 1) * V_HEAD_DIM] = out.T.astype(BF16)


def _flash_call(q, k, vt, cast_weights, tq, heads):
    b, s, _ = k.shape
    grid = (b, N_HEADS // heads, s // (2 * tq))
    n_steps = grid[0] * grid[1] * grid[2]
    step = lambda bi, h, i: ((bi * grid[1] + h) * grid[2] + i, 0)
    cast_specs = [pl.BlockSpec((w.shape[0] // n_steps, w.shape[1]), step) for w in cast_weights]
    return pl.pallas_call(
        functools.partial(_flash_kernel, tq=tq, heads=heads, n_cast=len(cast_weights)),
        grid=grid,
        in_specs=[
            pl.BlockSpec((None, 2, heads * HEAD_SLOT, tq), lambda bi, h, i: (bi, i, h, 0)),
            pl.BlockSpec((None, s, heads * HEAD_SLOT), lambda bi, h, i: (bi, 0, h)),
            pl.BlockSpec((None, s // tq, heads * V_HEAD_DIM, tq), lambda bi, h, i: (bi, 0, h, 0)),
        ] + cast_specs,
        out_specs=[pl.BlockSpec((None, 2 * tq, heads * V_HEAD_DIM), lambda bi, h, i: (bi, i, h))] + cast_specs,
        out_shape=[jax.ShapeDtypeStruct((b, s, N_HEADS * V_HEAD_DIM), BF16)]
        + [jax.ShapeDtypeStruct(w.shape, BF16) for w in cast_weights],
        scratch_shapes=[
            pltpu.VMEM((2 * heads, 1, tq), F32),
            pltpu.VMEM((2 * heads, V_HEAD_DIM + SUM_ROWS, tq), F32),
            pltpu.VMEM((2, 2 * heads, tq, tq), F32),
            pltpu.VMEM((2, 2 * heads, 1, tq), F32),
        ],
        compiler_params=pltpu.CompilerParams(dimension_semantics=("parallel", "parallel", "arbitrary"),
                                             vmem_limit_bytes=BIG_VMEM_LIMIT),
        name="flash",
    )(q, k, vt, *cast_weights)


def _mix_kernel(x_ref, u_ref, halo_ref, attn_ref, gp_ref, ga_ref, wg_ref, sc_ref,
                wpo_ref, wao_ref, wout_ref, g2_ref, o_ref, h2_ref, *, tm, seq):
    t0 = (pl.program_id(0) * tm) % seq
    pooled = _pool_mixer(u_ref[...], halo_ref[...], t0, wg_ref, sc_ref)
    y_pool = _dot(pooled, wpo_ref[...])
    y_attn = _dot(attn_ref[...], wao_ref[...])
    mixed = gp_ref[...] * y_pool + ga_ref[...] * y_attn
    x1 = x_ref[...] + _dot(mixed.astype(BF16), wout_ref[...])
    o_ref[...] = x1
    h2_ref[...] = _rms_bf16(x1, g2_ref[...])


def _mix_call(x, u_pool, attn, gates, w_grp, scale, w_po, w_ao, w_out, g_ffn, tm, seq):
    m, d = x.shape
    width = u_pool.shape[1]
    halo_blocks = tm // POOL_HALO
    row = lambda i: (i, 0)
    fixed = lambda i: (0, 0)
    return pl.pallas_call(
        functools.partial(_mix_kernel, tm=tm, seq=seq),
        grid=(m // tm,),
        in_specs=[
            pl.BlockSpec((tm, d), row),
            pl.BlockSpec((tm, width), row),
            pl.BlockSpec((POOL_HALO, width), lambda i: (jnp.maximum(i * halo_blocks - 1, 0), 0)),
            pl.BlockSpec((tm, attn.shape[1]), row),
            pl.BlockSpec((tm, d), lambda i: (i, 0)),
            pl.BlockSpec((tm, d), lambda i: (i, 1)),
            pl.BlockSpec(w_grp.shape, lambda i: (0, 0, 0)),
            pl.BlockSpec((1, width), fixed),
            pl.BlockSpec(w_po.shape, fixed),
            pl.BlockSpec(w_ao.shape, fixed),
            pl.BlockSpec(w_out.shape, fixed),
            pl.BlockSpec((1, d), fixed),
        ],
        out_specs=[pl.BlockSpec((tm, d), row), pl.BlockSpec((tm, d), row)],
        out_shape=[jax.ShapeDtypeStruct((m, d), F32), jax.ShapeDtypeStruct((m, d), BF16)],
        compiler_params=_params("parallel"),
        name="mix",
    )(x, u_pool, u_pool, attn, gates, gates, w_grp, scale, w_po, w_ao, w_out, g_ffn)


def _ffn_kernel(h_ref, x_ref, wg_ref, wu_ref, wd_ref, o_ref, *, res_steps):
    j = pl.program_id(1)

    def step(first):
        h = h_ref[...]
        a = _dot(h, wg_ref[...].astype(BF16))
        u = _dot(h, wu_ref[...].astype(BF16))
        act = _silu(a) * u
        down = _dot(act.astype(BF16), wd_ref[...].astype(BF16))
        if first:
            o_ref[...] = down
        else:
            o_ref[...] += down

    pl.when(j == 0)(lambda: step(True))
    pl.when(j > 0)(lambda: step(False))

    slab = x_ref.shape[1]
    for c in range(res_steps):
        @pl.when(j == c)
        def _():
            o_ref[:, c * slab:(c + 1) * slab] += x_ref[...]


def _ffn_call(h, x, w_gate, w_up, w_down, tm, tf):
    m, d = x.shape
    f = w_gate.shape[1]
    res_steps = FFN_RESIDUAL_STEPS
    assert f // tf >= res_steps
    return pl.pallas_call(
        functools.partial(_ffn_kernel, res_steps=res_steps),
        grid=(m // tm, f // tf),
        in_specs=[
            pl.BlockSpec((tm, d), lambda i, j: (i, 0)),
            pl.BlockSpec((tm, d // res_steps), lambda i, j: (i, jnp.minimum(j, res_steps - 1))),
            pl.BlockSpec((d, tf), lambda i, j: (0, j)),
            pl.BlockSpec((d, tf), lambda i, j: (0, j)),
            pl.BlockSpec((tf, d), lambda i, j: (j, 0)),
        ],
        out_specs=pl.BlockSpec((tm, d), lambda i, j: (i, 0)),
        out_shape=jax.ShapeDtypeStruct((m, d), F32),
        compiler_params=pltpu.CompilerParams(dimension_semantics=("parallel", "arbitrary"),
                                             vmem_limit_bytes=BIG_VMEM_LIMIT),
        name="ffn",
    )(h, x, w_gate, w_up, w_down)


def _rope_slot(t):
    z = jnp.zeros(t.shape[:-1] + (ROPE_HALF,), t.dtype)
    return jnp.concatenate([t[..., :ROPE_HALF], z, t[..., ROPE_HALF:], z], axis=-1)


def _head_slot(t):
    return jnp.concatenate([t[..., :QK_NOPE_DIM], _rope_slot(t[..., QK_NOPE_DIM:])], axis=-1)


def kernel(x, positions, attn_norm_g, w_in, b_gate, q_a_norm_g, w_q_b, kv_a_norm_g, w_kv_b,
           q_norm_g, k_norm_g, w_attn_o, w_pool_grp, pool_scale, w_pool_o, w_out,
           ffn_norm_g, w_ffn_gate, w_ffn_up, w_ffn_down):
    b, s, d = x.shape
    depth = w_in.shape[0]
    m = b * s
    pool_width = w_pool_o.shape[1]

    inv_freq = ROPE_THETA ** (-jnp.arange(ROPE_HALF, dtype=F32) / ROPE_HALF)
    ang_t = positions.astype(F32).reshape(1, m) * inv_freq.reshape(ROPE_HALF, 1)
    cos_t = jnp.cos(ang_t)
    sin_t = jnp.sin(ang_t)

    xf = x.reshape(m, d)
    for l in range(depth):
        g_lat = jnp.concatenate([q_a_norm_g[l], kv_a_norm_g[l]]).reshape(1, -1)
        tq = ATTN_TILE
        gq = jnp.broadcast_to((q_norm_g[l] * (QK_HEAD_DIM ** -0.5 * LOG2_E)).reshape(-1, 1),
                              (QK_HEAD_DIM, tq))
        gk = _head_slot(k_norm_g[l]).reshape(1, HEAD_SLOT)
        g_attn_norm = attn_norm_g[l].reshape(1, d)

        lat, u_pool, gates, kr = _inproj_call(xf, g_attn_norm, w_in[l].T, g_lat,
                                              b_gate[l].reshape(1, -1), pool_width, tm=INPROJ_ROWS)
        qt, k, vt = _qkv_call(lat, kr, cos_t, sin_t, w_q_b[l], w_kv_b[l], gq, gk, tm=tq)
        attn, w_po, w_ao, w_o = _flash_call(qt.reshape(b, s // tq, -1, tq), k.reshape(b, s, -1),
                                            vt.reshape(b, s // tq, -1, tq),
                                            [w_pool_o[l], w_attn_o[l], w_out[l]],
                                            tq=tq, heads=ATTN_HEADS_PER_STEP)
        xf, h2 = _mix_call(xf, u_pool, attn.reshape(m, -1), gates, w_pool_grp[l].astype(BF16),
                           pool_scale[l].reshape(1, -1), w_po, w_ao, w_o,
                           ffn_norm_g[l].reshape(1, d), tm=MIX_ROWS, seq=s)
        xf = _ffn_call(h2, xf, w_ffn_gate[l], w_ffn_up[l], w_ffn_down[l], tm=FFN_ROWS, tf=FFN_COLS)
    return xf.reshape(b, s, d)
```

```python
import functools

import jax
import jax.numpy as jnp
from jax import lax
from jax.experimental import pallas as pl
from jax.experimental.pallas import tpu as pltpu

F32 = jnp.float32
BF16 = jnp.bfloat16

N_HEADS = 16
QK_NOPE_DIM = 128
QK_ROPE_DIM = 64
QK_HEAD_DIM = QK_NOPE_DIM + QK_ROPE_DIM
V_HEAD_DIM = 128
Q_LORA_RANK = 512
KV_LORA_RANK = 512
ROPE_THETA = 10000.0
POOL_WINDOWS = (2, 4, 8, 16)
EPS = 1e-6

LANES = 128
SUBLANES = 8
HEAD_SLOT = 2 * LANES
ROPE_HALF = QK_ROPE_DIM // 2
POOL_HALO = 16
VMEM_LIMIT = 56 * 1024 * 1024
BIG_VMEM_LIMIT = 62 * 1024 * 1024
FFN_RESIDUAL_STEPS = 8

INPROJ_ROWS = 1024
ATTN_TILE = 512
ATTN_HEADS_PER_STEP = 4
MIX_ROWS = 256
FFN_ROWS, FFN_COLS = 1024, 512
INPROJ_TILES_PER_STEP = 2
SUM_ROWS = 16
LOG2_E = 1.4426950408889634
_CONTRACT_LAST = (((1,), (1,)), ((), ()))


def _params(*sem):
    return pltpu.CompilerParams(dimension_semantics=sem, vmem_limit_bytes=VMEM_LIMIT)


def _dot(a, b):
    return jnp.dot(a, b, preferred_element_type=F32)


def _sigmoid(x):
    return 0.5 * jnp.tanh(0.5 * x) + 0.5


def _silu(x):
    t = 0.5 * x
    return t + t * jnp.tanh(t)


def _rms_bf16(x, g):
    ms = jnp.mean(x * x, axis=-1, keepdims=True)
    return (x * lax.rsqrt(ms + EPS) * g).astype(BF16)


def _rope(t, cos, sin):
    return t * cos + pltpu.roll(t, 2 * ROPE_HALF, 1) * sin


def _dot_t(a, w):
    return lax.dot_general(a, w.astype(BF16), _CONTRACT_LAST, preferred_element_type=F32)


def _inproj_kernel(x_hbm, g_ref, w_hbm, gl_ref, b_ref, wkr_ref,
                   lat_ref, up_ref, gate_ref, kr_ref, h_ref, w_ring, w_sem, x_buf, x_sem,
                   *, lat_steps, up_steps, w_row, tn, tm):
    per_step = INPROJ_TILES_PER_STEP
    i = pl.program_id(0)
    j = pl.program_id(1)
    n_i = pl.num_programs(0)
    n_j = pl.num_programs(1)
    t = i * n_j + j
    last = n_i * n_j - 1

    def w_copy(tile):
        slot = lax.rem(tile, 2 * per_step)
        rows = pl.ds(w_row(lax.rem(tile, n_j * per_step)), tn)
        return pltpu.make_async_copy(w_hbm.at[rows, :], w_ring.at[slot], w_sem.at[slot])

    def x_copy(tile):
        rows = pl.ds(pl.multiple_of(tile * tm, tm), tm)
        return pltpu.make_async_copy(x_hbm.at[rows, :], x_buf, x_sem.at[0])

    @pl.when(t == 0)
    def _():
        x_copy(0).start()
        for k in range(per_step):
            w_copy(k).start()

    @pl.when(t < last)
    def _():
        for k in range(per_step):
            w_copy((t + 1) * per_step + k).start()

    @pl.when(jnp.logical_and(j == 1, i + 1 < n_i))
    def _():
        x_copy(i + 1).start()

    w_tiles = []
    for k in range(per_step):
        w_copy(t * per_step + k).wait()
        w_tiles.append(w_ring.at[lax.rem(t * per_step + k, 2 * per_step)])
    cols = [slice(k * tn, (k + 1) * tn) for k in range(per_step)]

    @pl.when(j == 0)
    def _():
        x_copy(i).wait()
        h_ref[...] = _rms_bf16(x_buf[...], g_ref[...])
        wkr = wkr_ref[...]
        z = jnp.zeros((ROPE_HALF, wkr.shape[1]), F32)
        slot = jnp.concatenate([wkr[:ROPE_HALF], z, wkr[ROPE_HALF:], z], axis=0)
        kr_ref[...] = _dot_t(h_ref[...], slot)

    @pl.when(j < lat_steps)
    def _():
        for w_ref, c in zip(w_tiles, cols):
            lat_ref[:, c] = _rms_bf16(_dot_t(h_ref[...], w_ref[...]), gl_ref[:, c])

    @pl.when(jnp.logical_and(j >= lat_steps, j < lat_steps + up_steps))
    def _():
        for w_ref, c in zip(w_tiles, cols):
            up_ref[:, c] = _dot_t(h_ref[...], w_ref[...])

    @pl.when(j >= lat_steps + up_steps)
    def _():
        for w_ref, c in zip(w_tiles, cols):
            gate_ref[:, c] = _sigmoid(_dot_t(h_ref[...], w_ref[...]) + b_ref[:, c])


def _inproj_call(x, g, w_t, g_lat, b_gate, pool_width, tm):
    m, d = x.shape
    tn = Q_LORA_RANK
    per_step = INPROJ_TILES_PER_STEP
    wide = per_step * tn
    lat_steps = g_lat.shape[1] // wide
    up_steps = pool_width // wide
    gate_steps = b_gate.shape[1] // wide
    assert (lat_steps * wide, up_steps * wide, gate_steps * wide) == (g_lat.shape[1], pool_width, b_gate.shape[1])
    n_lat = lat_steps * per_step
    o_kr = n_lat * tn
    o_up = o_kr + QK_ROPE_DIM
    clamp = lambda j, lo, n: jnp.clip(j - lo, 0, n - 1)
    sub = SUBLANES
    w_row = lambda tile: sub * jnp.where(tile < n_lat, tile * (tn // sub),
                                         o_up // sub + (tile - n_lat) * (tn // sub))
    return pl.pallas_call(
        functools.partial(_inproj_kernel, lat_steps=lat_steps, up_steps=up_steps, w_row=w_row, tn=tn, tm=tm),
        grid=(m // tm, lat_steps + up_steps + gate_steps),
        in_specs=[
            pl.BlockSpec(memory_space=pl.ANY),
            pl.BlockSpec((1, d), lambda i, j: (0, 0)),
            pl.BlockSpec(memory_space=pl.ANY),
            pl.BlockSpec((1, wide), lambda i, j: (0, clamp(j, 0, lat_steps))),
            pl.BlockSpec((1, wide), lambda i, j: (0, clamp(j, lat_steps + up_steps, gate_steps))),
            pl.BlockSpec((pl.Element(QK_ROPE_DIM), pl.Element(d)), lambda i, j: (o_kr, 0)),
        ],
        out_specs=[
            pl.BlockSpec((tm, wide), lambda i, j: (i, clamp(j, 0, lat_steps))),
            pl.BlockSpec((tm, wide), lambda i, j: (i, clamp(j, lat_steps, up_steps))),
            pl.BlockSpec((tm, wide), lambda i, j: (i, clamp(j, lat_steps + up_steps, gate_steps))),
            pl.BlockSpec((tm, LANES), lambda i, j: (i, 0)),
        ],
        out_shape=[
            jax.ShapeDtypeStruct((m, lat_steps * wide), BF16),
            jax.ShapeDtypeStruct((m, up_steps * wide), F32),
            jax.ShapeDtypeStruct((m, gate_steps * wide), F32),
            jax.ShapeDtypeStruct((m, LANES), F32),
        ],
        scratch_shapes=[pltpu.VMEM((tm, d), BF16), pltpu.VMEM((2 * per_step, tn, d), F32),
                        pltpu.SemaphoreType.DMA((2 * per_step,)),
                        pltpu.VMEM((tm, d), F32), pltpu.SemaphoreType.DMA((1,))],
        compiler_params=_params("arbitrary", "arbitrary"),
        name="inproj",
    )(x, g, w_t, g_lat, b_gate, w_t)


def _pool_mixer(u, halo, t0, wg_ref, sc_ref):
    tm = u.shape[0]
    halo = jnp.where(t0 > 0, halo, 0.0)
    ext = jnp.concatenate([halo, u], axis=0)
    pos = lax.broadcasted_iota(jnp.int32, (tm, 1), 0) + t0
    gd = wg_ref.shape[1]
    out = []
    for g, w in enumerate(POOL_WINDOWS):
        cols = slice(g * gd, (g + 1) * gd)
        a = ext[:, cols]
        shift = 1
        while shift < w:
            a = a + pltpu.roll(a, shift, 0)
            shift *= 2
        count = jnp.minimum(pos + 1, w).astype(F32)
        pooled = a[POOL_HALO:, :] / count - u[:, cols]
        y = _dot(pooled.astype(BF16), wg_ref[g]) * sc_ref[:, cols]
        out.append(y.astype(BF16))
    return jnp.concatenate(out, axis=1)


def _qkv_kernel(cq_ref, ckv_ref, kr_ref, cos_ref, sin_ref, wq_ref, wkv_ref,
                gq_ref, gk_ref, qt_ref, k_ref, vt_ref, wqt_s, wk_s, wvt_s, gq_s, gk_s):
    @pl.when(pl.program_id(0) == 0)
    def _():
        for c in range(wq_ref.shape[1] // LANES):
            wqt_s[c * LANES:(c + 1) * LANES, :] = wq_ref[:, c * LANES:(c + 1) * LANES].T.astype(BF16)
        for h in range(N_HEADS):
            lo = h * (QK_NOPE_DIM + V_HEAD_DIM)
            wk_s[:, h * LANES:(h + 1) * LANES] = wkv_ref[:, lo:lo + QK_NOPE_DIM].astype(BF16)
            wvt_s[h * LANES:(h + 1) * LANES, :] = (
                wkv_ref[:, lo + QK_NOPE_DIM:lo + QK_NOPE_DIM + V_HEAD_DIM].T.astype(BF16))
        gq_s[...] = jnp.broadcast_to(gq_ref[...] * (QK_HEAD_DIM ** -0.5 * LOG2_E), gq_s.shape)
        g = gk_ref[...]
        z = jnp.zeros((1, ROPE_HALF), F32)
        gk_s[...] = jnp.concatenate([g[:, :QK_NOPE_DIM], g[:, QK_NOPE_DIM:QK_NOPE_DIM + ROPE_HALF], z,
                                     g[:, QK_NOPE_DIM + ROPE_HALF:], z], axis=1)

    inv_dim = 1.0 / QK_HEAD_DIM
    half = ROPE_HALF
    cq = cq_ref[...]
    ckv = ckv_ref[...]
    cos_t = cos_ref[...]
    sin_t = sin_ref[...]
    qt = lax.dot_general(wqt_s[...], cq, _CONTRACT_LAST,
                         preferred_element_type=F32)
    vt_ref[...] = lax.dot_general(wvt_s[...], ckv, _CONTRACT_LAST,
                                  preferred_element_type=F32).astype(BF16)
    gq = gq_s[...]
    zeros = jnp.zeros((half, qt.shape[1]), BF16)
    for h in range(N_HEADS):
        src = h * QK_HEAD_DIM
        dst = h * HEAD_SLOT
        qh = qt[src:src + QK_HEAD_DIM]
        inv = lax.rsqrt(jnp.sum(qh * qh, axis=0, keepdims=True) * inv_dim + EPS)
        qh = qh * inv * gq
        t1 = qh[QK_NOPE_DIM:QK_NOPE_DIM + half]
        t2 = qh[QK_NOPE_DIM + half:]
        qt_ref[dst:dst + QK_NOPE_DIM] = qh[:QK_NOPE_DIM].astype(BF16)
        qt_ref[dst + LANES:dst + LANES + half] = (t1 * cos_t - t2 * sin_t).astype(BF16)
        qt_ref[dst + LANES + half:dst + LANES + 2 * half] = zeros
        qt_ref[dst + LANES + 2 * half:dst + LANES + 3 * half] = (t2 * cos_t + t1 * sin_t).astype(BF16)
        qt_ref[dst + LANES + 3 * half:dst + HEAD_SLOT] = zeros

    cos_r = cos_t.T
    sin_r = sin_t.T
    cos = jnp.concatenate([cos_r] * 4, axis=1)
    sin = jnp.concatenate([-sin_r, -sin_r, sin_r, sin_r], axis=1)
    kv = _dot(ckv, wk_s[...])
    gk_n, gk_r = gk_s[:, :LANES], gk_s[:, LANES:]
    kr = kr_ref[...]
    ss_kr = jnp.sum(kr * kr, axis=-1, keepdims=True)
    kr_rot = _rope(kr * gk_r, cos, sin)
    for h in range(N_HEADS):
        lo = h * HEAD_SLOT
        kn = kv[:, h * LANES:(h + 1) * LANES]
        ssk = jnp.sum(kn * kn, axis=-1, keepdims=True) + ss_kr
        invk = lax.rsqrt(ssk * inv_dim + EPS)
        k_ref[:, lo:lo + LANES] = (kn * invk * gk_n).astype(BF16)
        k_ref[:, lo + LANES:lo + HEAD_SLOT] = (kr_rot * invk).astype(BF16)


def _qkv_call(lat, kr, cos_t, sin_t, wq, wkv, gq, gk, tm):
    m = lat.shape[0]
    rank = Q_LORA_RANK
    wide = N_HEADS * HEAD_SLOT
    vw = N_HEADS * V_HEAD_DIM
    row = lambda i: (i, 0)
    col = lambda i: (0, i)
    fixed = lambda i: (0, 0)
    return pl.pallas_call(
        _qkv_kernel,
        grid=(m // tm,),
        in_specs=[
            pl.BlockSpec((tm, rank), lambda i: (i, 0)),
            pl.BlockSpec((tm, rank), lambda i: (i, 1)),
            pl.BlockSpec((tm, LANES), row),
            pl.BlockSpec((ROPE_HALF, tm), col),
            pl.BlockSpec((ROPE_HALF, tm), col),
            pl.BlockSpec(wq.shape, fixed, pipeline_mode=pl.Buffered(1)),
            pl.BlockSpec(wkv.shape, fixed, pipeline_mode=pl.Buffered(1)),
            pl.BlockSpec(gq.shape, fixed),
            pl.BlockSpec(gk.shape, fixed),
        ],
        out_specs=[
            pl.BlockSpec((None, wide, tm), lambda i: (i, 0, 0)),
            pl.BlockSpec((tm, wide), row),
            pl.BlockSpec((None, vw, tm), lambda i: (i, 0, 0)),
        ],
        out_shape=[
            jax.ShapeDtypeStruct((m // tm, wide, tm), BF16),
            jax.ShapeDtypeStruct((m, wide), BF16),
            jax.ShapeDtypeStruct((m // tm, vw, tm), BF16),
        ],
        scratch_shapes=[
            pltpu.VMEM((wq.shape[1], rank), BF16),
            pltpu.VMEM((rank, N_HEADS * QK_NOPE_DIM), BF16),
            pltpu.VMEM((vw, rank), BF16),
            pltpu.VMEM((QK_HEAD_DIM, tm), F32),
            pltpu.VMEM((1, HEAD_SLOT), F32),
        ],
        compiler_params=pltpu.CompilerParams(dimension_semantics=("arbitrary",),
                                             vmem_limit_bytes=BIG_VMEM_LIMIT),
        name="qkv",
    )(lat, lat, kr, cos_t, sin_t, wq, wkv, gq, gk)


def _flash_kernel(q_ref, k_ref, vt_ref, *refs, tq, heads, n_cast):
    cast_src, (o_ref, *cast_dst), (m_ref, acc_ref, s_ref, mx_ref) = (
        refs[:n_cast], refs[n_cast:2 * n_cast + 1], refs[2 * n_cast + 1:])
    for src, dst in zip(cast_src, cast_dst):
        dst[...] = src[...].astype(BF16)
    i = pl.program_id(2)
    ones = jnp.ones((SUM_ROWS, tq), BF16)
    m_ref[...] = jnp.full(m_ref.shape, -jnp.inf, F32)
    acc_ref[...] = jnp.zeros(acc_ref.shape, F32)

    half = tq // 2
    head_cols = lambda h: slice(h * HEAD_SLOT, (h + 1) * HEAD_SLOT)
    chains = [(t, h) for t in range(2) for h in range(heads)]
    cid = lambda t, h: t * heads + h

    def v_rows(h, c):
        vt = vt_ref[c, h * V_HEAD_DIM:(h + 1) * V_HEAD_DIM, :]
        return jnp.concatenate([vt, ones], axis=0)

    def new_max(ch, buf):
        m_prev = m_ref[ch]
        m_new = jnp.maximum(m_prev, mx_ref[buf, ch])
        m_ref[ch] = m_new
        return m_new, jnp.exp2(m_prev - m_new)

    def scores(t, h, c, buf):
        start = pl.multiple_of(c * tq, tq)
        s = _dot(k_ref[pl.ds(start, tq), head_cols(h)], q_ref[t, head_cols(h), :])
        s_ref[buf, cid(t, h)] = s
        mx_ref[buf, cid(t, h)] = jnp.max(s, axis=0, keepdims=True)

    def accumulate(t, h, c, buf):
        ch = cid(t, h)
        m_new, alpha = new_max(ch, buf)
        p = jnp.exp2(s_ref[buf, ch] - m_new)
        acc_ref[ch] = alpha * acc_ref[ch] + _dot(v_rows(h, c), p.astype(BF16))

    def scores_diagonal(t, h, c, buf):
        ch = cid(t, h)
        start = pl.multiple_of(c * tq, tq)
        causal = (lax.broadcasted_iota(jnp.int32, (half, tq), 0)
                  <= lax.broadcasted_iota(jnp.int32, (half, tq), 1))
        q_t = q_ref[t, head_cols(h), :]
        s0 = _dot(k_ref[pl.ds(start, half), head_cols(h)], q_t)
        s1 = _dot(k_ref[pl.ds(start + half, half), head_cols(h)], q_t[:, half:])
        s0 = jnp.where(causal, s0, -jnp.inf)
        s1 = jnp.where(causal[:, :half], s1, -jnp.inf)
        s_ref[buf, ch, :half, :] = s0
        s_ref[buf, ch, half:, half:] = s1
        m0 = jnp.max(s0, axis=0, keepdims=True)
        m1 = jnp.max(s1, axis=0, keepdims=True)
        mx_ref[buf, ch] = jnp.concatenate([m0[:, :half], jnp.maximum(m0[:, half:], m1)], axis=1)

    def accumulate_diagonal(t, h, c, buf):
        ch = cid(t, h)
        m_new, alpha = new_max(ch, buf)
        p0 = jnp.exp2(s_ref[buf, ch, :half, :] - m_new).astype(BF16)
        p1 = jnp.exp2(s_ref[buf, ch, half:, half:] - m_new[:, half:]).astype(BF16)
        vt = v_rows(h, c)
        acc = alpha * acc_ref[ch] + _dot(vt[:, :half], p0)
        acc_ref[ch, :, :half] = acc[:, :half]
        acc_ref[ch, :, half:] = acc[:, half:] + _dot(vt[:, half:], p1)

    def stage(c, buf, nxt, now):
        for t, h in chains:
            if nxt[t] == FULL:
                scores(t, h, c + 1, 1 - buf)
            elif nxt[t] == DIAG:
                scores_diagonal(t, h, c + 1, 1 - buf)
            if now[t] == FULL:
                accumulate(t, h, c, buf)
            elif now[t] == DIAG:
                accumulate_diagonal(t, h, c, buf)

    FULL, DIAG = "full", "diagonal"
    both_full, a_diag, b_diag, nothing = (FULL, FULL), (DIAG, FULL), (None, DIAG), (None, None)
    a_tile = 2 * i

    @pl.when(i == 0)
    def _():
        for t, h in chains:
            (scores_diagonal if t == 0 else scores)(t, h, 0, 0)
        stage(0, 0, b_diag, a_diag)
        stage(1, 1, nothing, b_diag)

    @pl.when(i > 0)
    def _():
        for t, h in chains:
            scores(t, h, 0, 0)

    def two_stages(j, carry):
        stage(2 * j, 0, both_full, both_full)
        stage(2 * j + 1, 1, both_full, both_full)
        return carry

    lax.fori_loop(0, i - 1, two_stages, None)

    @pl.when(i > 0)
    def _():
        stage(a_tile - 2, 0, both_full, both_full)
        stage(a_tile - 1, 1, a_diag, both_full)
        stage(a_tile, 0, b_diag, a_diag)
        stage(a_tile + 1, 1, nothing, b_diag)

    for t, h in chains:
        ch = cid(t, h)
        out = acc_ref[ch, :V_HEAD_DIM, :] / acc_ref[ch, V_HEAD_DIM:V_HEAD_DIM + 1, :]
        o_ref[t * tq:(t + 1) * tq, h * V_HEAD_DIM:(h + 1) * V_HEAD_DIM] = out.T.astype(BF16)


def _flash_call(q, k, vt, cast_weights, tq, heads):
    b, s, _ = k.shape
    grid = (b, N_HEADS // heads, s // (2 * tq))
    n_steps = grid[0] * grid[1] * grid[2]
    step = lambda bi, h, i: ((bi * grid[1] + h) * grid[2] + i, 0)
    cast_specs = [pl.BlockSpec((w.shape[0] // n_steps, w.shape[1]), step) for w in cast_weights]
    return pl.pallas_call(
        functools.partial(_flash_kernel, tq=tq, heads=heads, n_cast=len(cast_weights)),
        grid=grid,
        in_specs=[
            pl.BlockSpec((None, 2, heads * HEAD_SLOT, tq), lambda bi, h, i: (bi, i, h, 0)),
            pl.BlockSpec((None, s, heads * HEAD_SLOT), lambda bi, h, i: (bi, 0, h)),
            pl.BlockSpec((None, s // tq, heads * V_HEAD_DIM, tq), lambda bi, h, i: (bi, 0, h, 0)),
        ] + cast_specs,
        out_specs=[pl.BlockSpec((None, 2 * tq, heads * V_HEAD_DIM), lambda bi, h, i: (bi, i, h))] + cast_specs,
        out_shape=[jax.ShapeDtypeStruct((b, s, N_HEADS * V_HEAD_DIM), BF16)]
        + [jax.ShapeDtypeStruct(w.shape, BF16) for w in cast_weights],
        scratch_shapes=[
            pltpu.VMEM((2 * heads, 1, tq), F32),
            pltpu.VMEM((2 * heads, V_HEAD_DIM + SUM_ROWS, tq), F32),
            pltpu.VMEM((2, 2 * heads, tq, tq), F32),
            pltpu.VMEM((2, 2 * heads, 1, tq), F32),
        ],
        compiler_params=pltpu.CompilerParams(dimension_semantics=("parallel", "parallel", "arbitrary"),
                                             vmem_limit_bytes=BIG_VMEM_LIMIT),
        name="flash",
    )(q, k, vt, *cast_weights)


def _mix_kernel(x_ref, u_ref, halo_ref, attn_ref, gp_ref, ga_ref, wg_ref, sc_ref,
                wpo_ref, wao_ref, wout_ref, g2_ref, o_ref, h2_ref, *, tm, seq):
    t0 = (pl.program_id(0) * tm) % seq
    pooled = _pool_mixer(u_ref[...], halo_ref[...], t0, wg_ref, sc_ref)
    y_pool = _dot(pooled, wpo_ref[...])
    y_attn = _dot(attn_ref[...], wao_ref[...])
    mixed = gp_ref[...] * y_pool + ga_ref[...] * y_attn
    x1 = x_ref[...] + _dot(mixed.astype(BF16), wout_ref[...])
    o_ref[...] = x1
    h2_ref[...] = _rms_bf16(x1, g2_ref[...])


def _mix_call(x, u_pool, attn, gates, w_grp, scale, w_po, w_ao, w_out, g_ffn, tm, seq):
    m, d = x.shape
    width = u_pool.shape[1]
    halo_blocks = tm // POOL_HALO
    row = lambda i: (i, 0)
    fixed = lambda i: (0, 0)
    return pl.pallas_call(
        functools.partial(_mix_kernel, tm=tm, seq=seq),
        grid=(m // tm,),
        in_specs=[
            pl.BlockSpec((tm, d), row),
            pl.BlockSpec((tm, width), row),
            pl.BlockSpec((POOL_HALO, width), lambda i: (jnp.maximum(i * halo_blocks - 1, 0), 0)),
            pl.BlockSpec((tm, attn.shape[1]), row),
            pl.BlockSpec((tm, d), lambda i: (i, 0)),
            pl.BlockSpec((tm, d), lambda i: (i, 1)),
            pl.BlockSpec(w_grp.shape, lambda i: (0, 0, 0)),
            pl.BlockSpec((1, width), fixed),
            pl.BlockSpec(w_po.shape, fixed),
            pl.BlockSpec(w_ao.shape, fixed),
            pl.BlockSpec(w_out.shape, fixed),
            pl.BlockSpec((1, d), fixed),
        ],
        out_specs=[pl.BlockSpec((tm, d), row), pl.BlockSpec((tm, d), row)],
        out_shape=[jax.ShapeDtypeStruct((m, d), F32), jax.ShapeDtypeStruct((m, d), BF16)],
        compiler_params=_params("parallel"),
        name="mix",
    )(x, u_pool, u_pool, attn, gates, gates, w_grp, scale, w_po, w_ao, w_out, g_ffn)


def _ffn_kernel(h_ref, x_ref, wg_ref, wu_ref, wd_ref, o_ref, *, res_steps):
    j = pl.program_id(1)

    def step(first):
        h = h_ref[...]
        a = _dot(h, wg_ref[...].astype(BF16))
        u = _dot(h, wu_ref[...].astype(BF16))
        act = _silu(a) * u
        down = _dot(act.astype(BF16), wd_ref[...].astype(BF16))
        if first:
            o_ref[...] = down
        else:
            o_ref[...] += down

    pl.when(j == 0)(lambda: step(True))
    pl.when(j > 0)(lambda: step(False))

    slab = x_ref.shape[1]
    for c in range(res_steps):
        @pl.when(j == c)
        def _():
            o_ref[:, c * slab:(c + 1) * slab] += x_ref[...]


def _ffn_call(h, x, w_gate, w_up, w_down, tm, tf):
    m, d = x.shape
    f = w_gate.shape[1]
    res_steps = FFN_RESIDUAL_STEPS
    assert f // tf >= res_steps
    return pl.pallas_call(
        functools.partial(_ffn_kernel, res_steps=res_steps),
        grid=(m // tm, f // tf),
        in_specs=[
            pl.BlockSpec((tm, d), lambda i, j: (i, 0)),
            pl.BlockSpec((tm, d // res_steps), lambda i, j: (i, jnp.minimum(j, res_steps - 1))),
            pl.BlockSpec((d, tf), lambda i, j: (0, j)),
            pl.BlockSpec((d, tf), lambda i, j: (0, j)),
            pl.BlockSpec((tf, d), lambda i, j: (j, 0)),
        ],
        out_specs=pl.BlockSpec((tm, d), lambda i, j: (i, 0)),
        out_shape=jax.ShapeDtypeStruct((m, d), F32),
        compiler_params=pltpu.CompilerParams(dimension_semantics=("parallel", "arbitrary"),
                                             vmem_limit_bytes=BIG_VMEM_LIMIT),
        name="ffn",
    )(h, x, w_gate, w_up, w_down)


def kernel(x, positions, attn_norm_g, w_in, b_gate, q_a_norm_g, w_q_b, kv_a_norm_g, w_kv_b,
           q_norm_g, k_norm_g, w_attn_o, w_pool_grp, pool_scale, w_pool_o, w_out,
           ffn_norm_g, w_ffn_gate, w_ffn_up, w_ffn_down):
    b, s, d = x.shape
    depth = w_in.shape[0]
    m = b * s
    pool_width = w_pool_o.shape[1]

    inv_freq = ROPE_THETA ** (-jnp.arange(ROPE_HALF, dtype=F32) / ROPE_HALF)
    ang_t = positions.astype(F32).reshape(1, m) * inv_freq.reshape(ROPE_HALF, 1)
    cos_t = jnp.cos(ang_t)
    sin_t = jnp.sin(ang_t)

    xf = x.reshape(m, d)
    for l in range(depth):
        g_lat = jnp.concatenate([q_a_norm_g[l], kv_a_norm_g[l]]).reshape(1, -1)
        tq = ATTN_TILE
        gq = q_norm_g[l].reshape(QK_HEAD_DIM, 1)
        gk = k_norm_g[l].reshape(1, QK_HEAD_DIM)
        g_attn_norm = attn_norm_g[l].reshape(1, d)

        lat, u_pool, gates, kr = _inproj_call(xf, g_attn_norm, w_in[l].T, g_lat,
                                              b_gate[l].reshape(1, -1), pool_width, tm=INPROJ_ROWS)
        qt, k, vt = _qkv_call(lat, kr, cos_t, sin_t, w_q_b[l], w_kv_b[l], gq, gk, tm=tq)
        attn, w_po, w_ao, w_o = _flash_call(qt.reshape(b, s // tq, -1, tq), k.reshape(b, s, -1),
                                            vt.reshape(b, s // tq, -1, tq),
                                            [w_pool_o[l], w_attn_o[l], w_out[l]],
                                            tq=tq, heads=ATTN_HEADS_PER_STEP)
        xf, h2 = _mix_call(xf, u_pool, attn.reshape(m, -1), gates, w_pool_grp[l].astype(BF16),
                           pool_scale[l].reshape(1, -1), w_po, w_ao, w_o,
                           ffn_norm_g[l].reshape(1, d), tm=MIX_ROWS, seq=s)
        xf = _ffn_call(h2, xf, w_ffn_gate[l], w_ffn_up[l], w_ffn_down[l], tm=FFN_ROWS, tf=FFN_COLS)
    return xf.reshape(b, s, d)
```
